```python
import math
import jax
import jax.numpy as jnp
from jax import lax
import numpy as np

D_MODEL = 1024
BATCH = 8
SEQ = 2048
DEPTH = 2

CHUNK = 64
D_FF = 4096
LN_EPS = 1e-5
NORM_EPS = 1e-6
H_A = 4
DK_A = 128
DV_A = 128
CONV_K = 4
A_QK = H_A * DK_A
A_V = H_A * DV_A
H_B = 4
D_B = 128
B_PREV = 8
REL_CLIP = 128
B_W = H_B * D_B
H_C = 8
HKV_C = 2
G_C = H_C // HKV_C
D_C = 64
WINDOW = 128
C_PREV = WINDOW // CHUNK
C_Q = H_C * D_C
C_KV = HKV_C * D_C
N_BRANCH = 3
IN_SPLITS = (A_QK, A_QK, A_V, H_A, H_A, A_V, B_W, B_W, B_W, C_Q, C_KV, C_KV)
D_IN = sum(IN_SPLITS)

kernel_name = 'hybrid_chunk_streaming_block'


def layer_norm(x, g, b):
    xf = x.astype(jnp.float32)
    mu = jnp.mean(xf, -1, keepdims=True)
    xc = xf - mu
    var = jnp.mean(xc * xc, -1, keepdims=True)
    return (xc * lax.rsqrt(var + LN_EPS) * g.astype(jnp.float32) + b.astype(jnp.float32)).astype(x.dtype)


def swiglu(x, w_in, w_out):
    gate, up = jnp.split(x @ w_in, 2, axis=-1)
    return (jax.nn.silu(gate) * up) @ w_out


def causal_depthwise_conv(x, w):
    k = w.shape[0]
    return lax.conv_general_dilated(x, w[:, None, :], window_strides=(1,), padding=[(k - 1, 0)],
                                    dimension_numbers=('NWC', 'WIO', 'NWC'),
                                    feature_group_count=x.shape[-1])


def l2_normalize(x):
    return x * lax.rsqrt(jnp.sum(x * x, -1, keepdims=True) + NORM_EPS)


def to_chunks(a):
    b, t, h = a.shape[:3]
    a = a.reshape(b, t // CHUNK, CHUNK, h, *a.shape[3:])
    return jnp.moveaxis(a, (1, 3), (0, 2))


def chunk_gated_delta_rule(q, k, v, g, beta):
    b, t, h, dk = q.shape
    dv = v.shape[-1]
    q = to_chunks(q) * (dk ** -0.5)
    k = to_chunks(k)
    v = to_chunks(v)
    g = to_chunks(g)
    beta = to_chunks(beta)
    g_cum = jnp.cumsum(g, axis=-1)
    idx = jnp.arange(CHUNK)
    causal = idx[:, None] >= idx[None, :]
    strict = idx[:, None] > idx[None, :]
    decay = jnp.exp(jnp.where(causal, g_cum[..., :, None] - g_cum[..., None, :], -jnp.inf))
    k_beta = k * beta[..., None]
    lower = jnp.where(strict, jnp.einsum('nbhid,nbhjd->nbhij', k_beta, k) * decay, 0.0)
    eye = jnp.eye(CHUNK, dtype=jnp.float32)
    rhs = jnp.concatenate([v * beta[..., None], k_beta * jnp.exp(g_cum)[..., None]], axis=-1)
    uw = lax.linalg.triangular_solve(eye + lower, rhs, left_side=True, lower=True, unit_diagonal=True)
    u, w = uw[..., :dv], uw[..., dv:]
    attn = jnp.einsum('nbhid,nbhjd->nbhij', q, k) * decay
    q_dec = q * jnp.exp(g_cum)[..., None]
    g_last = g_cum[..., -1]
    k_dec = k * jnp.exp(g_last[..., None] - g_cum)[..., None]

    def step(state, inp):
        q_c, k_c, u_c, w_c, a_c, gl_c = inp
        v_new = u_c - jnp.einsum('bhcd,bhdv->bhcv', w_c, state)
        o_c = jnp.einsum('bhcd,bhdv->bhcv', q_c, state) + jnp.einsum('bhij,bhjv->bhiv', a_c, v_new)
        state = state * jnp.exp(gl_c)[..., None, None] + jnp.einsum('bhcd,bhcv->bhdv', k_c, v_new)
        return state, o_c

    s0 = jnp.zeros((b, h, dk, dv), jnp.float32)
    _, o = lax.scan(step, s0, (q_dec, k_dec, u, w, attn, g_last))
    o = jnp.moveaxis(o, (0, 2), (1, 3))
    return o.reshape(b, t, h, dv)


def gated_deltanet(qa, ka, va, beta_raw, a_raw, za, conv_w, a_log, dt_bias, norm_g):
    b, t, _ = qa.shape
    f32 = jnp.float32
    qkv = jax.nn.silu(causal_depthwise_conv(jnp.concatenate([qa, ka, va], -1), conv_w)).astype(f32)
    q, k, v = jnp.split(qkv, [A_QK, 2 * A_QK], axis=-1)
    q = l2_normalize(q.reshape(b, t, H_A, DK_A))
    k = l2_normalize(k.reshape(b, t, H_A, DK_A))
    v = v.reshape(b, t, H_A, DV_A)
    beta = jax.nn.sigmoid(beta_raw.astype(f32))
    g = -jnp.exp(a_log.astype(f32)) * jax.nn.softplus(a_raw.astype(f32) + dt_bias.astype(f32))
    o = chunk_gated_delta_rule(q, k, v, g, beta)
    z = za.reshape(b, t, H_A, DV_A).astype(f32)
    o = o * lax.rsqrt(jnp.mean(o * o, -1, keepdims=True) + NORM_EPS) * norm_g.astype(f32) * jax.nn.silu(z)
    return o.reshape(b, t, A_V).astype(qa.dtype)


def chunk_band(a, n_prev):
    b, t = a.shape[:2]
    n = t // CHUNK
    pad = [(0, 0), (n_prev * CHUNK, 0)] + [(0, 0)] * (a.ndim - 2)
    ap = jnp.pad(a, pad).reshape(b, n + n_prev, CHUNK, *a.shape[2:])
    band = jnp.stack([ap[:, j:j + n] for j in range(n_prev + 1)], axis=2)
    return band.reshape(b, n, (n_prev + 1) * CHUNK, *a.shape[2:])


def band_geometry(n, n_prev):
    i = jnp.arange(CHUNK)
    j = jnp.arange((n_prev + 1) * CHUNK)
    dist = n_prev * CHUNK + i[:, None] - j[None, :]
    valid = (jnp.arange(n)[:, None] - n_prev + j[None, :] // CHUNK) >= 0
    return dist, valid


def alibi_slopes(n):
    return 2.0 ** (-8.0 * jnp.arange(1, n + 1, dtype=jnp.float32) / n)


def chunk_relpos_attention(q, k, v, rel_bias):
    b, t, _ = q.shape
    n = t // CHUNK
    q = q.reshape(b, n, CHUNK, H_B, D_B)
    kb = chunk_band(k.reshape(b, t, H_B, D_B), B_PREV)
    vb = chunk_band(v.reshape(b, t, H_B, D_B), B_PREV)
    s = jnp.einsum('bnqhd,bnkhd->bnhqk', q, kb).astype(jnp.float32) * (D_B ** -0.5)
    dist, valid = band_geometry(n, B_PREV)
    rel_idx = jnp.clip(dist, -(CHUNK - 1), REL_CLIP) + (CHUNK - 1)
    bias = rel_bias.astype(jnp.float32)[:, rel_idx]
    s = jnp.where(valid[None, :, None, None, :], s + bias, -jnp.inf)
    p = jax.nn.softmax(s, axis=-1).astype(v.dtype)
    o = jnp.einsum('bnhqk,bnkhd->bnqhd', p, vb)
    return o.reshape(b, t, B_W)


def swa_sink_attention(q, k, v, sinks):
    b, t, _ = q.shape
    n = t // CHUNK
    q = q.reshape(b, n, CHUNK, HKV_C, G_C, D_C)
    kb = chunk_band(k.reshape(b, t, HKV_C, D_C), C_PREV)
    vb = chunk_band(v.reshape(b, t, HKV_C, D_C), C_PREV)
    s = jnp.einsum('bnqhgd,bnkhd->bnhgqk', q, kb).astype(jnp.float32) * (D_C ** -0.5)
    dist, valid = band_geometry(n, C_PREV)
    slopes = alibi_slopes(H_C).reshape(HKV_C, G_C)
    s = s - slopes[:, :, None, None] * jnp.abs(dist).astype(jnp.float32)
    s = jnp.where(valid[None, :, None, None, None, :], s, -jnp.inf)
    sink = jnp.broadcast_to(sinks.astype(jnp.float32).reshape(HKV_C, G_C)[:, :, None, None], s.shape[:-1] + (1,))
    p = jax.nn.softmax(jnp.concatenate([s, sink], axis=-1), axis=-1)[..., :-1].astype(v.dtype)
    o = jnp.einsum('bnhgqk,bnkhd->bnqhgd', p, vb)
    return o.reshape(b, t, C_Q)


def hybrid_mixer(x, w_in, b_in, conv_w, a_log, dt_bias, gdn_norm_g, rel_bias, sinks,
                 w_gate, b_gate, w_br_a, w_br_b, w_br_c, w_out):
    b, t, d = x.shape
    proj = x @ w_in + b_in
    split_at = np.cumsum(IN_SPLITS)[:-1].tolist()
    (qa, ka, va, beta_raw, a_raw, za, qb, kb, vb, qc, kc, vc) = jnp.split(proj, split_at, axis=-1)
    o_a = gated_deltanet(qa, ka, va, beta_raw, a_raw, za, conv_w, a_log, dt_bias, gdn_norm_g)
    o_b = chunk_relpos_attention(qb, kb, vb, rel_bias)
    o_c = swa_sink_attention(qc, kc, vc, sinks)
    gates = jax.nn.sigmoid(x @ w_gate + b_gate).reshape(b, t, N_BRANCH, d)
    merged = (gates[:, :, 0] * (o_a @ w_br_a) + gates[:, :, 1] * (o_b @ w_br_b)
              + gates[:, :, 2] * (o_c @ w_br_c))
    return merged @ w_out


def setup_inputs(seed: int = 0) -> dict:
    key = jax.random.key(seed)
    keys = list(jax.random.split(key, 32))
    f32 = jnp.float32
    L, D = DEPTH, D_MODEL
    sub_scale = (8.0 * DEPTH) ** -0.25

    def nrm(shape, scale):
        return jax.random.normal(keys.pop(), shape, f32) * scale

    def gain(shape):
        return 1.0 + nrm(shape, 0.02)

    x = nrm((BATCH, SEQ, D), 1.0)
    ln1_g = gain((L, D))
    ln1_b = nrm((L, D), 0.02)
    w_ff1_in = nrm((L, D, 2 * D_FF), D ** -0.5)
    w_ff1_out = nrm((L, D_FF, D), D_FF ** -0.5 * sub_scale)
    w_in = nrm((L, D, D_IN), D ** -0.5)
    b_in = nrm((L, D_IN), 0.02)
    conv_w = nrm((L, CONV_K, 2 * A_QK + A_V), CONV_K ** -0.5)
    a_log = jnp.log(jax.random.uniform(keys.pop(), (L, H_A), f32, 1.0, 16.0))
    dt = jnp.exp(jax.random.uniform(keys.pop(), (L, H_A), f32, math.log(1e-3), math.log(1e-1)))
    dt_bias = dt + jnp.log(-jnp.expm1(-dt))
    gdn_norm_g = gain((L, DV_A))
    rel_bias = nrm((L, H_B, CHUNK + REL_CLIP), 0.1)
    sinks = nrm((L, H_C), 0.5)
    w_gate = nrm((L, D, N_BRANCH * D), D ** -0.5)
    b_gate = nrm((L, N_BRANCH * D), 0.02)
    w_br_a = nrm((L, A_V, D), A_V ** -0.5 * sub_scale)
    w_br_b = nrm((L, B_W, D), B_W ** -0.5 * sub_scale)
    w_br_c = nrm((L, C_Q, D), C_Q ** -0.5 * sub_scale)
    w_out = nrm((L, D, D), D ** -0.5 * sub_scale)
    ln2_g = gain((L, D))
    ln2_b = nrm((L, D), 0.02)
    w_ff2_in = nrm((L, D, 2 * D_FF), D ** -0.5)
    w_ff2_out = nrm((L, D_FF, D), D_FF ** -0.5 * sub_scale)
    ln3_g = gain((L, D))
    ln3_b = nrm((L, D), 0.02)
    return {'x': x, 'ln1_g': ln1_g, 'ln1_b': ln1_b, 'w_ff1_in': w_ff1_in, 'w_ff1_out': w_ff1_out,
            'w_in': w_in, 'b_in': b_in, 'conv_w': conv_w, 'a_log': a_log, 'dt_bias': dt_bias,
            'gdn_norm_g': gdn_norm_g, 'rel_bias': rel_bias, 'sinks': sinks, 'w_gate': w_gate,
            'b_gate': b_gate, 'w_br_a': w_br_a, 'w_br_b': w_br_b, 'w_br_c': w_br_c, 'w_out': w_out,
            'ln2_g': ln2_g, 'ln2_b': ln2_b, 'w_ff2_in': w_ff2_in, 'w_ff2_out': w_ff2_out,
            'ln3_g': ln3_g, 'ln3_b': ln3_b}


def reference(x, ln1_g, ln1_b, w_ff1_in, w_ff1_out, w_in, b_in, conv_w, a_log, dt_bias,
              gdn_norm_g, rel_bias, sinks, w_gate, b_gate, w_br_a, w_br_b, w_br_c, w_out,
              ln2_g, ln2_b, w_ff2_in, w_ff2_out, ln3_g, ln3_b):
    alpha = (2.0 * DEPTH) ** 0.25
    for l in range(DEPTH):
        x = layer_norm(alpha * x + 0.5 * swiglu(x, w_ff1_in[l], w_ff1_out[l]), ln1_g[l], ln1_b[l])
        mix = hybrid_mixer(x, w_in[l], b_in[l], conv_w[l], a_log[l], dt_bias[l], gdn_norm_g[l],
                           rel_bias[l], sinks[l], w_gate[l], b_gate[l], w_br_a[l], w_br_b[l],
                           w_br_c[l], w_out[l])
        x = layer_norm(alpha * x + mix, ln2_g[l], ln2_b[l])
        x = layer_norm(alpha * x + 0.5 * swiglu(x, w_ff2_in[l], w_ff2_out[l]), ln3_g[l], ln3_b[l])
    return x
```

```python
import functools
import math

import jax
import jax.numpy as jnp
import numpy as np
from jax import lax
from jax.experimental import pallas as pl
from jax.experimental.pallas import tpu as pltpu

F32 = jnp.float32
BF16 = jnp.bfloat16

D_MODEL = 1024
DEPTH = 2
CHUNK = 64
D_FF = 4096
LN_EPS = 1e-5
NORM_EPS = 1e-6
H_A, DK_A, DV_A, CONV_K = 4, 128, 128, 4
A_QK = H_A * DK_A
A_V = H_A * DV_A
H_B, D_B, B_PREV, REL_CLIP = 4, 128, 8, 128
B_W = H_B * D_B
H_C, HKV_C, D_C, WINDOW = 8, 2, 64, 128
G_C = H_C // HKV_C
C_PREV = WINDOW // CHUNK
C_Q = H_C * D_C
C_KV = HKV_C * D_C
N_BRANCH = 3
ALPHA = (2.0 * DEPTH) ** 0.25

LANES = 128
VMEM_LIMIT = 56 * 1024 * 1024
NEG_BIG = -1e30

SMALL_W = LANES
QKV_A = 2 * A_QK + A_V
A_W = QKV_A + A_V
C_W = C_Q + 2 * C_KV


def _dot(a, b):
    return jnp.dot(a, b, preferred_element_type=F32)


def _dot_nt(a, b):
    return lax.dot_general(a, b, (((1,), (1,)), ((), ())), preferred_element_type=F32)


def _layer_norm(y, g, b):
    mu = jnp.mean(y, axis=-1, keepdims=True)
    yc = y - mu
    var = jnp.mean(yc * yc, axis=-1, keepdims=True)
    return yc * lax.rsqrt(var + LN_EPS) * g + b


def _silu(x):
    return x * jax.nn.sigmoid(x)


def _params(*sem):
    return pltpu.CompilerParams(dimension_semantics=sem, vmem_limit_bytes=VMEM_LIMIT)


FFN_TM = 1024
FFN_TF = 512


def _ffn_kernel(x_ref, wg_ref, wu_ref, wo_ref, g_ref, b_ref, o_ref, xb_ref, acc_ref):
    f = pl.program_id(1)

    @pl.when(f == 0)
    def _():
        xb_ref[...] = x_ref[...].astype(BF16)
        acc_ref[...] = jnp.zeros_like(acc_ref)

    xb = xb_ref[...]
    gate = _dot(xb, wg_ref[...])
    up = _dot(xb, wu_ref[...])
    h = (_silu(gate) * up).astype(BF16)
    acc_ref[...] += _dot(h, wo_ref[...])

    @pl.when(f == pl.num_programs(1) - 1)
    def _():
        y = ALPHA * x_ref[...] + 0.5 * acc_ref[...]
        o_ref[...] = _layer_norm(y, g_ref[...], b_ref[...])


def _ffn(x, w_in, w_out, g, b):
    n = x.shape[0]
    nf = D_FF // FFN_TF
    return pl.pallas_call(
        _ffn_kernel,
        grid=(n // FFN_TM, nf),
        in_specs=[
            pl.BlockSpec((FFN_TM, D_MODEL), lambda i, f: (i, 0)),
            pl.BlockSpec((D_MODEL, FFN_TF), lambda i, f: (0, f)),
            pl.BlockSpec((D_MODEL, FFN_TF), lambda i, f: (0, f + nf)),
            pl.BlockSpec((FFN_TF, D_MODEL), lambda i, f: (f, 0)),
            pl.BlockSpec((1, D_MODEL), lambda i, f: (0, 0)),
            pl.BlockSpec((1, D_MODEL), lambda i, f: (0, 0)),
        ],
        out_specs=pl.BlockSpec((FFN_TM, D_MODEL), lambda i, f: (i, 0)),
        out_shape=jax.ShapeDtypeStruct((n, D_MODEL), F32),
        scratch_shapes=[pltpu.VMEM((FFN_TM, D_MODEL), BF16), pltpu.VMEM((FFN_TM, D_MODEL), F32)],
        compiler_params=_params("parallel", "arbitrary"),
        name="ffn_ln",
    )(x, w_in, w_in, w_out, g, b)


PROJ_TM = 512


def _proj_kernel(x_ref, wa_ref, ws_ref, wb_ref, wc_ref, ba_ref, bs_ref, bb_ref, bc_ref,
                 oa_ref, os_ref, ob_ref, oc_ref):
    xb = x_ref[...].astype(BF16)
    oa_ref[...] = _dot(xb, wa_ref[...]) + ba_ref[...]
    os_ref[...] = _dot(xb, ws_ref[...]) + bs_ref[...]
    ob_ref[...] = (_dot(xb, wb_ref[...]) + bb_ref[...]).astype(BF16)
    oc_ref[...] = (_dot(xb, wc_ref[...]) + bc_ref[...]).astype(BF16)


def _proj(x, wa, ws, wb, wc, ba, bs, bb, bc):
    n = x.shape[0]
    widths = (A_W, SMALL_W, 3 * B_W, C_W)
    row = lambda w: pl.BlockSpec((PROJ_TM, w), lambda i: (i, 0))
    const = lambda r, w: pl.BlockSpec((r, w), lambda i: (0, 0))
    return pl.pallas_call(
        _proj_kernel,
        grid=(n // PROJ_TM,),
        in_specs=[row(D_MODEL)] + [const(D_MODEL, w) for w in widths] + [const(1, w) for w in widths],
        out_specs=[row(w) for w in widths],
        out_shape=[jax.ShapeDtypeStruct((n, A_W), F32), jax.ShapeDtypeStruct((n, SMALL_W), F32),
                   jax.ShapeDtypeStruct((n, 3 * B_W), BF16), jax.ShapeDtypeStruct((n, C_W), BF16)],
        compiler_params=_params("parallel"),
        name="in_proj",
    )(x, wa, ws, wb, wc, ba, bs, bb, bc)


GDN_R = 256
GDN_NC = GDN_R // CHUNK
CONV_PAD = 8
LOG2_CHUNK = CHUNK.bit_length() - 1
NEUMANN_STEPS = LOG2_CHUNK - 1


def _gdn_kernel(a_ref, s_ref, cw_ref, alog_ref, dtb_ref, ng_ref, o_ref, xbuf, st_ref):
    i = pl.program_id(1)

    @pl.when(i == 0)
    def _():
        st_ref[...] = jnp.zeros_like(st_ref)
        xbuf[0:CONV_PAD, :] = jnp.zeros((CONV_PAD, QKV_A), F32)

    xbuf[CONV_PAD:CONV_PAD + GDN_R, :] = a_ref[:, 0:QKV_A]
    base = CONV_PAD - (CONV_K - 1)
    acc = cw_ref[0:1, :] * xbuf[base:base + GDN_R, :]
    for j in range(1, CONV_K):
        acc = acc + cw_ref[j:j + 1, :] * xbuf[base + j:base + j + GDN_R, :]
    xbuf[0:CONV_PAD, :] = a_ref[GDN_R - CONV_PAD:GDN_R, 0:QKV_A]
    qkv = _silu(acc)

    s = s_ref[...]
    beta_all = jax.nn.sigmoid(s)
    g_all = -jnp.exp(alog_ref[...]) * jax.nn.softplus(s + dtb_ref[...])
    row = lax.broadcasted_iota(jnp.int32, (GDN_R, GDN_R), 0)
    col = lax.broadcasted_iota(jnp.int32, (GDN_R, GDN_R), 1)
    same_chunk = jnp.right_shift(row, LOG2_CHUNK) == jnp.right_shift(col, LOG2_CHUNK)
    tri = jnp.where(same_chunk & (col <= row), 1.0, 0.0).astype(F32)
    gcum = jnp.dot(tri, g_all, preferred_element_type=F32, precision=lax.Precision.HIGHEST)
    gcum_t = gcum.T

    ri = lax.broadcasted_iota(jnp.int32, (CHUNK, CHUNK), 0)
    ci = lax.broadcasted_iota(jnp.int32, (CHUNK, CHUNK), 1)
    causal = ri >= ci
    strict = ri > ci
    eye = jnp.where(ri == ci, 1.0, 0.0).astype(F32)
    ng = ng_ref[...]

    for h in range(H_A):
        lo = h * DK_A
        qh = qkv[:, lo:lo + DK_A]
        kh = qkv[:, A_QK + lo:A_QK + lo + DK_A]
        vh = qkv[:, 2 * A_QK + lo:2 * A_QK + lo + DV_A]
        qn = qh * (lax.rsqrt(jnp.sum(qh * qh, -1, keepdims=True) + NORM_EPS) * (DK_A ** -0.5))
        kn = kh * lax.rsqrt(jnp.sum(kh * kh, -1, keepdims=True) + NORM_EPS)
        bcol = beta_all[:, h:h + 1]
        gcol = gcum[:, H_A + h:H_A + h + 1]
        eg = jnp.exp(gcol)
        kb = kn * bcol
        rhs_all = jnp.concatenate([vh * bcol, kb * eg], axis=1).astype(BF16)
        qd = (qn * eg).astype(BF16)
        qn_b = qn.astype(BF16)
        kn_b = kn.astype(BF16)
        kb_b = kb.astype(BF16)
        state = st_ref[h]
        for c in range(GDN_NC):
            r0 = c * CHUNK
            gc = gcol[r0:r0 + CHUNK]
            gr = gcum_t[H_A + h:H_A + h + 1, r0:r0 + CHUNK]
            decay = jnp.exp(jnp.where(causal, gc - gr, -jnp.inf))
            k_c = kn_b[r0:r0 + CHUNK]
            kk = _dot_nt(kb_b[r0:r0 + CHUNK], k_c)
            m = jnp.where(strict, -(kk * decay), 0.0)
            inv = eye + m
            p = m
            for _ in range(NEUMANN_STEPS):
                pb = p.astype(BF16)
                p = _dot(pb, pb)
                inv = inv + _dot(p.astype(BF16), inv.astype(BF16))
            uw = _dot(inv.astype(BF16), rhs_all[r0:r0 + CHUNK])
            u = uw[:, 0:DV_A]
            w = uw[:, DV_A:DV_A + DK_A]
            attn = _dot_nt(qn_b[r0:r0 + CHUNK], k_c) * decay
            g_last = gcol[r0 + CHUNK - 1:r0 + CHUNK]
            k_dec = kn[r0:r0 + CHUNK] * jnp.exp(g_last - gc)
            state_b = state.astype(BF16)
            v_new = u - _dot(w.astype(BF16), state_b)
            v_new_b = v_new.astype(BF16)
            o = _dot(qd[r0:r0 + CHUNK], state_b) + _dot(attn.astype(BF16), v_new_b)
            state = state * jnp.exp(g_last) + _dot(k_dec.T.astype(BF16), v_new_b)
            z = a_ref[r0:r0 + CHUNK, QKV_A + lo:QKV_A + lo + DV_A]
            on = o * lax.rsqrt(jnp.mean(o * o, -1, keepdims=True) + NORM_EPS) * ng * _silu(z)
            o_ref[r0:r0 + CHUNK, lo:lo + DV_A] = on.astype(BF16)
        st_ref[h] = state


def _gdn(a, s, conv_w, alog_row, dtb_row, norm_g, batch, seq):
    n = a.shape[0]
    nblk = seq // GDN_R
    const = lambda r, w: pl.BlockSpec((r, w), lambda b, i: (0, 0))
    return pl.pallas_call(
        _gdn_kernel,
        grid=(batch, nblk),
        in_specs=[
            pl.BlockSpec((GDN_R, A_W), lambda b, i: (b * nblk + i, 0)),
            pl.BlockSpec((GDN_R, SMALL_W), lambda b, i: (b * nblk + i, 0)),
            const(CONV_K, QKV_A), const(1, SMALL_W), const(1, SMALL_W), const(1, DV_A),
        ],
        out_specs=pl.BlockSpec((GDN_R, A_V), lambda b, i: (b * nblk + i, 0)),
        out_shape=jax.ShapeDtypeStruct((n, A_V), BF16),
        scratch_shapes=[pltpu.VMEM((CONV_PAD + GDN_R, QKV_A), F32),
                        pltpu.VMEM((H_A, DK_A, DV_A), F32)],
        compiler_params=_params("parallel", "arbitrary"),
        name="gdn",
    )(a, s, conv_w, alog_row, dtb_row, norm_g)


ATT_R = 256
ATT_CH = ATT_R // CHUNK
B_KBLK = 3
assert (B_KBLK - 1) * ATT_CH == B_PREV


def _attn_b_kernel(q_ref, k0_ref, k1_ref, k2_ref, v0_ref, v1_ref, v2_ref, bias_ref, o_ref):
    i = pl.program_id(1)
    k_refs = (k0_ref, k1_ref, k2_ref)
    v_refs = (v0_ref, v1_ref, v2_ref)
    scale = D_B ** -0.5
    for h in range(H_B):
        lo = h * D_B
        q = q_ref[:, lo:lo + D_B]
        parts = []
        for j in range(B_KBLK):
            sj = _dot_nt(q, k_refs[j][:, lo:lo + D_B]) * scale + bias_ref[h, j]
            if j < B_KBLK - 1:
                sj = jnp.where(i + j >= B_KBLK - 1, sj, NEG_BIG)
            parts.append(sj)
        s = jnp.concatenate(parts, axis=1)
        mx = jnp.max(s, axis=-1, keepdims=True)
        p = jnp.exp(s - mx)
        denom = jnp.sum(p, axis=-1, keepdims=True)
        pb = p.astype(BF16)
        o = _dot(pb[:, 0:ATT_R], v_refs[0][:, lo:lo + D_B])
        for j in range(1, B_KBLK):
            o = o + _dot(pb[:, j * ATT_R:(j + 1) * ATT_R], v_refs[j][:, lo:lo + D_B])
        o_ref[:, lo:lo + D_B] = (o / denom).astype(BF16)


def _attn_b(qkv, bias, batch, seq):
    n = qkv.shape[0]
    nblk = seq // ATT_R

    def kv_spec(j, colblk):
        return pl.BlockSpec(
            (ATT_R, B_W), lambda b, i: (b * nblk + jnp.maximum(i + j - (B_KBLK - 1), 0), colblk))

    return pl.pallas_call(
        _attn_b_kernel,
        grid=(batch, nblk),
        in_specs=[pl.BlockSpec((ATT_R, B_W), lambda b, i: (b * nblk + i, 0))]
        + [kv_spec(j, 1) for j in range(B_KBLK)] + [kv_spec(j, 2) for j in range(B_KBLK)]
        + [pl.BlockSpec((H_B, B_KBLK, ATT_R, ATT_R), lambda b, i: (0, 0, 0, 0))],
        out_specs=pl.BlockSpec((ATT_R, B_W), lambda b, i: (b * nblk + i, 0)),
        out_shape=jax.ShapeDtypeStruct((n, B_W), BF16),
        compiler_params=_params("parallel", "parallel"),
        name="attn_b",
    )(qkv, qkv, qkv, qkv, qkv, qkv, qkv, bias)


def _attn_b_bias(rel_bias):
    r = np.arange(ATT_R)
    qchunk = r // CHUNK + B_PREV
    out = []
    for j in range(B_KBLK):
        kchunk = j * ATT_CH + r // CHUNK
        delta = qchunk[:, None] - kchunk[None, :]
        dist = delta * CHUNK + (r % CHUNK)[:, None] - (r % CHUNK)[None, :]
        idx = np.clip(dist, -(CHUNK - 1), REL_CLIP) + (CHUNK - 1)
        vis = (delta >= 0) & (delta <= B_PREV)
        out.append(jnp.where(jnp.asarray(vis)[None], rel_bias[:, idx], NEG_BIG))
    return jnp.stack(out, axis=1).astype(F32)


C_KR = 2 * CHUNK
C_KBLK = 3


def _attn_c_kernel(sink_ref, q_ref, k0_ref, k1_ref, k2_ref, v0_ref, v1_ref, v2_ref, o_ref):
    i = pl.program_id(1)
    k_refs = (k0_ref, k1_ref, k2_ref)
    v_refs = (v0_ref, v1_ref, v2_ref)
    scale = D_C ** -0.5
    tq = lax.broadcasted_iota(jnp.int32, (ATT_R, C_KR), 0) + C_KR
    dist_f, vis = [], []
    for j in range(C_KBLK):
        tk = lax.broadcasted_iota(jnp.int32, (ATT_R, C_KR), 1) + j * C_KR
        delta = jnp.right_shift(tq, LOG2_CHUNK) - jnp.right_shift(tk, LOG2_CHUNK)
        ok = (delta >= 0) & (delta <= C_PREV)
        if j == 0:
            ok = ok & (i > 0)
        vis.append(ok)
        dist_f.append(jnp.abs(tq - tk).astype(F32))
    for h in range(H_C):
        g = h // G_C
        slope = 2.0 ** (-8.0 * (h + 1) / H_C)
        q = q_ref[:, h * D_C:(h + 1) * D_C]
        parts = []
        for j in range(C_KBLK):
            sj = _dot_nt(q, k_refs[j][:, g * D_C:(g + 1) * D_C]) * scale - slope * dist_f[j]
            parts.append(jnp.where(vis[j], sj, NEG_BIG))
        s = jnp.concatenate(parts, axis=1)
        sink = sink_ref[h]
        mx = jnp.maximum(jnp.max(s, axis=-1, keepdims=True), sink)
        p = jnp.exp(s - mx)
        denom = jnp.sum(p, axis=-1, keepdims=True) + jnp.exp(sink - mx)
        pb = p.astype(BF16)
        o = _dot(pb[:, 0:C_KR], v_refs[0][:, g * D_C:(g + 1) * D_C])
        for j in range(1, C_KBLK):
            o = o + _dot(pb[:, j * C_KR:(j + 1) * C_KR], v_refs[j][:, g * D_C:(g + 1) * D_C])
        o_ref[:, h * D_C:(h + 1) * D_C] = (o / denom).astype(BF16)


def _attn_c(qkv, sinks, batch, seq):
    n = qkv.shape[0]
    nblk = seq // ATT_R
    kper = ATT_R // C_KR
    kcol, vcol = C_Q // C_KV, C_Q // C_KV + 1

    def kv_spec(j, colblk):
        return pl.BlockSpec(
            (C_KR, C_KV),
            lambda b, i: (b * nblk * kper + jnp.maximum(i * kper + j - 1, 0), colblk))

    return pl.pallas_call(
        _attn_c_kernel,
        grid=(batch, nblk),
        in_specs=[pl.BlockSpec(memory_space=pltpu.SMEM),
                  pl.BlockSpec((ATT_R, C_Q), lambda b, i: (b * nblk + i, 0))]
        + [kv_spec(j, kcol) for j in range(C_KBLK)] + [kv_spec(j, vcol) for j in range(C_KBLK)],
        out_specs=pl.BlockSpec((ATT_R, C_Q), lambda b, i: (b * nblk + i, 0)),
        out_shape=jax.ShapeDtypeStruct((n, C_Q), BF16),
        compiler_params=_params("parallel", "parallel"),
        name="attn_c",
    )(sinks, qkv, qkv, qkv, qkv, qkv, qkv, qkv)


MERGE_TM = 512


def _merge_kernel(x_ref, oa_ref, ob_ref, oc_ref, wg_ref, bg_ref, wa_ref, wb_ref, wc_ref, wo_ref,
                  g_ref, b_ref, o_ref):
    x = x_ref[...]
    xb = x.astype(BF16)
    merged = None
    for r, (br_ref, w_ref) in enumerate(((oa_ref, wa_ref), (ob_ref, wb_ref), (oc_ref, wc_ref))):
        lo = r * D_MODEL
        gate = jax.nn.sigmoid(_dot(xb, wg_ref[:, lo:lo + D_MODEL]) + bg_ref[:, lo:lo + D_MODEL])
        term = gate * _dot(br_ref[...], w_ref[...])
        merged = term if merged is None else merged + term
    y = ALPHA * x + _dot(merged.astype(BF16), wo_ref[...])
    o_ref[...] = _layer_norm(y, g_ref[...], b_ref[...])


def _merge(x, oa, ob, oc, w_gate, b_gate, w_a, w_b, w_c, w_out, g, b):
    n = x.shape[0]
    row = lambda w: pl.BlockSpec((MERGE_TM, w), lambda i: (i, 0))
    const = lambda r, w: pl.BlockSpec((r, w), lambda i: (0, 0))
    return pl.pallas_call(
        _merge_kernel,
        grid=(n // MERGE_TM,),
        in_specs=[row(D_MODEL), row(A_V), row(B_W), row(C_Q),
                  const(D_MODEL, N_BRANCH * D_MODEL), const(1, N_BRANCH * D_MODEL),
                  const(A_V, D_MODEL), const(B_W, D_MODEL), const(C_Q, D_MODEL),
                  const(D_MODEL, D_MODEL), const(1, D_MODEL), const(1, D_MODEL)],
        out_specs=row(D_MODEL),
        out_shape=jax.ShapeDtypeStruct((n, D_MODEL), F32),
        compiler_params=_params("parallel"),
        name="merge_ln",
    )(x, oa, ob, oc, w_gate, b_gate, w_a, w_b, w_c, w_out, g, b)


def _lane_row(vals, offset):
    return jnp.zeros((1, LANES), F32).at[0, offset:offset + vals.shape[0]].set(vals.astype(F32))


def kernel(x, ln1_g, ln1_b, w_ff1_in, w_ff1_out, w_in, b_in, conv_w, a_log, dt_bias, gdn_norm_g,
           rel_bias, sinks, w_gate, b_gate, w_br_a, w_br_b, w_br_c, w_out, ln2_g, ln2_b,
           w_ff2_in, w_ff2_out, ln3_g, ln3_b):
    batch, seq, d = x.shape
    assert d == D_MODEL and seq % ATT_R == 0 and (batch * seq) % FFN_TM == 0
    h = x.reshape(batch * seq, d)
    row = lambda v: v.reshape(1, -1).astype(F32)
    o_small = QKV_A
    o_z = o_small + 2 * H_A
    o_b = o_z + A_V
    o_c = o_b + 3 * B_W
    for l in range(DEPTH):
        h = _ffn(h, w_ff1_in[l].astype(BF16), w_ff1_out[l].astype(BF16), row(ln1_g[l]), row(ln1_b[l]))
        wi, bi = w_in[l], b_in[l]
        pad = SMALL_W - 2 * H_A
        wa = jnp.concatenate([wi[:, 0:QKV_A], wi[:, o_z:o_b]], axis=1).astype(BF16)
        ws = jnp.pad(wi[:, o_small:o_z], ((0, 0), (0, pad))).astype(BF16)
        ba = jnp.concatenate([bi[0:QKV_A], bi[o_z:o_b]])
        bs = jnp.pad(bi[o_small:o_z], (0, pad))
        pa, ps, pb, pc = _proj(h, wa, ws, wi[:, o_b:o_c].astype(BF16), wi[:, o_c:].astype(BF16),
                               row(ba), row(bs), row(bi[o_b:o_c]), row(bi[o_c:]))
        o_a = _gdn(pa, ps, conv_w[l].astype(F32), _lane_row(a_log[l], H_A), _lane_row(dt_bias[l], H_A),
                   row(gdn_norm_g[l]), batch, seq)
        o_bb = _attn_b(pb, _attn_b_bias(rel_bias[l]), batch, seq)
        o_cc = _attn_c(pc, sinks[l].astype(F32), batch, seq)
        h = _merge(h, o_a, o_bb, o_cc, w_gate[l].astype(BF16), row(b_gate[l]),
                   w_br_a[l].astype(BF16), w_br_b[l].astype(BF16), w_br_c[l].astype(BF16),
                   w_out[l].astype(BF16), row(ln2_g[l]), row(ln2_b[l]))
        h = _ffn(h, w_ff2_in[l].astype(BF16), w_ff2_out[l].astype(BF16), row(ln3_g[l]), row(ln3_b[l]))
    return h.reshape(batch, seq, d)
```

```python
import functools
import math

import jax
import jax.numpy as jnp
import numpy as np
from jax import lax
from jax.experimental import pallas as pl
from jax.experimental.pallas import tpu as pltpu

F32 = jnp.float32
BF16 = jnp.bfloat16

D_MODEL = 1024
DEPTH = 2
CHUNK = 64
D_FF = 4096
LN_EPS = 1e-5
NORM_EPS = 1e-6
H_A, DK_A, DV_A, CONV_K = 4, 128, 128, 4
A_QK = H_A * DK_A
A_V = H_A * DV_A
H_B, D_B, B_PREV, REL_CLIP = 4, 128, 8, 128
B_W = H_B * D_B
H_C, HKV_C, D_C, WINDOW = 8, 2, 64, 128
G_C = H_C // HKV_C
C_PREV = WINDOW // CHUNK
C_Q = H_C * D_C
C_KV = HKV_C * D_C
N_BRANCH = 3
ALPHA = (2.0 * DEPTH) ** 0.25

LANES = 128
VMEM_LIMIT = 56 * 1024 * 1024
NEG_BIG = -1e30

SMALL_W = LANES
QKV_A = 2 * A_QK + A_V
A_W = QKV_A + A_V
C_W = C_Q + 2 * C_KV


def _dot(a, b):
    return jnp.dot(a, b, preferred_element_type=F32)


def _dot_nt(a, b):
    return lax.dot_general(a, b, (((1,), (1,)), ((), ())), preferred_element_type=F32)


def _layer_norm(y, g, b):
    mu = jnp.mean(y, axis=-1, keepdims=True)
    yc = y - mu
    var = jnp.mean(yc * yc, axis=-1, keepdims=True)
    return yc * lax.rsqrt(var + LN_EPS) * g + b


def _silu(x):
    return x * jax.nn.sigmoid(x)


def _params(*sem):
    return pltpu.CompilerParams(dimension_semantics=sem, vmem_limit_bytes=VMEM_LIMIT)


FFN_TM = 1024
FFN_TF = 512


def _ffn_kernel(x_ref, wg_ref, wu_ref, wo_ref, g_ref, b_ref, o_ref, xb_ref, acc_ref):
    f = pl.program_id(1)

    @pl.when(f == 0)
    def _():
        xb_ref[...] = x_ref[...].astype(BF16)
        acc_ref[...] = jnp.zeros_like(acc_ref)

    xb = xb_ref[...]
    gate = _dot(xb, wg_ref[...])
    up = _dot(xb, wu_ref[...])
    h = (_silu(gate) * up).astype(BF16)
    acc_ref[...] += _dot(h, wo_ref[...])

    @pl.when(f == pl.num_programs(1) - 1)
    def _():
        y = ALPHA * x_ref[...] + 0.5 * acc_ref[...]
        o_ref[...] = _layer_norm(y, g_ref[...], b_ref[...])


def _ffn(x, w_in, w_out, g, b):
    n = x.shape[0]
    nf = D_FF // FFN_TF
    return pl.pallas_call(
        _ffn_kernel,
        grid=(n // FFN_TM, nf),
        in_specs=[
            pl.BlockSpec((FFN_TM, D_MODEL), lambda i, f: (i, 0)),
            pl.BlockSpec((D_MODEL, FFN_TF), lambda i, f: (0, f)),
            pl.BlockSpec((D_MODEL, FFN_TF), lambda i, f: (0, f + nf)),
            pl.BlockSpec((FFN_TF, D_MODEL), lambda i, f: (f, 0)),
            pl.BlockSpec((1, D_MODEL), lambda i, f: (0, 0)),
            pl.BlockSpec((1, D_MODEL), lambda i, f: (0, 0)),
        ],
        out_specs=pl.BlockSpec((FFN_TM, D_MODEL), lambda i, f: (i, 0)),
        out_shape=jax.ShapeDtypeStruct((n, D_MODEL), F32),
        scratch_shapes=[pltpu.VMEM((FFN_TM, D_MODEL), BF16), pltpu.VMEM((FFN_TM, D_MODEL), F32)],
        compiler_params=_params("parallel", "arbitrary"),
        name="ffn_ln",
    )(x, w_in, w_in, w_out, g, b)


PROJ_TM = 512


def _proj_kernel(x_ref, wa_ref, ws_ref, wb_ref, wc_ref, ba_ref, bs_ref, bb_ref, bc_ref,
                 oa_ref, os_ref, ob_ref, oc_ref):
    xb = x_ref[...].astype(BF16)
    oa_ref[...] = _dot(xb, wa_ref[...]) + ba_ref[...]
    os_ref[...] = _dot(xb, ws_ref[...]) + bs_ref[...]
    ob_ref[...] = (_dot(xb, wb_ref[...]) + bb_ref[...]).astype(BF16)
    oc_ref[...] = (_dot(xb, wc_ref[...]) + bc_ref[...]).astype(BF16)


def _proj(x, wa, ws, wb, wc, ba, bs, bb, bc):
    n = x.shape[0]
    widths = (A_W, SMALL_W, 3 * B_W, C_W)
    row = lambda w: pl.BlockSpec((PROJ_TM, w), lambda i: (i, 0))
    const = lambda r, w: pl.BlockSpec((r, w), lambda i: (0, 0))
    return pl.pallas_call(
        _proj_kernel,
        grid=(n // PROJ_TM,),
        in_specs=[row(D_MODEL)] + [const(D_MODEL, w) for w in widths] + [const(1, w) for w in widths],
        out_specs=[row(w) for w in widths],
        out_shape=[jax.ShapeDtypeStruct((n, A_W), F32), jax.ShapeDtypeStruct((n, SMALL_W), F32),
                   jax.ShapeDtypeStruct((n, 3 * B_W), BF16), jax.ShapeDtypeStruct((n, C_W), BF16)],
        compiler_params=_params("parallel"),
        name="in_proj",
    )(x, wa, ws, wb, wc, ba, bs, bb, bc)


GDN_R = 256
GDN_NC = GDN_R // CHUNK
CONV_PAD = 8
LOG2_CHUNK = CHUNK.bit_length() - 1
NEUMANN_STEPS = LOG2_CHUNK - 1


def _gdn_kernel(a_ref, s_ref, cw_ref, alog_ref, dtb_ref, ng_ref, o_ref, xbuf, st_ref):
    i = pl.program_id(1)

    @pl.when(i == 0)
    def _():
        st_ref[...] = jnp.zeros_like(st_ref)
        xbuf[0:CONV_PAD, :] = jnp.zeros((CONV_PAD, QKV_A), F32)

    xbuf[CONV_PAD:CONV_PAD + GDN_R, :] = a_ref[:, 0:QKV_A]
    base = CONV_PAD - (CONV_K - 1)
    acc = cw_ref[0:1, :] * xbuf[base:base + GDN_R, :]
    for j in range(1, CONV_K):
        acc = acc + cw_ref[j:j + 1, :] * xbuf[base + j:base + j + GDN_R, :]
    xbuf[0:CONV_PAD, :] = a_ref[GDN_R - CONV_PAD:GDN_R, 0:QKV_A]
    qkv = _silu(acc)

    s = s_ref[...]
    beta_all = jax.nn.sigmoid(s)
    g_all = -jnp.exp(alog_ref[...]) * jax.nn.softplus(s + dtb_ref[...])
    row = lax.broadcasted_iota(jnp.int32, (GDN_R, GDN_R), 0)
    col = lax.broadcasted_iota(jnp.int32, (GDN_R, GDN_R), 1)
    same_chunk = jnp.right_shift(row, LOG2_CHUNK) == jnp.right_shift(col, LOG2_CHUNK)
    causal = same_chunk & (col <= row)
    strict = same_chunk & (col < row)
    eye = jnp.where(row == col, 1.0, 0.0).astype(F32)
    hi = lax.Precision.HIGHEST
    gcum = jnp.dot(jnp.where(causal, 1.0, 0.0).astype(F32), g_all,
                   preferred_element_type=F32, precision=hi)
    gsum = jnp.dot(jnp.where(same_chunk, 1.0, 0.0).astype(F32), g_all,
                   preferred_element_type=F32, precision=hi)
    gcum_t = gcum.T
    ng = ng_ref[...]
    heads = range(H_A)

    qn, kn_b, rhs, decay, gcol, glast = [], [], [], [], [], []
    kk, qk, kdec = [], [], []
    for h in heads:
        lo = h * DK_A
        qh = qkv[:, lo:lo + DK_A]
        kh = qkv[:, A_QK + lo:A_QK + lo + DK_A]
        vh = qkv[:, 2 * A_QK + lo:2 * A_QK + lo + DV_A]
        q_h = qh * (lax.rsqrt(jnp.sum(qh * qh, -1, keepdims=True) + NORM_EPS) * (DK_A ** -0.5))
        k_h = kh * lax.rsqrt(jnp.sum(kh * kh, -1, keepdims=True) + NORM_EPS)
        bcol = beta_all[:, h:h + 1]
        g_h = gcum[:, H_A + h:H_A + h + 1]
        gl_h = gsum[:, H_A + h:H_A + h + 1]
        eg = jnp.exp(g_h)
        kb = k_h * bcol
        k_b = k_h.astype(BF16)
        kk.append(_dot_nt(kb.astype(BF16), k_b))
        qk.append(_dot_nt(q_h.astype(BF16), k_b))
        rhs.append(jnp.concatenate([vh * bcol, kb * eg], axis=1).astype(BF16))
        decay.append(jnp.exp(jnp.where(causal, g_h - gcum_t[H_A + h:H_A + h + 1, :], -jnp.inf)))
        qn.append(q_h * eg)
        kdec.append(k_h * jnp.exp(gl_h - g_h))
        gcol.append(g_h)
        glast.append(gl_h)

    m = [jnp.where(strict, -(kk[h] * decay[h]), 0.0) for h in heads]
    inv = [eye + m[h] for h in heads]
    p = m
    for _ in range(NEUMANN_STEPS):
        pb = [p[h].astype(BF16) for h in heads]
        p = [_dot(pb[h], pb[h]) for h in heads]
        inv = [inv[h] + _dot(p[h].astype(BF16), inv[h].astype(BF16)) for h in heads]
    uw = [_dot(inv[h].astype(BF16), rhs[h]).astype(BF16) for h in heads]
    aw = [_dot((qk[h] * decay[h]).astype(BF16), uw[h]) for h in heads]
    qt = [(qn[h] - aw[h][:, DV_A:DV_A + DK_A]).astype(BF16) for h in heads]

    bp = [[_dot(kdec[h][c * CHUNK:(c + 1) * CHUNK].T.astype(BF16), uw[h][c * CHUNK:(c + 1) * CHUNK])
           for c in range(GDN_NC)] for h in heads]
    state = [st_ref[h] for h in heads]
    for c in range(GDN_NC):
        r0 = c * CHUNK
        for h in heads:
            lo = h * DV_A
            s_b = state[h].astype(BF16)
            o = _dot(qt[h][r0:r0 + CHUNK], s_b) + aw[h][r0:r0 + CHUNK, 0:DV_A]
            decay_c = jnp.exp(glast[h][r0:r0 + 1])
            state[h] = (state[h] * decay_c + bp[h][c][:, 0:DV_A]
                        - _dot(bp[h][c][:, DV_A:DV_A + DK_A].astype(BF16), s_b))
            z = a_ref[r0:r0 + CHUNK, QKV_A + lo:QKV_A + lo + DV_A]
            on = o * lax.rsqrt(jnp.mean(o * o, -1, keepdims=True) + NORM_EPS) * ng * _silu(z)
            o_ref[r0:r0 + CHUNK, lo:lo + DV_A] = on.astype(BF16)
    for h in heads:
        st_ref[h] = state[h]


def _gdn(a, s, conv_w, alog_row, dtb_row, norm_g, batch, seq):
    n = a.shape[0]
    nblk = seq // GDN_R
    const = lambda r, w: pl.BlockSpec((r, w), lambda b, i: (0, 0))
    return pl.pallas_call(
        _gdn_kernel,
        grid=(batch, nblk),
        in_specs=[
            pl.BlockSpec((GDN_R, A_W), lambda b, i: (b * nblk + i, 0)),
            pl.BlockSpec((GDN_R, SMALL_W), lambda b, i: (b * nblk + i, 0)),
            const(CONV_K, QKV_A), const(1, SMALL_W), const(1, SMALL_W), const(1, DV_A),
        ],
        out_specs=pl.BlockSpec((GDN_R, A_V), lambda b, i: (b * nblk + i, 0)),
        out_shape=jax.ShapeDtypeStruct((n, A_V), BF16),
        scratch_shapes=[pltpu.VMEM((CONV_PAD + GDN_R, QKV_A), F32),
                        pltpu.VMEM((H_A, DK_A, DV_A), F32)],
        compiler_params=_params("parallel", "arbitrary"),
        name="gdn",
    )(a, s, conv_w, alog_row, dtb_row, norm_g)


ATT_R = 256
ATT_CH = ATT_R // CHUNK
B_KBLK = 3
assert (B_KBLK - 1) * ATT_CH == B_PREV


def _attn_b_kernel(q_ref, k0_ref, k1_ref, k2_ref, v0_ref, v1_ref, v2_ref, bias_ref, o_ref):
    i = pl.program_id(1)
    k_refs = (k0_ref, k1_ref, k2_ref)
    v_refs = (v0_ref, v1_ref, v2_ref)
    scale = D_B ** -0.5
    for h in range(H_B):
        lo = h * D_B
        q = q_ref[:, lo:lo + D_B]
        parts = []
        for j in range(B_KBLK):
            sj = _dot_nt(q, k_refs[j][:, lo:lo + D_B]) * scale
            if j < B_KBLK - 1:
                sj = jnp.where(i + j >= B_KBLK - 1, sj, NEG_BIG)
            parts.append(sj)
        s = jnp.concatenate(parts, axis=1) + bias_ref[h]
        mx = jnp.max(s, axis=-1, keepdims=True)
        p = jnp.exp(s - mx)
        denom = jnp.sum(p, axis=-1, keepdims=True)
        pb = p.astype(BF16)
        o = _dot(pb[:, 0:ATT_R], v_refs[0][:, lo:lo + D_B])
        for j in range(1, B_KBLK):
            o = o + _dot(pb[:, j * ATT_R:(j + 1) * ATT_R], v_refs[j][:, lo:lo + D_B])
        o_ref[:, lo:lo + D_B] = (o / denom).astype(BF16)


def _attn_b(qkv, bias, batch, seq):
    n = qkv.shape[0]
    nblk = seq // ATT_R

    def kv_spec(j, colblk):
        return pl.BlockSpec(
            (ATT_R, B_W), lambda b, i: (b * nblk + jnp.maximum(i + j - (B_KBLK - 1), 0), colblk))

    return pl.pallas_call(
        _attn_b_kernel,
        grid=(batch, nblk),
        in_specs=[pl.BlockSpec((ATT_R, B_W), lambda b, i: (b * nblk + i, 0))]
        + [kv_spec(j, 1) for j in range(B_KBLK)] + [kv_spec(j, 2) for j in range(B_KBLK)]
        + [pl.BlockSpec((H_B, ATT_R, B_KBLK * ATT_R), lambda b, i: (0, 0, 0))],
        out_specs=pl.BlockSpec((ATT_R, B_W), lambda b, i: (b * nblk + i, 0)),
        out_shape=jax.ShapeDtypeStruct((n, B_W), BF16),
        compiler_params=_params("parallel", "parallel"),
        name="attn_b",
    )(qkv, qkv, qkv, qkv, qkv, qkv, qkv, bias)


def _attn_b_bias(rel_bias):
    nh, nrel = rel_bias.shape
    span = 2 * CHUNK
    n_toeplitz = -(-REL_CLIP // CHUNK) + 1
    assert nrel == CHUNK + REL_CLIP and (n_toeplitz - 1) * CHUNK + span <= nrel + span
    rb = jnp.concatenate([rel_bias, jnp.broadcast_to(rel_bias[:, -1:], (nh, span))], axis=1).astype(F32)
    far = rel_bias[:, -1].astype(F32)[:, None, None]
    qchunk = np.arange(ATT_R) // CHUNK + B_PREV
    kchunk = np.arange(B_KBLK * ATT_R) // CHUNK
    delta = jnp.asarray(qchunk[:, None] - kchunk[None, :])[None]
    table = jnp.where((delta >= n_toeplitz) & (delta <= B_PREV), far, NEG_BIG)
    for d in range(n_toeplitz):
        w = rb[:, d * CHUNK:d * CHUNK + span]
        flat = jnp.tile(w, (1, CHUNK + 1))[:, :CHUNK * (span + 1)]
        skew = flat.reshape(nh, CHUNK, span + 1)
        blk = jnp.flip(skew[:, :, 0:CHUNK], axis=2)
        table = jnp.where(delta == d, jnp.tile(blk, (1, ATT_CH, B_KBLK * ATT_CH)), table)
    return table


C_KR = 2 * CHUNK
C_KBLK = 3


def _attn_c_kernel(sink_ref, q_ref, k0_ref, k1_ref, k2_ref, v0_ref, v1_ref, v2_ref, o_ref):
    i = pl.program_id(1)
    k_refs = (k0_ref, k1_ref, k2_ref)
    v_refs = (v0_ref, v1_ref, v2_ref)
    scale = D_C ** -0.5
    tq = lax.broadcasted_iota(jnp.int32, (ATT_R, C_KR), 0) + C_KR
    dist_f, vis = [], []
    for j in range(C_KBLK):
        tk = lax.broadcasted_iota(jnp.int32, (ATT_R, C_KR), 1) + j * C_KR
        delta = jnp.right_shift(tq, LOG2_CHUNK) - jnp.right_shift(tk, LOG2_CHUNK)
        ok = (delta >= 0) & (delta <= C_PREV)
        if j == 0:
            ok = ok & (i > 0)
        vis.append(ok)
        dist_f.append(jnp.abs(tq - tk).astype(F32))
    for h in range(H_C):
        g = h // G_C
        slope = 2.0 ** (-8.0 * (h + 1) / H_C)
        q = q_ref[:, h * D_C:(h + 1) * D_C]
        parts = []
        for j in range(C_KBLK):
            sj = _dot_nt(q, k_refs[j][:, g * D_C:(g + 1) * D_C]) * scale - slope * dist_f[j]
            parts.append(jnp.where(vis[j], sj, NEG_BIG))
        s = jnp.concatenate(parts, axis=1)
        sink = sink_ref[h]
        mx = jnp.maximum(jnp.max(s, axis=-1, keepdims=True), sink)
        p = jnp.exp(s - mx)
        denom = jnp.sum(p, axis=-1, keepdims=True) + jnp.exp(sink - mx)
        pb = p.astype(BF16)
        o = _dot(pb[:, 0:C_KR], v_refs[0][:, g * D_C:(g + 1) * D_C])
        for j in range(1, C_KBLK):
            o = o + _dot(pb[:, j * C_KR:(j + 1) * C_KR], v_refs[j][:, g * D_C:(g + 1) * D_C])
        o_ref[:, h * D_C:(h + 1) * D_C] = (o / denom).astype(BF16)


def _attn_c(qkv, sinks, batch, seq):
    n = qkv.shape[0]
    nblk = seq // ATT_R
    kper = ATT_R // C_KR
    kcol, vcol = C_Q // C_KV, C_Q // C_KV + 1

    def kv_spec(j, colblk):
        return pl.BlockSpec(
            (C_KR, C_KV),
            lambda b, i: (b * nblk * kper + jnp.maximum(i * kper + j - 1, 0), colblk))

    return pl.pallas_call(
        _attn_c_kernel,
        grid=(batch, nblk),
        in_specs=[pl.BlockSpec(memory_space=pltpu.SMEM),
                  pl.BlockSpec((ATT_R, C_Q), lambda b, i: (b * nblk + i, 0))]
        + [kv_spec(j, kcol) for j in range(C_KBLK)] + [kv_spec(j, vcol) for j in range(C_KBLK)],
        out_specs=pl.BlockSpec((ATT_R, C_Q), lambda b, i: (b * nblk + i, 0)),
        out_shape=jax.ShapeDtypeStruct((n, C_Q), BF16),
        compiler_params=_params("parallel", "parallel"),
        name="attn_c",
    )(sinks, qkv, qkv, qkv, qkv, qkv, qkv, qkv)


MERGE_TM = 512


def _merge_kernel(x_ref, oa_ref, ob_ref, oc_ref, wg_ref, bg_ref, wa_ref, wb_ref, wc_ref, wo_ref,
                  g_ref, b_ref, o_ref):
    x = x_ref[...]
    xb = x.astype(BF16)
    merged = None
    for r, (br_ref, w_ref) in enumerate(((oa_ref, wa_ref), (ob_ref, wb_ref), (oc_ref, wc_ref))):
        lo = r * D_MODEL
        gate = jax.nn.sigmoid(_dot(xb, wg_ref[:, lo:lo + D_MODEL]) + bg_ref[:, lo:lo + D_MODEL])
        term = gate * _dot(br_ref[...], w_ref[...])
        merged = term if merged is None else merged + term
    y = ALPHA * x + _dot(merged.astype(BF16), wo_ref[...])
    o_ref[...] = _layer_norm(y, g_ref[...], b_ref[...])


def _merge(x, oa, ob, oc, w_gate, b_gate, w_a, w_b, w_c, w_out, g, b):
    n = x.shape[0]
    row = lambda w: pl.BlockSpec((MERGE_TM, w), lambda i: (i, 0))
    const = lambda r, w: pl.BlockSpec((r, w), lambda i: (0, 0))
    return pl.pallas_call(
        _merge_kernel,
        grid=(n // MERGE_TM,),
        in_specs=[row(D_MODEL), row(A_V), row(B_W), row(C_Q),
                  const(D_MODEL, N_BRANCH * D_MODEL), const(1, N_BRANCH * D_MODEL),
                  const(A_V, D_MODEL), const(B_W, D_MODEL), const(C_Q, D_MODEL),
                  const(D_MODEL, D_MODEL), const(1, D_MODEL), const(1, D_MODEL)],
        out_specs=row(D_MODEL),
        out_shape=jax.ShapeDtypeStruct((n, D_MODEL), F32),
        compiler_params=_params("parallel"),
        name="merge_ln",
    )(x, oa, ob, oc, w_gate, b_gate, w_a, w_b, w_c, w_out, g, b)


def _lane_row(vals, offset):
    return jnp.zeros((1, LANES), F32).at[0, offset:offset + vals.shape[0]].set(vals.astype(F32))


def kernel(x, ln1_g, ln1_b, w_ff1_in, w_ff1_out, w_in, b_in, conv_w, a_log, dt_bias, gdn_norm_g,
           rel_bias, sinks, w_gate, b_gate, w_br_a, w_br_b, w_br_c, w_out, ln2_g, ln2_b,
           w_ff2_in, w_ff2_out, ln3_g, ln3_b):
    batch, seq, d = x.shape
    assert d == D_MODEL and seq % ATT_R == 0 and (batch * seq) % FFN_TM == 0
    h = x.reshape(batch * seq, d)
    row = lambda v: v.reshape(1, -1).astype(F32)
    o_small = QKV_A
    o_z = o_small + 2 * H_A
    o_b = o_z + A_V
    o_c = o_b + 3 * B_W
    for l in range(DEPTH):
        h = _ffn(h, w_ff1_in[l].astype(BF16), w_ff1_out[l].astype(BF16), row(ln1_g[l]), row(ln1_b[l]))
        wi, bi = w_in[l], b_in[l]
        pad = SMALL_W - 2 * H_A
        wa = jnp.concatenate([wi[:, 0:QKV_A], wi[:, o_z:o_b]], axis=1).astype(BF16)
        ws = jnp.pad(wi[:, o_small:o_z], ((0, 0), (0, pad))).astype(BF16)
        ba = jnp.concatenate([bi[0:QKV_A], bi[o_z:o_b]])
        bs = jnp.pad(bi[o_small:o_z], (0, pad))
        pa, ps, pb, pc = _proj(h, wa, ws, wi[:, o_b:o_c].astype(BF16), wi[:, o_c:].astype(BF16),
                               row(ba), row(bs), row(bi[o_b:o_c]), row(bi[o_c:]))
        o_a = _gdn(pa, ps, conv_w[l].astype(F32), _lane_row(a_log[l], H_A), _lane_row(dt_bias[l], H_A),
                   row(gdn_norm_g[l]), batch, seq)
        o_bb = _attn_b(pb, _attn_b_bias(rel_bias[l]), batch, seq)
        o_cc = _attn_c(pc, sinks[l].astype(F32), batch, seq)
        h = _merge(h, o_a, o_bb, o_cc, w_gate[l].astype(BF16), row(b_gate[l]),
                   w_br_a[l].astype(BF16), w_br_b[l].astype(BF16), w_br_c[l].astype(BF16),
                   w_out[l].astype(BF16), row(ln2_g[l]), row(ln2_b[l]))
        h = _ffn(h, w_ff2_in[l].astype(BF16), w_ff2_out[l].astype(BF16), row(ln3_g[l]), row(ln3_b[l]))
    return h.reshape(batch, seq, d)
```

```python
import functools
import math

import jax
import jax.numpy as jnp
import numpy as np
from jax import lax
from jax.experimental import pallas as pl
from jax.experimental.pallas import tpu as pltpu

F32 = jnp.float32
BF16 = jnp.bfloat16

D_MODEL = 1024
DEPTH = 2
CHUNK = 64
D_FF = 4096
LN_EPS = 1e-5
NORM_EPS = 1e-6
H_A, DK_A, DV_A, CONV_K = 4, 128, 128, 4
A_QK = H_A * DK_A
A_V = H_A * DV_A
H_B, D_B, B_PREV, REL_CLIP = 4, 128, 8, 128
B_W = H_B * D_B
H_C, HKV_C, D_C, WINDOW = 8, 2, 64, 128
G_C = H_C // HKV_C
C_PREV = WINDOW // CHUNK
C_Q = H_C * D_C
C_KV = HKV_C * D_C
N_BRANCH = 3
ALPHA = (2.0 * DEPTH) ** 0.25

LANES = 128
VMEM_LIMIT = 56 * 1024 * 1024
NEG_BIG = -1e30

SMALL_W = LANES
QKV_A = 2 * A_QK + A_V
A_W = QKV_A + A_V
C_W = C_Q + 2 * C_KV


def _dot(a, b):
    return jnp.dot(a, b, preferred_element_type=F32)


def _dot_nt(a, b):
    return lax.dot_general(a, b, (((1,), (1,)), ((), ())), preferred_element_type=F32)


def _layer_norm(y, g, b):
    mu = jnp.mean(y, axis=-1, keepdims=True)
    yc = y - mu
    var = jnp.mean(yc * yc, axis=-1, keepdims=True)
    return yc * lax.rsqrt(var + LN_EPS) * g + b


def _silu(x):
    return x * jax.nn.sigmoid(x)


def _params(*sem):
    return pltpu.CompilerParams(dimension_semantics=sem, vmem_limit_bytes=VMEM_LIMIT)


FFN_TM = 512
FFN_SUB = 256
FFN_TF = 1024


def _ffn_kernel(x_ref, wi_ref, wo_ref, g_ref, b_ref, o_ref, h_ref):
    for s in range(FFN_TM // FFN_SUB):
        rows = slice(s * FFN_SUB, (s + 1) * FFN_SUB)
        x = x_ref[rows, :]
        xb = x.astype(BF16)
        for f in range(D_FF // FFN_TF):
            gate = _dot(xb, wi_ref[:, f * FFN_TF:(f + 1) * FFN_TF])
            up = _dot(xb, wi_ref[:, D_FF + f * FFN_TF:D_FF + (f + 1) * FFN_TF])
            h_ref[rows, f * FFN_TF:(f + 1) * FFN_TF] = (_silu(gate) * up).astype(BF16)
        y = ALPHA * x + 0.5 * _dot(h_ref[rows, :], wo_ref[...])
        o_ref[rows, :] = _layer_norm(y, g_ref[...], b_ref[...])


def _ffn(x, w_in, w_out, g, b):
    n = x.shape[0]
    resident = lambda r, w: pl.BlockSpec((r, w), lambda i: (0, 0), pipeline_mode=pl.Buffered(1))
    return pl.pallas_call(
        _ffn_kernel,
        grid=(n // FFN_TM,),
        in_specs=[
            pl.BlockSpec((FFN_TM, D_MODEL), lambda i: (i, 0)),
            resident(D_MODEL, 2 * D_FF), resident(D_FF, D_MODEL),
            resident(1, D_MODEL), resident(1, D_MODEL),
        ],
        out_specs=pl.BlockSpec((FFN_TM, D_MODEL), lambda i: (i, 0)),
        out_shape=jax.ShapeDtypeStruct((n, D_MODEL), F32),
        scratch_shapes=[pltpu.VMEM((FFN_TM, D_FF), BF16)],
        compiler_params=_params("parallel"),
        name="ffn_ln",
    )(x, w_in, w_out, g, b)


PROJ_TM = 512


def _proj_kernel(x_ref, wa_ref, ws_ref, wb_ref, wc_ref, ba_ref, bs_ref, bb_ref, bc_ref,
                 oa_ref, os_ref, ob_ref, oc_ref):
    xb = x_ref[...].astype(BF16)
    oa_ref[...] = _dot(xb, wa_ref[...]) + ba_ref[...]
    os_ref[...] = _dot(xb, ws_ref[...]) + bs_ref[...]
    ob_ref[...] = (_dot(xb, wb_ref[...]) + bb_ref[...]).astype(BF16)
    oc_ref[...] = (_dot(xb, wc_ref[...]) + bc_ref[...]).astype(BF16)


def _proj(x, wa, ws, wb, wc, ba, bs, bb, bc):
    n = x.shape[0]
    widths = (A_W, SMALL_W, 3 * B_W, C_W)
    row = lambda w: pl.BlockSpec((PROJ_TM, w), lambda i: (i, 0))
    const = lambda r, w: pl.BlockSpec((r, w), lambda i: (0, 0))
    return pl.pallas_call(
        _proj_kernel,
        grid=(n // PROJ_TM,),
        in_specs=[row(D_MODEL)] + [const(D_MODEL, w) for w in widths] + [const(1, w) for w in widths],
        out_specs=[row(w) for w in widths],
        out_shape=[jax.ShapeDtypeStruct((n, A_W), F32), jax.ShapeDtypeStruct((n, SMALL_W), F32),
                   jax.ShapeDtypeStruct((n, 3 * B_W), BF16), jax.ShapeDtypeStruct((n, C_W), BF16)],
        compiler_params=_params("parallel"),
        name="in_proj",
    )(x, wa, ws, wb, wc, ba, bs, bb, bc)


GDN_R = 256
GDN_NC = GDN_R // CHUNK
CONV_PAD = 8
LOG2_CHUNK = CHUNK.bit_length() - 1
NEUMANN_STEPS = LOG2_CHUNK - 1


def _gdn_kernel(a_ref, s_ref, cw_ref, alog_ref, dtb_ref, ng_ref, o_ref, xbuf, st_ref):
    i = pl.program_id(1)

    @pl.when(i == 0)
    def _():
        st_ref[...] = jnp.zeros_like(st_ref)
        xbuf[0:CONV_PAD, :] = jnp.zeros((CONV_PAD, QKV_A), F32)

    xbuf[CONV_PAD:CONV_PAD + GDN_R, :] = a_ref[:, 0:QKV_A]
    base = CONV_PAD - (CONV_K - 1)
    acc = cw_ref[0:1, :] * xbuf[base:base + GDN_R, :]
    for j in range(1, CONV_K):
        acc = acc + cw_ref[j:j + 1, :] * xbuf[base + j:base + j + GDN_R, :]
    xbuf[0:CONV_PAD, :] = a_ref[GDN_R - CONV_PAD:GDN_R, 0:QKV_A]
    qkv = _silu(acc)

    s = s_ref[...]
    beta_all = jax.nn.sigmoid(s)
    g_all = -jnp.exp(alog_ref[...]) * jax.nn.softplus(s + dtb_ref[...])
    row = lax.broadcasted_iota(jnp.int32, (GDN_R, GDN_R), 0)
    col = lax.broadcasted_iota(jnp.int32, (GDN_R, GDN_R), 1)
    same_chunk = jnp.right_shift(row, LOG2_CHUNK) == jnp.right_shift(col, LOG2_CHUNK)
    causal = same_chunk & (col <= row)
    strict = same_chunk & (col < row)
    eye = jnp.where(row == col, 1.0, 0.0).astype(F32)
    hi = lax.Precision.HIGHEST
    gcum = jnp.dot(jnp.where(causal, 1.0, 0.0).astype(F32), g_all,
                   preferred_element_type=F32, precision=hi)
    gsum = jnp.dot(jnp.where(same_chunk, 1.0, 0.0).astype(F32), g_all,
                   preferred_element_type=F32, precision=hi)
    gcum_t = gcum.T
    ng = ng_ref[...]
    heads = range(H_A)

    qn, kn_b, rhs, decay, gcol, glast = [], [], [], [], [], []
    kk, qk, kdec = [], [], []
    for h in heads:
        lo = h * DK_A
        qh = qkv[:, lo:lo + DK_A]
        kh = qkv[:, A_QK + lo:A_QK + lo + DK_A]
        vh = qkv[:, 2 * A_QK + lo:2 * A_QK + lo + DV_A]
        q_h = qh * (lax.rsqrt(jnp.sum(qh * qh, -1, keepdims=True) + NORM_EPS) * (DK_A ** -0.5))
        k_h = kh * lax.rsqrt(jnp.sum(kh * kh, -1, keepdims=True) + NORM_EPS)
        bcol = beta_all[:, h:h + 1]
        g_h = gcum[:, H_A + h:H_A + h + 1]
        gl_h = gsum[:, H_A + h:H_A + h + 1]
        eg = jnp.exp(g_h)
        kb = k_h * bcol
        k_b = k_h.astype(BF16)
        kk.append(_dot_nt(kb.astype(BF16), k_b))
        qk.append(_dot_nt(q_h.astype(BF16), k_b))
        rhs.append(jnp.concatenate([vh * bcol, kb * eg], axis=1).astype(BF16))
        decay.append(jnp.exp(jnp.where(causal, g_h - gcum_t[H_A + h:H_A + h + 1, :], -jnp.inf)))
        qn.append(q_h * eg)
        kdec.append(k_h * jnp.exp(gl_h - g_h))
        gcol.append(g_h)
        glast.append(gl_h)

    m = [jnp.where(strict, -(kk[h] * decay[h]), 0.0) for h in heads]
    inv = [eye + m[h] for h in heads]
    p = m
    for _ in range(NEUMANN_STEPS):
        pb = [p[h].astype(BF16) for h in heads]
        p = [_dot(pb[h], pb[h]) for h in heads]
        inv = [inv[h] + _dot(p[h].astype(BF16), inv[h].astype(BF16)) for h in heads]
    uw = [_dot(inv[h].astype(BF16), rhs[h]).astype(BF16) for h in heads]
    aw = [_dot((qk[h] * decay[h]).astype(BF16), uw[h]) for h in heads]
    qt = [(qn[h] - aw[h][:, DV_A:DV_A + DK_A]).astype(BF16) for h in heads]

    bp = [[_dot(kdec[h][c * CHUNK:(c + 1) * CHUNK].T.astype(BF16), uw[h][c * CHUNK:(c + 1) * CHUNK])
           for c in range(GDN_NC)] for h in heads]
    state = [st_ref[h] for h in heads]
    for c in range(GDN_NC):
        r0 = c * CHUNK
        for h in heads:
            lo = h * DV_A
            s_b = state[h].astype(BF16)
            o = _dot(qt[h][r0:r0 + CHUNK], s_b) + aw[h][r0:r0 + CHUNK, 0:DV_A]
            decay_c = jnp.exp(glast[h][r0:r0 + 1])
            state[h] = (state[h] * decay_c + bp[h][c][:, 0:DV_A]
                        - _dot(bp[h][c][:, DV_A:DV_A + DK_A].astype(BF16), s_b))
            z = a_ref[r0:r0 + CHUNK, QKV_A + lo:QKV_A + lo + DV_A]
            on = o * lax.rsqrt(jnp.mean(o * o, -1, keepdims=True) + NORM_EPS) * ng * _silu(z)
            o_ref[r0:r0 + CHUNK, lo:lo + DV_A] = on.astype(BF16)
    for h in heads:
        st_ref[h] = state[h]


def _gdn(a, s, conv_w, alog_row, dtb_row, norm_g, batch, seq):
    n = a.shape[0]
    nblk = seq // GDN_R
    const = lambda r, w: pl.BlockSpec((r, w), lambda b, i: (0, 0))
    return pl.pallas_call(
        _gdn_kernel,
        grid=(batch, nblk),
        in_specs=[
            pl.BlockSpec((GDN_R, A_W), lambda b, i: (b * nblk + i, 0)),
            pl.BlockSpec((GDN_R, SMALL_W), lambda b, i: (b * nblk + i, 0)),
            const(CONV_K, QKV_A), const(1, SMALL_W), const(1, SMALL_W), const(1, DV_A),
        ],
        out_specs=pl.BlockSpec((GDN_R, A_V), lambda b, i: (b * nblk + i, 0)),
        out_shape=jax.ShapeDtypeStruct((n, A_V), BF16),
        scratch_shapes=[pltpu.VMEM((CONV_PAD + GDN_R, QKV_A), F32),
                        pltpu.VMEM((H_A, DK_A, DV_A), F32)],
        compiler_params=_params("parallel", "arbitrary"),
        name="gdn",
    )(a, s, conv_w, alog_row, dtb_row, norm_g)


ATT_R = 256
ATT_CH = ATT_R // CHUNK
B_KBLK = 3
assert (B_KBLK - 1) * ATT_CH == B_PREV


def _attn_b_kernel(q_ref, k0_ref, k1_ref, k2_ref, v0_ref, v1_ref, v2_ref, bias_ref, o_ref):
    i = pl.program_id(1)
    k_refs = (k0_ref, k1_ref, k2_ref)
    v_refs = (v0_ref, v1_ref, v2_ref)
    scale = D_B ** -0.5
    for h in range(H_B):
        lo = h * D_B
        q = q_ref[:, lo:lo + D_B]
        parts = []
        for j in range(B_KBLK):
            sj = _dot_nt(q, k_refs[j][:, lo:lo + D_B]) * scale
            if j < B_KBLK - 1:
                sj = jnp.where(i + j >= B_KBLK - 1, sj, NEG_BIG)
            parts.append(sj)
        s = jnp.concatenate(parts, axis=1) + bias_ref[h]
        mx = jnp.max(s, axis=-1, keepdims=True)
        p = jnp.exp(s - mx)
        denom = jnp.sum(p, axis=-1, keepdims=True)
        pb = p.astype(BF16)
        o = _dot(pb[:, 0:ATT_R], v_refs[0][:, lo:lo + D_B])
        for j in range(1, B_KBLK):
            o = o + _dot(pb[:, j * ATT_R:(j + 1) * ATT_R], v_refs[j][:, lo:lo + D_B])
        o_ref[:, lo:lo + D_B] = (o / denom).astype(BF16)


def _attn_b(qkv, bias, batch, seq):
    n = qkv.shape[0]
    nblk = seq // ATT_R

    def kv_spec(j, colblk):
        return pl.BlockSpec(
            (ATT_R, B_W), lambda b, i: (b * nblk + jnp.maximum(i + j - (B_KBLK - 1), 0), colblk))

    return pl.pallas_call(
        _attn_b_kernel,
        grid=(batch, nblk),
        in_specs=[pl.BlockSpec((ATT_R, B_W), lambda b, i: (b * nblk + i, 0))]
        + [kv_spec(j, 1) for j in range(B_KBLK)] + [kv_spec(j, 2) for j in range(B_KBLK)]
        + [pl.BlockSpec((H_B, ATT_R, B_KBLK * ATT_R), lambda b, i: (0, 0, 0))],
        out_specs=pl.BlockSpec((ATT_R, B_W), lambda b, i: (b * nblk + i, 0)),
        out_shape=jax.ShapeDtypeStruct((n, B_W), BF16),
        compiler_params=_params("parallel", "parallel"),
        name="attn_b",
    )(qkv, qkv, qkv, qkv, qkv, qkv, qkv, bias)


def _attn_b_bias(rel_bias):
    nh, nrel = rel_bias.shape
    span = 2 * CHUNK
    n_toeplitz = -(-REL_CLIP // CHUNK) + 1
    assert nrel == CHUNK + REL_CLIP and (n_toeplitz - 1) * CHUNK + span <= nrel + span
    rb = jnp.concatenate([rel_bias, jnp.broadcast_to(rel_bias[:, -1:], (nh, span))], axis=1).astype(F32)
    far = rel_bias[:, -1].astype(F32)[:, None, None]
    qchunk = np.arange(ATT_R) // CHUNK + B_PREV
    kchunk = np.arange(B_KBLK * ATT_R) // CHUNK
    delta = jnp.asarray(qchunk[:, None] - kchunk[None, :])[None]
    table = jnp.where((delta >= n_toeplitz) & (delta <= B_PREV), far, NEG_BIG)
    for d in range(n_toeplitz):
        w = rb[:, d * CHUNK:d * CHUNK + span]
        flat = jnp.tile(w, (1, CHUNK + 1))[:, :CHUNK * (span + 1)]
        skew = flat.reshape(nh, CHUNK, span + 1)
        blk = jnp.flip(skew[:, :, 0:CHUNK], axis=2)
        table = jnp.where(delta == d, jnp.tile(blk, (1, ATT_CH, B_KBLK * ATT_CH)), table)
    return table


C_KR = 2 * CHUNK
C_KBLK = 3


def _attn_c_kernel(sink_ref, q_ref, k0_ref, k1_ref, k2_ref, v0_ref, v1_ref, v2_ref, o_ref):
    i = pl.program_id(1)
    k_refs = (k0_ref, k1_ref, k2_ref)
    v_refs = (v0_ref, v1_ref, v2_ref)
    scale = D_C ** -0.5
    tq = lax.broadcasted_iota(jnp.int32, (ATT_R, C_KR), 0) + C_KR
    dist_f, vis = [], []
    for j in range(C_KBLK):
        tk = lax.broadcasted_iota(jnp.int32, (ATT_R, C_KR), 1) + j * C_KR
        delta = jnp.right_shift(tq, LOG2_CHUNK) - jnp.right_shift(tk, LOG2_CHUNK)
        ok = (delta >= 0) & (delta <= C_PREV)
        if j == 0:
            ok = ok & (i > 0)
        vis.append(ok)
        dist_f.append(jnp.abs(tq - tk).astype(F32))
    for h in range(H_C):
        g = h // G_C
        slope = 2.0 ** (-8.0 * (h + 1) / H_C)
        q = q_ref[:, h * D_C:(h + 1) * D_C]
        parts = []
        for j in range(C_KBLK):
            sj = _dot_nt(q, k_refs[j][:, g * D_C:(g + 1) * D_C]) * scale - slope * dist_f[j]
            parts.append(jnp.where(vis[j], sj, NEG_BIG))
        s = jnp.concatenate(parts, axis=1)
        sink = sink_ref[h]
        mx = jnp.maximum(jnp.max(s, axis=-1, keepdims=True), sink)
        p = jnp.exp(s - mx)
        denom = jnp.sum(p, axis=-1, keepdims=True) + jnp.exp(sink - mx)
        pb = p.astype(BF16)
        o = _dot(pb[:, 0:C_KR], v_refs[0][:, g * D_C:(g + 1) * D_C])
        for j in range(1, C_KBLK):
            o = o + _dot(pb[:, j * C_KR:(j + 1) * C_KR], v_refs[j][:, g * D_C:(g + 1) * D_C])
        o_ref[:, h * D_C:(h + 1) * D_C] = (o / denom).astype(BF16)


def _attn_c(qkv, sinks, batch, seq):
    n = qkv.shape[0]
    nblk = seq // ATT_R
    kper = ATT_R // C_KR
    kcol, vcol = C_Q // C_KV, C_Q // C_KV + 1

    def kv_spec(j, colblk):
        return pl.BlockSpec(
            (C_KR, C_KV),
            lambda b, i: (b * nblk * kper + jnp.maximum(i * kper + j - 1, 0), colblk))

    return pl.pallas_call(
        _attn_c_kernel,
        grid=(batch, nblk),
        in_specs=[pl.BlockSpec(memory_space=pltpu.SMEM),
                  pl.BlockSpec((ATT_R, C_Q), lambda b, i: (b * nblk + i, 0))]
        + [kv_spec(j, kcol) for j in range(C_KBLK)] + [kv_spec(j, vcol) for j in range(C_KBLK)],
        out_specs=pl.BlockSpec((ATT_R, C_Q), lambda b, i: (b * nblk + i, 0)),
        out_shape=jax.ShapeDtypeStruct((n, C_Q), BF16),
        compiler_params=_params("parallel", "parallel"),
        name="attn_c",
    )(sinks, qkv, qkv, qkv, qkv, qkv, qkv, qkv)


MERGE_TM = 512


def _merge_kernel(x_ref, oa_ref, ob_ref, oc_ref, wg_ref, bg_ref, wa_ref, wb_ref, wc_ref, wo_ref,
                  g_ref, b_ref, o_ref):
    x = x_ref[...]
    xb = x.astype(BF16)
    merged = None
    for r, (br_ref, w_ref) in enumerate(((oa_ref, wa_ref), (ob_ref, wb_ref), (oc_ref, wc_ref))):
        lo = r * D_MODEL
        gate = jax.nn.sigmoid(_dot(xb, wg_ref[:, lo:lo + D_MODEL]) + bg_ref[:, lo:lo + D_MODEL])
        term = gate * _dot(br_ref[...], w_ref[...])
        merged = term if merged is None else merged + term
    y = ALPHA * x + _dot(merged.astype(BF16), wo_ref[...])
    o_ref[...] = _layer_norm(y, g_ref[...], b_ref[...])


def _merge(x, oa, ob, oc, w_gate, b_gate, w_a, w_b, w_c, w_out, g, b):
    n = x.shape[0]
    row = lambda w: pl.BlockSpec((MERGE_TM, w), lambda i: (i, 0))
    const = lambda r, w: pl.BlockSpec((r, w), lambda i: (0, 0))
    return pl.pallas_call(
        _merge_kernel,
        grid=(n // MERGE_TM,),
        in_specs=[row(D_MODEL), row(A_V), row(B_W), row(C_Q),
                  const(D_MODEL, N_BRANCH * D_MODEL), const(1, N_BRANCH * D_MODEL),
                  const(A_V, D_MODEL), const(B_W, D_MODEL), const(C_Q, D_MODEL),
                  const(D_MODEL, D_MODEL), const(1, D_MODEL), const(1, D_MODEL)],
        out_specs=row(D_MODEL),
        out_shape=jax.ShapeDtypeStruct((n, D_MODEL), F32),
        compiler_params=_params("parallel"),
        name="merge_ln",
    )(x, oa, ob, oc, w_gate, b_gate, w_a, w_b, w_c, w_out, g, b)


def _lane_row(vals, offset):
    return jnp.zeros((1, LANES), F32).at[0, offset:offset + vals.shape[0]].set(vals.astype(F32))


def kernel(x, ln1_g, ln1_b, w_ff1_in, w_ff1_out, w_in, b_in, conv_w, a_log, dt_bias, gdn_norm_g,
           rel_bias, sinks, w_gate, b_gate, w_br_a, w_br_b, w_br_c, w_out, ln2_g, ln2_b,
           w_ff2_in, w_ff2_out, ln3_g, ln3_b):
    batch, seq, d = x.shape
    assert d == D_MODEL and seq % ATT_R == 0 and (batch * seq) % FFN_TM == 0
    h = x.reshape(batch * seq, d)
    row = lambda v: v.reshape(1, -1).astype(F32)
    o_small = QKV_A
    o_z = o_small + 2 * H_A
    o_b = o_z + A_V
    o_c = o_b + 3 * B_W
    for l in range(DEPTH):
        h = _ffn(h, w_ff1_in[l].astype(BF16), w_ff1_out[l].astype(BF16), row(ln1_g[l]), row(ln1_b[l]))
        wi, bi = w_in[l], b_in[l]
        pad = SMALL_W - 2 * H_A
        wa = jnp.concatenate([wi[:, 0:QKV_A], wi[:, o_z:o_b]], axis=1).astype(BF16)
        ws = jnp.pad(wi[:, o_small:o_z], ((0, 0), (0, pad))).astype(BF16)
        ba = jnp.concatenate([bi[0:QKV_A], bi[o_z:o_b]])
        bs = jnp.pad(bi[o_small:o_z], (0, pad))
        pa, ps, pb, pc = _proj(h, wa, ws, wi[:, o_b:o_c].astype(BF16), wi[:, o_c:].astype(BF16),
                               row(ba), row(bs), row(bi[o_b:o_c]), row(bi[o_c:]))
        o_a = _gdn(pa, ps, conv_w[l].astype(F32), _lane_row(a_log[l], H_A), _lane_row(dt_bias[l], H_A),
                   row(gdn_norm_g[l]), batch, seq)
        o_bb = _attn_b(pb, _attn_b_bias(rel_bias[l]), batch, seq)
        o_cc = _attn_c(pc, sinks[l].astype(F32), batch, seq)
        h = _merge(h, o_a, o_bb, o_cc, w_gate[l].astype(BF16), row(b_gate[l]),
                   w_br_a[l].astype(BF16), w_br_b[l].astype(BF16), w_br_c[l].astype(BF16),
                   w_out[l].astype(BF16), row(ln2_g[l]), row(ln2_b[l]))
        h = _ffn(h, w_ff2_in[l].astype(BF16), w_ff2_out[l].astype(BF16), row(ln3_g[l]), row(ln3_b[l]))
    return h.reshape(batch, seq, d)
```

```python
import functools
import math

import jax
import jax.numpy as jnp
import numpy as np
from jax import lax
from jax.experimental import pallas as pl
from jax.experimental.pallas import tpu as pltpu

F32 = jnp.float32
BF16 = jnp.bfloat16

D_MODEL = 1024
DEPTH = 2
CHUNK = 64
D_FF = 4096
LN_EPS = 1e-5
NORM_EPS = 1e-6
H_A, DK_A, DV_A, CONV_K = 4, 128, 128, 4
A_QK = H_A * DK_A
A_V = H_A * DV_A
H_B, D_B, B_PREV, REL_CLIP = 4, 128, 8, 128
B_W = H_B * D_B
H_C, HKV_C, D_C, WINDOW = 8, 2, 64, 128
G_C = H_C // HKV_C
C_PREV = WINDOW // CHUNK
C_Q = H_C * D_C
C_KV = HKV_C * D_C
N_BRANCH = 3
ALPHA = (2.0 * DEPTH) ** 0.25

LANES = 128
VMEM_LIMIT = 56 * 1024 * 1024
NEG_BIG = -1e30

SMALL_W = LANES
QKV_A = 2 * A_QK + A_V
A_W = QKV_A + A_V
C_W = C_Q + 2 * C_KV


def _dot(a, b):
    return jnp.dot(a, b, preferred_element_type=F32)


def _dot_nt(a, b):
    return lax.dot_general(a, b, (((1,), (1,)), ((), ())), preferred_element_type=F32)


def _layer_norm(y, g, b):
    mu = jnp.mean(y, axis=-1, keepdims=True)
    yc = y - mu
    var = jnp.mean(yc * yc, axis=-1, keepdims=True)
    return yc * lax.rsqrt(var + LN_EPS) * g + b


def _silu(x):
    return x * jax.nn.sigmoid(x)


def _params(*sem):
    return pltpu.CompilerParams(dimension_semantics=sem, vmem_limit_bytes=VMEM_LIMIT)


FFN_TM = 512
FFN_SUB = 256
FFN_TF = 1024


FFN_WCH = 1024


def _stage_cast(pairs, stage, sem):
    copies = [pltpu.make_async_copy(src, stage.at[c % 2], sem.at[c % 2])
              for c, (src, _) in enumerate(pairs)]
    copies[0].start()
    for c, (_, dst) in enumerate(pairs):
        if c + 1 < len(pairs):
            copies[c + 1].start()
        copies[c].wait()
        dst[...] = stage[c % 2].astype(BF16)


def _ffn_kernel(layer, x_ref, wi_hbm, wo_hbm, g_ref, b_ref, o_ref, wi_ref, wo_ref, h_ref, stage, sem):
    @pl.when(pl.program_id(0) == 0)
    def _():
        w = FFN_WCH
        pairs = [(wi_hbm.at[layer, :, pl.ds(c * w, w)], wi_ref.at[:, pl.ds(c * w, w)])
                 for c in range(2 * D_FF // w)]
        pairs += [(wo_hbm.at[layer, pl.ds(c * w, w), :], wo_ref.at[pl.ds(c * w, w), :])
                  for c in range(D_FF // w)]
        _stage_cast(pairs, stage, sem)

    for s in range(FFN_TM // FFN_SUB):
        rows = slice(s * FFN_SUB, (s + 1) * FFN_SUB)
        x = x_ref[rows, :]
        xb = x.astype(BF16)
        for f in range(D_FF // FFN_TF):
            gate = _dot(xb, wi_ref[:, f * FFN_TF:(f + 1) * FFN_TF])
            up = _dot(xb, wi_ref[:, D_FF + f * FFN_TF:D_FF + (f + 1) * FFN_TF])
            h_ref[rows, f * FFN_TF:(f + 1) * FFN_TF] = (_silu(gate) * up).astype(BF16)
        y = ALPHA * x + 0.5 * _dot(h_ref[rows, :], wo_ref[...])
        o_ref[rows, :] = _layer_norm(y, g_ref[...], b_ref[...])


def _ffn(x, w_in, w_out, g, b, layer):
    n = x.shape[0]
    resident = lambda r, w: pl.BlockSpec((r, w), lambda i: (0, 0), pipeline_mode=pl.Buffered(1))
    return pl.pallas_call(
        functools.partial(_ffn_kernel, layer),
        grid=(n // FFN_TM,),
        in_specs=[
            pl.BlockSpec((FFN_TM, D_MODEL), lambda i: (i, 0)),
            pl.BlockSpec(memory_space=pl.ANY), pl.BlockSpec(memory_space=pl.ANY),
            resident(1, D_MODEL), resident(1, D_MODEL),
        ],
        out_specs=pl.BlockSpec((FFN_TM, D_MODEL), lambda i: (i, 0)),
        out_shape=jax.ShapeDtypeStruct((n, D_MODEL), F32),
        scratch_shapes=[pltpu.VMEM((D_MODEL, 2 * D_FF), BF16), pltpu.VMEM((D_FF, D_MODEL), BF16),
                        pltpu.VMEM((FFN_TM, D_FF), BF16),
                        pltpu.VMEM((2, FFN_WCH, FFN_WCH), F32), pltpu.SemaphoreType.DMA((2,))],
        compiler_params=_params("arbitrary"),
        name="ffn_ln",
    )(x, w_in, w_out, g, b)


PROJ_TM = 512


def _proj_kernel(x_ref, wa_ref, ws_ref, wb_ref, wc_ref, ba_ref, bs_ref, bb_ref, bc_ref,
                 oa_ref, os_ref, ob_ref, oc_ref):
    xb = x_ref[...].astype(BF16)
    oa_ref[...] = _dot(xb, wa_ref[...]) + ba_ref[...]
    os_ref[...] = _dot(xb, ws_ref[...]) + bs_ref[...]
    ob_ref[...] = (_dot(xb, wb_ref[...]) + bb_ref[...]).astype(BF16)
    oc_ref[...] = (_dot(xb, wc_ref[...]) + bc_ref[...]).astype(BF16)


def _proj(x, wa, ws, wb, wc, ba, bs, bb, bc):
    n = x.shape[0]
    widths = (A_W, SMALL_W, 3 * B_W, C_W)
    row = lambda w: pl.BlockSpec((PROJ_TM, w), lambda i: (i, 0))
    const = lambda r, w: pl.BlockSpec((r, w), lambda i: (0, 0))
    return pl.pallas_call(
        _proj_kernel,
        grid=(n // PROJ_TM,),
        in_specs=[row(D_MODEL)] + [const(D_MODEL, w) for w in widths] + [const(1, w) for w in widths],
        out_specs=[row(w) for w in widths],
        out_shape=[jax.ShapeDtypeStruct((n, A_W), F32), jax.ShapeDtypeStruct((n, SMALL_W), F32),
                   jax.ShapeDtypeStruct((n, 3 * B_W), BF16), jax.ShapeDtypeStruct((n, C_W), BF16)],
        compiler_params=_params("parallel"),
        name="in_proj",
    )(x, wa, ws, wb, wc, ba, bs, bb, bc)


GDN_R = 256
GDN_NC = GDN_R // CHUNK
CONV_PAD = 8
LOG2_CHUNK = CHUNK.bit_length() - 1
NEUMANN_STEPS = LOG2_CHUNK - 1


def _gdn_kernel(a_ref, s_ref, cw_ref, alog_ref, dtb_ref, ng_ref, o_ref, xbuf, st_ref):
    i = pl.program_id(1)

    @pl.when(i == 0)
    def _():
        st_ref[...] = jnp.zeros_like(st_ref)
        xbuf[0:CONV_PAD, :] = jnp.zeros((CONV_PAD, QKV_A), F32)

    xbuf[CONV_PAD:CONV_PAD + GDN_R, :] = a_ref[:, 0:QKV_A]
    base = CONV_PAD - (CONV_K - 1)
    acc = cw_ref[0:1, :] * xbuf[base:base + GDN_R, :]
    for j in range(1, CONV_K):
        acc = acc + cw_ref[j:j + 1, :] * xbuf[base + j:base + j + GDN_R, :]
    xbuf[0:CONV_PAD, :] = a_ref[GDN_R - CONV_PAD:GDN_R, 0:QKV_A]
    qkv = _silu(acc)

    s = s_ref[...]
    beta_all = jax.nn.sigmoid(s)
    g_all = -jnp.exp(alog_ref[...]) * jax.nn.softplus(s + dtb_ref[...])
    row = lax.broadcasted_iota(jnp.int32, (GDN_R, GDN_R), 0)
    col = lax.broadcasted_iota(jnp.int32, (GDN_R, GDN_R), 1)
    same_chunk = jnp.right_shift(row, LOG2_CHUNK) == jnp.right_shift(col, LOG2_CHUNK)
    causal = same_chunk & (col <= row)
    strict = same_chunk & (col < row)
    eye = jnp.where(row == col, 1.0, 0.0).astype(F32)
    hi = lax.Precision.HIGHEST
    gcum = jnp.dot(jnp.where(causal, 1.0, 0.0).astype(F32), g_all,
                   preferred_element_type=F32, precision=hi)
    gsum = jnp.dot(jnp.where(same_chunk, 1.0, 0.0).astype(F32), g_all,
                   preferred_element_type=F32, precision=hi)
    gcum_t = gcum.T
    ng = ng_ref[...]
    heads = range(H_A)

    qn, kn_b, rhs, decay, gcol, glast = [], [], [], [], [], []
    kk, qk, kdec = [], [], []
    for h in heads:
        lo = h * DK_A
        qh = qkv[:, lo:lo + DK_A]
        kh = qkv[:, A_QK + lo:A_QK + lo + DK_A]
        vh = qkv[:, 2 * A_QK + lo:2 * A_QK + lo + DV_A]
        q_h = qh * (lax.rsqrt(jnp.sum(qh * qh, -1, keepdims=True) + NORM_EPS) * (DK_A ** -0.5))
        k_h = kh * lax.rsqrt(jnp.sum(kh * kh, -1, keepdims=True) + NORM_EPS)
        bcol = beta_all[:, h:h + 1]
        g_h = gcum[:, H_A + h:H_A + h + 1]
        gl_h = gsum[:, H_A + h:H_A + h + 1]
        eg = jnp.exp(g_h)
        kb = k_h * bcol
        k_b = k_h.astype(BF16)
        kk.append(_dot_nt(kb.astype(BF16), k_b))
        qk.append(_dot_nt(q_h.astype(BF16), k_b))
        rhs.append(jnp.concatenate([vh * bcol, kb * eg], axis=1).astype(BF16))
        decay.append(jnp.exp(jnp.where(causal, g_h - gcum_t[H_A + h:H_A + h + 1, :], -jnp.inf)))
        qn.append(q_h * eg)
        kdec.append(k_h * jnp.exp(gl_h - g_h))
        gcol.append(g_h)
        glast.append(gl_h)

    m = [jnp.where(strict, -(kk[h] * decay[h]), 0.0) for h in heads]
    inv = [eye + m[h] for h in heads]
    p = m
    for _ in range(NEUMANN_STEPS):
        pb = [p[h].astype(BF16) for h in heads]
        p = [_dot(pb[h], pb[h]) for h in heads]
        inv = [inv[h] + _dot(p[h].astype(BF16), inv[h].astype(BF16)) for h in heads]
    uw = [_dot(inv[h].astype(BF16), rhs[h]).astype(BF16) for h in heads]
    aw = [_dot((qk[h] * decay[h]).astype(BF16), uw[h]) for h in heads]
    qt = [(qn[h] - aw[h][:, DV_A:DV_A + DK_A]).astype(BF16) for h in heads]

    bp = [[_dot(kdec[h][c * CHUNK:(c + 1) * CHUNK].T.astype(BF16), uw[h][c * CHUNK:(c + 1) * CHUNK])
           for c in range(GDN_NC)] for h in heads]
    state = [st_ref[h] for h in heads]
    for c in range(GDN_NC):
        r0 = c * CHUNK
        for h in heads:
            lo = h * DV_A
            s_b = state[h].astype(BF16)
            o = _dot(qt[h][r0:r0 + CHUNK], s_b) + aw[h][r0:r0 + CHUNK, 0:DV_A]
            decay_c = jnp.exp(glast[h][r0:r0 + 1])
            state[h] = (state[h] * decay_c + bp[h][c][:, 0:DV_A]
                        - _dot(bp[h][c][:, DV_A:DV_A + DK_A].astype(BF16), s_b))
            z = a_ref[r0:r0 + CHUNK, QKV_A + lo:QKV_A + lo + DV_A]
            on = o * lax.rsqrt(jnp.mean(o * o, -1, keepdims=True) + NORM_EPS) * ng * _silu(z)
            o_ref[r0:r0 + CHUNK, lo:lo + DV_A] = on.astype(BF16)
    for h in heads:
        st_ref[h] = state[h]


def _gdn(a, s, conv_w, alog_row, dtb_row, norm_g, batch, seq):
    n = a.shape[0]
    nblk = seq // GDN_R
    const = lambda r, w: pl.BlockSpec((r, w), lambda b, i: (0, 0))
    return pl.pallas_call(
        _gdn_kernel,
        grid=(batch, nblk),
        in_specs=[
            pl.BlockSpec((GDN_R, A_W), lambda b, i: (b * nblk + i, 0)),
            pl.BlockSpec((GDN_R, SMALL_W), lambda b, i: (b * nblk + i, 0)),
            const(CONV_K, QKV_A), const(1, SMALL_W), const(1, SMALL_W), const(1, DV_A),
        ],
        out_specs=pl.BlockSpec((GDN_R, A_V), lambda b, i: (b * nblk + i, 0)),
        out_shape=jax.ShapeDtypeStruct((n, A_V), BF16),
        scratch_shapes=[pltpu.VMEM((CONV_PAD + GDN_R, QKV_A), F32),
                        pltpu.VMEM((H_A, DK_A, DV_A), F32)],
        compiler_params=_params("parallel", "arbitrary"),
        name="gdn",
    )(a, s, conv_w, alog_row, dtb_row, norm_g)


ATT_R = 256
ATT_CH = ATT_R // CHUNK
B_KBLK = 3
assert (B_KBLK - 1) * ATT_CH == B_PREV


def _attn_b_kernel(q_ref, k0_ref, k1_ref, k2_ref, v0_ref, v1_ref, v2_ref, bias_ref, o_ref):
    i = pl.program_id(1)
    k_refs = (k0_ref, k1_ref, k2_ref)
    v_refs = (v0_ref, v1_ref, v2_ref)
    scale = D_B ** -0.5
    for h in range(H_B):
        lo = h * D_B
        q = q_ref[:, lo:lo + D_B]
        parts = []
        for j in range(B_KBLK):
            sj = _dot_nt(q, k_refs[j][:, lo:lo + D_B]) * scale
            if j < B_KBLK - 1:
                sj = jnp.where(i + j >= B_KBLK - 1, sj, NEG_BIG)
            parts.append(sj)
        s = jnp.concatenate(parts, axis=1) + bias_ref[h]
        mx = jnp.max(s, axis=-1, keepdims=True)
        p = jnp.exp(s - mx)
        denom = jnp.sum(p, axis=-1, keepdims=True)
        pb = p.astype(BF16)
        o = _dot(pb[:, 0:ATT_R], v_refs[0][:, lo:lo + D_B])
        for j in range(1, B_KBLK):
            o = o + _dot(pb[:, j * ATT_R:(j + 1) * ATT_R], v_refs[j][:, lo:lo + D_B])
        o_ref[:, lo:lo + D_B] = (o / denom).astype(BF16)


def _attn_b(qkv, bias, batch, seq):
    n = qkv.shape[0]
    nblk = seq // ATT_R

    def kv_spec(j, colblk):
        return pl.BlockSpec(
            (ATT_R, B_W), lambda b, i: (b * nblk + jnp.maximum(i + j - (B_KBLK - 1), 0), colblk))

    return pl.pallas_call(
        _attn_b_kernel,
        grid=(batch, nblk),
        in_specs=[pl.BlockSpec((ATT_R, B_W), lambda b, i: (b * nblk + i, 0))]
        + [kv_spec(j, 1) for j in range(B_KBLK)] + [kv_spec(j, 2) for j in range(B_KBLK)]
        + [pl.BlockSpec((H_B, ATT_R, B_KBLK * ATT_R), lambda b, i: (0, 0, 0))],
        out_specs=pl.BlockSpec((ATT_R, B_W), lambda b, i: (b * nblk + i, 0)),
        out_shape=jax.ShapeDtypeStruct((n, B_W), BF16),
        compiler_params=_params("parallel", "parallel"),
        name="attn_b",
    )(qkv, qkv, qkv, qkv, qkv, qkv, qkv, bias)


def _attn_b_bias(rel_bias):
    nh, nrel = rel_bias.shape
    span = 2 * CHUNK
    n_toeplitz = -(-REL_CLIP // CHUNK) + 1
    assert nrel == CHUNK + REL_CLIP and (n_toeplitz - 1) * CHUNK + span <= nrel + span
    rb = jnp.concatenate([rel_bias, jnp.broadcast_to(rel_bias[:, -1:], (nh, span))], axis=1).astype(F32)
    far = rel_bias[:, -1].astype(F32)[:, None, None]
    qchunk = np.arange(ATT_R) // CHUNK + B_PREV
    kchunk = np.arange(B_KBLK * ATT_R) // CHUNK
    delta = jnp.asarray(qchunk[:, None] - kchunk[None, :])[None]
    table = jnp.where((delta >= n_toeplitz) & (delta <= B_PREV), far, NEG_BIG)
    for d in range(n_toeplitz):
        w = rb[:, d * CHUNK:d * CHUNK + span]
        flat = jnp.tile(w, (1, CHUNK + 1))[:, :CHUNK * (span + 1)]
        skew = flat.reshape(nh, CHUNK, span + 1)
        blk = jnp.flip(skew[:, :, 0:CHUNK], axis=2)
        table = jnp.where(delta == d, jnp.tile(blk, (1, ATT_CH, B_KBLK * ATT_CH)), table)
    return table


C_KR = 2 * CHUNK
C_KBLK = 3


def _attn_c_kernel(sink_ref, q_ref, k0_ref, k1_ref, k2_ref, v0_ref, v1_ref, v2_ref, o_ref):
    i = pl.program_id(1)
    k_refs = (k0_ref, k1_ref, k2_ref)
    v_refs = (v0_ref, v1_ref, v2_ref)
    scale = D_C ** -0.5
    tq = lax.broadcasted_iota(jnp.int32, (ATT_R, C_KR), 0) + C_KR
    dist_f, vis = [], []
    for j in range(C_KBLK):
        tk = lax.broadcasted_iota(jnp.int32, (ATT_R, C_KR), 1) + j * C_KR
        delta = jnp.right_shift(tq, LOG2_CHUNK) - jnp.right_shift(tk, LOG2_CHUNK)
        ok = (delta >= 0) & (delta <= C_PREV)
        if j == 0:
            ok = ok & (i > 0)
        vis.append(ok)
        dist_f.append(jnp.abs(tq - tk).astype(F32))
    for h in range(H_C):
        g = h // G_C
        slope = 2.0 ** (-8.0 * (h + 1) / H_C)
        q = q_ref[:, h * D_C:(h + 1) * D_C]
        parts = []
        for j in range(C_KBLK):
            sj = _dot_nt(q, k_refs[j][:, g * D_C:(g + 1) * D_C]) * scale - slope * dist_f[j]
            parts.append(jnp.where(vis[j], sj, NEG_BIG))
        s = jnp.concatenate(parts, axis=1)
        sink = sink_ref[h]
        mx = jnp.maximum(jnp.max(s, axis=-1, keepdims=True), sink)
        p = jnp.exp(s - mx)
        denom = jnp.sum(p, axis=-1, keepdims=True) + jnp.exp(sink - mx)
        pb = p.astype(BF16)
        o = _dot(pb[:, 0:C_KR], v_refs[0][:, g * D_C:(g + 1) * D_C])
        for j in range(1, C_KBLK):
            o = o + _dot(pb[:, j * C_KR:(j + 1) * C_KR], v_refs[j][:, g * D_C:(g + 1) * D_C])
        o_ref[:, h * D_C:(h + 1) * D_C] = (o / denom).astype(BF16)


def _attn_c(qkv, sinks, batch, seq):
    n = qkv.shape[0]
    nblk = seq // ATT_R
    kper = ATT_R // C_KR
    kcol, vcol = C_Q // C_KV, C_Q // C_KV + 1

    def kv_spec(j, colblk):
        return pl.BlockSpec(
            (C_KR, C_KV),
            lambda b, i: (b * nblk * kper + jnp.maximum(i * kper + j - 1, 0), colblk))

    return pl.pallas_call(
        _attn_c_kernel,
        grid=(batch, nblk),
        in_specs=[pl.BlockSpec(memory_space=pltpu.SMEM),
                  pl.BlockSpec((ATT_R, C_Q), lambda b, i: (b * nblk + i, 0))]
        + [kv_spec(j, kcol) for j in range(C_KBLK)] + [kv_spec(j, vcol) for j in range(C_KBLK)],
        out_specs=pl.BlockSpec((ATT_R, C_Q), lambda b, i: (b * nblk + i, 0)),
        out_shape=jax.ShapeDtypeStruct((n, C_Q), BF16),
        compiler_params=_params("parallel", "parallel"),
        name="attn_c",
    )(sinks, qkv, qkv, qkv, qkv, qkv, qkv, qkv)


MERGE_TM = 512


MERGE_WR = 512


def _merge_kernel(layer, x_ref, oa_ref, ob_ref, oc_ref, wg_hbm, bg_ref, wa_hbm, wb_hbm, wc_hbm, wo_hbm,
                  g_ref, b_ref, o_ref, wg_ref, wa_ref, wb_ref, wc_ref, wo_ref, stage, sem):
    @pl.when(pl.program_id(0) == 0)
    def _():
        r = MERGE_WR
        pairs = [(wg_hbm.at[layer, pl.ds(i * r, r), pl.ds(j * D_MODEL, D_MODEL)],
                  wg_ref.at[pl.ds(i * r, r), pl.ds(j * D_MODEL, D_MODEL)])
                 for i in range(D_MODEL // r) for j in range(N_BRANCH)]
        pairs += [(src.at[layer], dst) for src, dst in
                  ((wa_hbm, wa_ref), (wb_hbm, wb_ref), (wc_hbm, wc_ref))]
        pairs += [(wo_hbm.at[layer, pl.ds(i * r, r), :], wo_ref.at[pl.ds(i * r, r), :])
                  for i in range(D_MODEL // r)]
        _stage_cast(pairs, stage, sem)

    x = x_ref[...]
    xb = x.astype(BF16)
    merged = None
    for r, (br_ref, w_ref) in enumerate(((oa_ref, wa_ref), (ob_ref, wb_ref), (oc_ref, wc_ref))):
        lo = r * D_MODEL
        gate = jax.nn.sigmoid(_dot(xb, wg_ref[:, lo:lo + D_MODEL]) + bg_ref[:, lo:lo + D_MODEL])
        term = gate * _dot(br_ref[...], w_ref[...])
        merged = term if merged is None else merged + term
    y = ALPHA * x + _dot(merged.astype(BF16), wo_ref[...])
    o_ref[...] = _layer_norm(y, g_ref[...], b_ref[...])


def _merge(x, oa, ob, oc, w_gate, b_gate, w_a, w_b, w_c, w_out, g, b, layer):
    n = x.shape[0]
    assert A_V == B_W == C_Q == MERGE_WR
    row = lambda w: pl.BlockSpec((MERGE_TM, w), lambda i: (i, 0))
    const = lambda r, w: pl.BlockSpec((r, w), lambda i: (0, 0))
    hbm = pl.BlockSpec(memory_space=pl.ANY)
    return pl.pallas_call(
        functools.partial(_merge_kernel, layer),
        grid=(n // MERGE_TM,),
        in_specs=[row(D_MODEL), row(A_V), row(B_W), row(C_Q),
                  hbm, const(1, N_BRANCH * D_MODEL), hbm, hbm, hbm, hbm,
                  const(1, D_MODEL), const(1, D_MODEL)],
        out_specs=row(D_MODEL),
        out_shape=jax.ShapeDtypeStruct((n, D_MODEL), F32),
        scratch_shapes=[pltpu.VMEM((D_MODEL, N_BRANCH * D_MODEL), BF16),
                        pltpu.VMEM((A_V, D_MODEL), BF16), pltpu.VMEM((B_W, D_MODEL), BF16),
                        pltpu.VMEM((C_Q, D_MODEL), BF16), pltpu.VMEM((D_MODEL, D_MODEL), BF16),
                        pltpu.VMEM((2, MERGE_WR, D_MODEL), F32), pltpu.SemaphoreType.DMA((2,))],
        compiler_params=_params("arbitrary"),
        name="merge_ln",
    )(x, oa, ob, oc, w_gate, b_gate, w_a, w_b, w_c, w_out, g, b)


def _lane_row(vals, offset):
    return jnp.zeros((1, LANES), F32).at[0, offset:offset + vals.shape[0]].set(vals.astype(F32))


def kernel(x, ln1_g, ln1_b, w_ff1_in, w_ff1_out, w_in, b_in, conv_w, a_log, dt_bias, gdn_norm_g,
           rel_bias, sinks, w_gate, b_gate, w_br_a, w_br_b, w_br_c, w_out, ln2_g, ln2_b,
           w_ff2_in, w_ff2_out, ln3_g, ln3_b):
    batch, seq, d = x.shape
    assert d == D_MODEL and seq % ATT_R == 0 and (batch * seq) % FFN_TM == 0
    h = x.reshape(batch * seq, d)
    row = lambda v: v.reshape(1, -1).astype(F32)
    o_small = QKV_A
    o_z = o_small + 2 * H_A
    o_b = o_z + A_V
    o_c = o_b + 3 * B_W
    for l in range(DEPTH):
        h = _ffn(h, w_ff1_in, w_ff1_out, row(ln1_g[l]), row(ln1_b[l]), l)
        wi, bi = w_in[l], b_in[l]
        pad = SMALL_W - 2 * H_A
        wa = jnp.concatenate([wi[:, 0:QKV_A], wi[:, o_z:o_b]], axis=1).astype(BF16)
        ws = jnp.pad(wi[:, o_small:o_z], ((0, 0), (0, pad))).astype(BF16)
        ba = jnp.concatenate([bi[0:QKV_A], bi[o_z:o_b]])
        bs = jnp.pad(bi[o_small:o_z], (0, pad))
        pa, ps, pb, pc = _proj(h, wa, ws, wi[:, o_b:o_c].astype(BF16), wi[:, o_c:].astype(BF16),
                               row(ba), row(bs), row(bi[o_b:o_c]), row(bi[o_c:]))
        o_a = _gdn(pa, ps, conv_w[l].astype(F32), _lane_row(a_log[l], H_A), _lane_row(dt_bias[l], H_A),
                   row(gdn_norm_g[l]), batch, seq)
        o_bb = _attn_b(pb, _attn_b_bias(rel_bias[l]), batch, seq)
        o_cc = _attn_c(pc, sinks[l].astype(F32), batch, seq)
        h = _merge(h, o_a, o_bb, o_cc, w_gate, row(b_gate[l]), w_br_a, w_br_b, w_br_c, w_out,
                   row(ln2_g[l]), row(ln2_b[l]), l)
        h = _ffn(h, w_ff2_in, w_ff2_out, row(ln3_g[l]), row(ln3_b[l]), l)
    return h.reshape(batch, seq, d)
```

```python
import functools
import math

import jax
import jax.numpy as jnp
import numpy as np
from jax import lax
from jax.experimental import pallas as pl
from jax.experimental.pallas import tpu as pltpu

F32 = jnp.float32
BF16 = jnp.bfloat16

D_MODEL = 1024
DEPTH = 2
CHUNK = 64
D_FF = 4096
LN_EPS = 1e-5
NORM_EPS = 1e-6
H_A, DK_A, DV_A, CONV_K = 4, 128, 128, 4
A_QK = H_A * DK_A
A_V = H_A * DV_A
H_B, D_B, B_PREV, REL_CLIP = 4, 128, 8, 128
B_W = H_B * D_B
H_C, HKV_C, D_C, WINDOW = 8, 2, 64, 128
G_C = H_C // HKV_C
C_PREV = WINDOW // CHUNK
C_Q = H_C * D_C
C_KV = HKV_C * D_C
N_BRANCH = 3
ALPHA = (2.0 * DEPTH) ** 0.25

LANES = 128
VMEM_LIMIT = 56 * 1024 * 1024
NEG_BIG = -1e30

SMALL_W = LANES
QKV_A = 2 * A_QK + A_V
A_W = QKV_A + A_V
C_W = C_Q + 2 * C_KV


def _dot(a, b):
    return jnp.dot(a, b, preferred_element_type=F32)


def _dot_nt(a, b):
    return lax.dot_general(a, b, (((1,), (1,)), ((), ())), preferred_element_type=F32)


def _layer_norm(y, g, b):
    mu = jnp.mean(y, axis=-1, keepdims=True)
    yc = y - mu
    var = jnp.mean(yc * yc, axis=-1, keepdims=True)
    return yc * lax.rsqrt(var + LN_EPS) * g + b


def _silu(x):
    return x * jax.nn.sigmoid(x)


def _params(*sem):
    return pltpu.CompilerParams(dimension_semantics=sem, vmem_limit_bytes=VMEM_LIMIT)


FFN_TM = 512
FFN_SUB = 256
FFN_TF = 1024


FFN_WCH = 1024


def _stage_cast(pairs, stage, sem):
    copies = [pltpu.make_async_copy(src, stage.at[c % 2], sem.at[c % 2])
              for c, (src, _) in enumerate(pairs)]
    copies[0].start()
    for c, (_, dst) in enumerate(pairs):
        if c + 1 < len(pairs):
            copies[c + 1].start()
        copies[c].wait()
        dst[...] = stage[c % 2].astype(BF16)


def _ffn_kernel(layer, x_ref, wi_hbm, wo_hbm, g_ref, b_ref, o_ref, wi_ref, wo_ref, h_ref, stage, sem):
    @pl.when(pl.program_id(0) == 0)
    def _():
        w = FFN_WCH
        pairs = [(wi_hbm.at[layer, :, pl.ds(c * w, w)], wi_ref.at[:, pl.ds(c * w, w)])
                 for c in range(2 * D_FF // w)]
        pairs += [(wo_hbm.at[layer, pl.ds(c * w, w), :], wo_ref.at[pl.ds(c * w, w), :])
                  for c in range(D_FF // w)]
        _stage_cast(pairs, stage, sem)

    for s in range(FFN_TM // FFN_SUB):
        rows = slice(s * FFN_SUB, (s + 1) * FFN_SUB)
        x = x_ref[rows, :]
        xb = x.astype(BF16)
        for f in range(D_FF // FFN_TF):
            gate = _dot(xb, wi_ref[:, f * FFN_TF:(f + 1) * FFN_TF])
            up = _dot(xb, wi_ref[:, D_FF + f * FFN_TF:D_FF + (f + 1) * FFN_TF])
            h_ref[rows, f * FFN_TF:(f + 1) * FFN_TF] = (_silu(gate) * up).astype(BF16)
        y = ALPHA * x + 0.5 * _dot(h_ref[rows, :], wo_ref[...])
        o_ref[rows, :] = _layer_norm(y, g_ref[...], b_ref[...])


def _ffn(x, w_in, w_out, g, b, layer):
    n = x.shape[0]
    resident = lambda r, w: pl.BlockSpec((r, w), lambda i: (0, 0), pipeline_mode=pl.Buffered(1))
    return pl.pallas_call(
        functools.partial(_ffn_kernel, layer),
        grid=(n // FFN_TM,),
        in_specs=[
            pl.BlockSpec((FFN_TM, D_MODEL), lambda i: (i, 0)),
            pl.BlockSpec(memory_space=pl.ANY), pl.BlockSpec(memory_space=pl.ANY),
            resident(1, D_MODEL), resident(1, D_MODEL),
        ],
        out_specs=pl.BlockSpec((FFN_TM, D_MODEL), lambda i: (i, 0)),
        out_shape=jax.ShapeDtypeStruct((n, D_MODEL), F32),
        scratch_shapes=[pltpu.VMEM((D_MODEL, 2 * D_FF), BF16), pltpu.VMEM((D_FF, D_MODEL), BF16),
                        pltpu.VMEM((FFN_TM, D_FF), BF16),
                        pltpu.VMEM((2, FFN_WCH, FFN_WCH), F32), pltpu.SemaphoreType.DMA((2,))],
        compiler_params=_params("arbitrary"),
        name="ffn_ln",
    )(x, w_in, w_out, g, b)


PROJ_TM = 512


def _proj_kernel(x_ref, wa_ref, ws_ref, wb_ref, wc_ref, ba_ref, bs_ref, bb_ref, bc_ref,
                 oa_ref, os_ref, ob_ref, oc_ref):
    xb = x_ref[...].astype(BF16)
    oa_ref[...] = _dot(xb, wa_ref[...]) + ba_ref[...]
    os_ref[...] = _dot(xb, ws_ref[...]) + bs_ref[...]
    yb = _dot(xb, wb_ref[...]) + bb_ref[...]
    ob_ref[:, 0:B_W] = (yb[:, 0:B_W] * (D_B ** -0.5)).astype(BF16)
    ob_ref[:, B_W:3 * B_W] = yb[:, B_W:3 * B_W].astype(BF16)
    yc = _dot(xb, wc_ref[...]) + bc_ref[...]
    oc_ref[:, 0:C_Q] = (yc[:, 0:C_Q] * (D_C ** -0.5)).astype(BF16)
    oc_ref[:, C_Q:C_W] = yc[:, C_Q:C_W].astype(BF16)


def _proj(x, wa, ws, wb, wc, ba, bs, bb, bc):
    n = x.shape[0]
    widths = (A_W, SMALL_W, 3 * B_W, C_W)
    row = lambda w: pl.BlockSpec((PROJ_TM, w), lambda i: (i, 0))
    const = lambda r, w: pl.BlockSpec((r, w), lambda i: (0, 0))
    return pl.pallas_call(
        _proj_kernel,
        grid=(n // PROJ_TM,),
        in_specs=[row(D_MODEL)] + [const(D_MODEL, w) for w in widths] + [const(1, w) for w in widths],
        out_specs=[row(w) for w in widths],
        out_shape=[jax.ShapeDtypeStruct((n, A_W), F32), jax.ShapeDtypeStruct((n, SMALL_W), F32),
                   jax.ShapeDtypeStruct((n, 3 * B_W), BF16), jax.ShapeDtypeStruct((n, C_W), BF16)],
        compiler_params=_params("parallel"),
        name="in_proj",
    )(x, wa, ws, wb, wc, ba, bs, bb, bc)


GDN_R = 256
GDN_NC = GDN_R // CHUNK
CONV_PAD = 8
LOG2_CHUNK = CHUNK.bit_length() - 1
NEUMANN_STEPS = LOG2_CHUNK - 1


def _gdn_kernel(a_ref, s_ref, cw_ref, alog_ref, dtb_ref, ng_ref, o_ref, xbuf, st_ref):
    i = pl.program_id(1)

    @pl.when(i == 0)
    def _():
        st_ref[...] = jnp.zeros_like(st_ref)
        xbuf[...] = jnp.zeros_like(xbuf)

    x_new = a_ref[:, 0:QKV_A]
    xx = jnp.concatenate([xbuf[...], x_new], axis=0)
    acc = cw_ref[CONV_K - 1:CONV_K, :] * x_new
    for s in range(1, CONV_K):
        shifted = pltpu.roll(xx, s, axis=0)[CONV_PAD:CONV_PAD + GDN_R]
        acc = acc + cw_ref[CONV_K - 1 - s:CONV_K - s, :] * shifted
    xbuf[...] = x_new[GDN_R - CONV_PAD:GDN_R]
    qkv = _silu(acc)

    s = s_ref[...]
    beta_all = jax.nn.sigmoid(s)
    g_all = -jnp.exp(alog_ref[...]) * jax.nn.softplus(s + dtb_ref[...])
    row = lax.broadcasted_iota(jnp.int32, (GDN_R, GDN_R), 0)
    col = lax.broadcasted_iota(jnp.int32, (GDN_R, GDN_R), 1)
    same_chunk = jnp.right_shift(row, LOG2_CHUNK) == jnp.right_shift(col, LOG2_CHUNK)
    causal = same_chunk & (col <= row)
    strict = same_chunk & (col < row)
    eye = jnp.where(row == col, 1.0, 0.0).astype(F32)
    g_hi = g_all.astype(BF16)
    g_r1 = g_all - g_hi.astype(F32)
    g_mid = g_r1.astype(BF16)
    g_lo = (g_r1 - g_mid.astype(F32)).astype(BF16)
    g_parts = _dot(jnp.where(causal, 1.0, 0.0).astype(BF16),
                   jnp.concatenate([g_hi, g_mid, g_lo], axis=1))
    gcum = g_parts[:, 0:LANES] + g_parts[:, LANES:2 * LANES] + g_parts[:, 2 * LANES:3 * LANES]
    gsum = jnp.concatenate(
        [jnp.broadcast_to(gcum[(c + 1) * CHUNK - 1:(c + 1) * CHUNK, :], (CHUNK, LANES))
         for c in range(GDN_NC)], axis=0)
    gcum_t = gcum.T
    ng = ng_ref[...]
    heads = range(H_A)

    qn, kn_b, rhs, decay, gcol, glast = [], [], [], [], [], []
    kk, qk, kdec = [], [], []
    for h in heads:
        lo = h * DK_A
        qh = qkv[:, lo:lo + DK_A]
        kh = qkv[:, A_QK + lo:A_QK + lo + DK_A]
        vh = qkv[:, 2 * A_QK + lo:2 * A_QK + lo + DV_A]
        q_h = qh * (lax.rsqrt(jnp.sum(qh * qh, -1, keepdims=True) + NORM_EPS) * (DK_A ** -0.5))
        k_h = kh * lax.rsqrt(jnp.sum(kh * kh, -1, keepdims=True) + NORM_EPS)
        bcol = beta_all[:, h:h + 1]
        g_h = gcum[:, H_A + h:H_A + h + 1]
        gl_h = gsum[:, H_A + h:H_A + h + 1]
        eg = jnp.exp(g_h)
        kb = k_h * bcol
        k_b = k_h.astype(BF16)
        kk.append(_dot_nt(kb.astype(BF16), k_b))
        qk.append(_dot_nt(q_h.astype(BF16), k_b))
        rhs.append(jnp.concatenate([vh * bcol, kb * eg], axis=1).astype(BF16))
        decay.append(jnp.exp(jnp.where(causal, g_h - gcum_t[H_A + h:H_A + h + 1, :], -jnp.inf)))
        qn.append(q_h * eg)
        kdec.append(k_h * jnp.exp(gl_h - g_h))
        gcol.append(g_h)
        glast.append(gl_h)

    m = [jnp.where(strict, -(kk[h] * decay[h]), 0.0) for h in heads]
    inv = [eye + m[h] for h in heads]
    p = m
    for _ in range(NEUMANN_STEPS):
        pb = [p[h].astype(BF16) for h in heads]
        p = [_dot(pb[h], pb[h]) for h in heads]
        inv = [inv[h] + _dot(p[h].astype(BF16), inv[h].astype(BF16)) for h in heads]
    uw = [_dot(inv[h].astype(BF16), rhs[h]).astype(BF16) for h in heads]
    aw = [_dot((qk[h] * decay[h]).astype(BF16), uw[h]) for h in heads]
    qt = [(qn[h] - aw[h][:, DV_A:DV_A + DK_A]).astype(BF16) for h in heads]

    bp = [[_dot(kdec[h][c * CHUNK:(c + 1) * CHUNK].T.astype(BF16), uw[h][c * CHUNK:(c + 1) * CHUNK])
           for c in range(GDN_NC)] for h in heads]
    state = [st_ref[h] for h in heads]
    for c in range(GDN_NC):
        r0 = c * CHUNK
        for h in heads:
            lo = h * DV_A
            s_b = state[h].astype(BF16)
            o = _dot(qt[h][r0:r0 + CHUNK], s_b) + aw[h][r0:r0 + CHUNK, 0:DV_A]
            decay_c = jnp.exp(glast[h][r0:r0 + 1])
            state[h] = (state[h] * decay_c + bp[h][c][:, 0:DV_A]
                        - _dot(bp[h][c][:, DV_A:DV_A + DK_A].astype(BF16), s_b))
            z = a_ref[r0:r0 + CHUNK, QKV_A + lo:QKV_A + lo + DV_A]
            on = o * lax.rsqrt(jnp.mean(o * o, -1, keepdims=True) + NORM_EPS) * ng * _silu(z)
            o_ref[r0:r0 + CHUNK, lo:lo + DV_A] = on.astype(BF16)
    for h in heads:
        st_ref[h] = state[h]


def _gdn(a, s, conv_w, alog_row, dtb_row, norm_g, batch, seq):
    n = a.shape[0]
    nblk = seq // GDN_R
    const = lambda r, w: pl.BlockSpec((r, w), lambda b, i: (0, 0))
    return pl.pallas_call(
        _gdn_kernel,
        grid=(batch, nblk),
        in_specs=[
            pl.BlockSpec((GDN_R, A_W), lambda b, i: (b * nblk + i, 0)),
            pl.BlockSpec((GDN_R, SMALL_W), lambda b, i: (b * nblk + i, 0)),
            const(CONV_K, QKV_A), const(1, SMALL_W), const(1, SMALL_W), const(1, DV_A),
        ],
        out_specs=pl.BlockSpec((GDN_R, A_V), lambda b, i: (b * nblk + i, 0)),
        out_shape=jax.ShapeDtypeStruct((n, A_V), BF16),
        scratch_shapes=[pltpu.VMEM((CONV_PAD, QKV_A), F32),
                        pltpu.VMEM((H_A, DK_A, DV_A), F32)],
        compiler_params=_params("parallel", "arbitrary"),
        name="gdn",
    )(a, s, conv_w, alog_row, dtb_row, norm_g)


ATT_R = 256
ATT_CH = ATT_R // CHUNK
B_KBLK = 3
assert (B_KBLK - 1) * ATT_CH == B_PREV
B_STRIP_W = B_KBLK * ATT_R + (ATT_CH - 2) * CHUNK


def _attn_b_kernel(q_ref, k0_ref, k1_ref, k2_ref, v0_ref, v1_ref, v2_ref, bias_ref, o_ref):
    i = pl.program_id(1)
    k_refs = (k0_ref, k1_ref, k2_ref)
    v_refs = (v0_ref, v1_ref, v2_ref)
    heads = range(H_B)
    cols = lambda h: slice(h * D_B, (h + 1) * D_B)
    scores = []
    for h in heads:
        q = q_ref[:, cols(h)]
        parts = []
        for j in range(B_KBLK):
            sj = _dot_nt(q, k_refs[j][:, cols(h)])
            if j < B_KBLK - 1:
                sj = jnp.where(i + j >= B_KBLK - 1, sj, NEG_BIG)
            parts.append(sj)
        bias = []
        for a in range(ATT_CH):
            shift = ATT_CH - 1 - a
            start = (shift - shift % 2) * CHUNK
            bias.append(bias_ref[h, shift % 2, :, start:start + B_KBLK * ATT_R])
        scores.append(jnp.concatenate(parts, axis=1) + jnp.concatenate(bias, axis=0))
    probs, denoms = [], []
    for s in scores:
        p = jnp.exp(s - jnp.max(s, axis=-1, keepdims=True))
        denoms.append(jnp.sum(p, axis=-1, keepdims=True))
        probs.append(p.astype(BF16))
    for h in heads:
        o = _dot(probs[h][:, 0:ATT_R], v_refs[0][:, cols(h)])
        for j in range(1, B_KBLK):
            o = o + _dot(probs[h][:, j * ATT_R:(j + 1) * ATT_R], v_refs[j][:, cols(h)])
        o_ref[:, cols(h)] = (o / denoms[h]).astype(BF16)


def _attn_b(qkv, bias, batch, seq, layer):
    n = qkv.shape[0]
    nblk = seq // ATT_R

    def kv_spec(j, colblk):
        return pl.BlockSpec(
            (ATT_R, B_W), lambda b, i: (b * nblk + jnp.maximum(i + j - (B_KBLK - 1), 0), colblk))

    return pl.pallas_call(
        _attn_b_kernel,
        grid=(batch, nblk),
        in_specs=[pl.BlockSpec((ATT_R, B_W), lambda b, i: (b * nblk + i, 0))]
        + [kv_spec(j, 1) for j in range(B_KBLK)] + [kv_spec(j, 2) for j in range(B_KBLK)]
        + [pl.BlockSpec((H_B, 2, CHUNK, B_STRIP_W), lambda b, i: (layer, 0, 0, 0))],
        out_specs=pl.BlockSpec((ATT_R, B_W), lambda b, i: (b * nblk + i, 0)),
        out_shape=jax.ShapeDtypeStruct((n, B_W), BF16),
        compiler_params=_params("parallel", "parallel"),
        name="attn_b",
    )(qkv, qkv, qkv, qkv, qkv, qkv, qkv, bias)


def _attn_b_bias(rel_bias):
    nh, nrel = rel_bias.shape
    assert nrel == CHUNK + REL_CLIP
    lead = (ATT_CH - 1) * CHUNK
    vis_end = lead + (B_PREV + 1) * CHUNK
    width = B_KBLK * ATT_R + lead
    period = 1 << (width + CHUNK - 1).bit_length()
    rb = rel_bias.astype(F32)
    far = jnp.broadcast_to(rb[:, -1:], (nh, period))
    f = jnp.concatenate([far[:, :vis_end - nrel], jnp.flip(rb, axis=1), far[:, :period - vis_end]], axis=1)
    g = jnp.concatenate([f[:, :1], jnp.flip(f[:, 1:], axis=1)], axis=1)
    skew = jnp.tile(g, (1, CHUNK + 1))[:, :CHUNK * (period + 1)].reshape(nh, CHUNK, period + 1)
    toep = jnp.flip(skew[:, :, period - width + 1:period + 1], axis=2)
    c = np.arange(width)
    strip = jnp.where(jnp.asarray((c >= lead) & (c < vis_end))[None, None, :], toep, NEG_BIG)
    return jnp.stack([strip[:, :, 0:B_STRIP_W], strip[:, :, CHUNK:CHUNK + B_STRIP_W]], axis=1)


C_KR = 2 * CHUNK
C_KBLK = 3


def _attn_c_kernel(sink_ref, q_ref, k0_ref, k1_ref, k2_ref, v0_ref, v1_ref, v2_ref, tbl_ref, o_ref):
    i = pl.program_id(1)
    k_refs = (k0_ref, k1_ref, k2_ref)
    v_refs = (v0_ref, v1_ref, v2_ref)
    jobs = [(h, half) for h in range(H_C) for half in range(ATT_R // C_KR)]
    kv_cols = lambda h: slice((h // G_C) * D_C, (h // G_C + 1) * D_C)
    scores = []
    for h, half in jobs:
        q = q_ref[half * C_KR:(half + 1) * C_KR, h * D_C:(h + 1) * D_C]
        s_prev = _dot_nt(q, k_refs[half][:, kv_cols(h)])
        if half == 0:
            s_prev = jnp.where(i > 0, s_prev, NEG_BIG)
        s_own = _dot_nt(q, k_refs[half + 1][:, kv_cols(h)])
        scores.append(jnp.concatenate([s_prev, s_own], axis=1) + tbl_ref[h])
    probs, denoms = [], []
    for (h, half), s in zip(jobs, scores):
        sink = sink_ref[h]
        mx = jnp.maximum(jnp.max(s, axis=-1, keepdims=True), sink)
        p = jnp.exp(s - mx)
        denoms.append(jnp.sum(p, axis=-1, keepdims=True) + jnp.exp(sink - mx))
        probs.append(p.astype(BF16))
    for (h, half), pb, denom in zip(jobs, probs, denoms):
        o = (_dot(pb[:, 0:C_KR], v_refs[half][:, kv_cols(h)])
             + _dot(pb[:, C_KR:2 * C_KR], v_refs[half + 1][:, kv_cols(h)]))
        o_ref[half * C_KR:(half + 1) * C_KR, h * D_C:(h + 1) * D_C] = (o / denom).astype(BF16)


def _attn_c_table():
    tq = np.arange(C_KR)[:, None] + C_KR
    tk = np.arange(2 * C_KR)[None, :]
    delta = tq // CHUNK - tk // CHUNK
    vis = (delta >= 0) & (delta <= C_PREV)
    slopes = 2.0 ** (-8.0 * np.arange(1, H_C + 1) / H_C)
    tbl = np.where(vis[None], -slopes[:, None, None] * np.abs(tq - tk)[None].astype(np.float64), NEG_BIG)
    return jnp.asarray(tbl, F32)


def _attn_c(qkv, sinks, batch, seq):
    n = qkv.shape[0]
    nblk = seq // ATT_R
    kper = ATT_R // C_KR
    kcol, vcol = C_Q // C_KV, C_Q // C_KV + 1

    def kv_spec(j, colblk):
        return pl.BlockSpec(
            (C_KR, C_KV),
            lambda b, i: (b * nblk * kper + jnp.maximum(i * kper + j - 1, 0), colblk))

    return pl.pallas_call(
        _attn_c_kernel,
        grid=(batch, nblk),
        in_specs=[pl.BlockSpec(memory_space=pltpu.SMEM),
                  pl.BlockSpec((ATT_R, C_Q), lambda b, i: (b * nblk + i, 0))]
        + [kv_spec(j, kcol) for j in range(C_KBLK)] + [kv_spec(j, vcol) for j in range(C_KBLK)]
        + [pl.BlockSpec((H_C, C_KR, 2 * C_KR), lambda b, i: (0, 0, 0))],
        out_specs=pl.BlockSpec((ATT_R, C_Q), lambda b, i: (b * nblk + i, 0)),
        out_shape=jax.ShapeDtypeStruct((n, C_Q), BF16),
        compiler_params=_params("parallel", "parallel"),
        name="attn_c",
    )(sinks, qkv, qkv, qkv, qkv, qkv, qkv, qkv, _attn_c_table())


MERGE_TM = 512
MERGE_SUB = 256


MERGE_WR = 512


def _merge_kernel(layer, x_ref, oa_ref, ob_ref, oc_ref, wg_hbm, bg_ref, wa_hbm, wb_hbm, wc_hbm, wo_hbm,
                  g_ref, b_ref, o_ref, wg_ref, wa_ref, wb_ref, wc_ref, wo_ref, stage, sem):
    @pl.when(pl.program_id(0) == 0)
    def _():
        r = MERGE_WR
        pairs = [(wg_hbm.at[layer, pl.ds(i * r, r), pl.ds(j * D_MODEL, D_MODEL)],
                  wg_ref.at[pl.ds(i * r, r), pl.ds(j * D_MODEL, D_MODEL)])
                 for i in range(D_MODEL // r) for j in range(N_BRANCH)]
        pairs += [(src.at[layer], dst) for src, dst in
                  ((wa_hbm, wa_ref), (wb_hbm, wb_ref), (wc_hbm, wc_ref))]
        pairs += [(wo_hbm.at[layer, pl.ds(i * r, r), :], wo_ref.at[pl.ds(i * r, r), :])
                  for i in range(D_MODEL // r)]
        _stage_cast(pairs, stage, sem)

    for s in range(MERGE_TM // MERGE_SUB):
        rows = slice(s * MERGE_SUB, (s + 1) * MERGE_SUB)
        x = x_ref[rows, :]
        xb = x.astype(BF16)
        merged = None
        for r, (br_ref, w_ref) in enumerate(((oa_ref, wa_ref), (ob_ref, wb_ref), (oc_ref, wc_ref))):
            lo = r * D_MODEL
            gate = jax.nn.sigmoid(_dot(xb, wg_ref[:, lo:lo + D_MODEL]) + bg_ref[:, lo:lo + D_MODEL])
            term = gate * _dot(br_ref[rows, :], w_ref[...])
            merged = term if merged is None else merged + term
        y = ALPHA * x + _dot(merged.astype(BF16), wo_ref[...])
        o_ref[rows, :] = _layer_norm(y, g_ref[...], b_ref[...])


def _merge(x, oa, ob, oc, w_gate, b_gate, w_a, w_b, w_c, w_out, g, b, layer):
    n = x.shape[0]
    assert A_V == B_W == C_Q == MERGE_WR
    row = lambda w: pl.BlockSpec((MERGE_TM, w), lambda i: (i, 0))
    const = lambda r, w: pl.BlockSpec((r, w), lambda i: (0, 0))
    hbm = pl.BlockSpec(memory_space=pl.ANY)
    return pl.pallas_call(
        functools.partial(_merge_kernel, layer),
        grid=(n // MERGE_TM,),
        in_specs=[row(D_MODEL), row(A_V), row(B_W), row(C_Q),
                  hbm, const(1, N_BRANCH * D_MODEL), hbm, hbm, hbm, hbm,
                  const(1, D_MODEL), const(1, D_MODEL)],
        out_specs=row(D_MODEL),
        out_shape=jax.ShapeDtypeStruct((n, D_MODEL), F32),
        scratch_shapes=[pltpu.VMEM((D_MODEL, N_BRANCH * D_MODEL), BF16),
                        pltpu.VMEM((A_V, D_MODEL), BF16), pltpu.VMEM((B_W, D_MODEL), BF16),
                        pltpu.VMEM((C_Q, D_MODEL), BF16), pltpu.VMEM((D_MODEL, D_MODEL), BF16),
                        pltpu.VMEM((2, MERGE_WR, D_MODEL), F32), pltpu.SemaphoreType.DMA((2,))],
        compiler_params=_params("arbitrary"),
        name="merge_ln",
    )(x, oa, ob, oc, w_gate, b_gate, w_a, w_b, w_c, w_out, g, b)


def _lane_row(vals, offset):
    return jnp.zeros((1, LANES), F32).at[0, offset:offset + vals.shape[0]].set(vals.astype(F32))


def kernel(x, ln1_g, ln1_b, w_ff1_in, w_ff1_out, w_in, b_in, conv_w, a_log, dt_bias, gdn_norm_g,
           rel_bias, sinks, w_gate, b_gate, w_br_a, w_br_b, w_br_c, w_out, ln2_g, ln2_b,
           w_ff2_in, w_ff2_out, ln3_g, ln3_b):
    batch, seq, d = x.shape
    assert d == D_MODEL and seq % ATT_R == 0 and (batch * seq) % FFN_TM == 0
    h = x.reshape(batch * seq, d)
    row = lambda v: v.reshape(1, -1).astype(F32)
    o_small = QKV_A
    o_z = o_small + 2 * H_A
    o_b = o_z + A_V
    o_c = o_b + 3 * B_W
    bias_strips = _attn_b_bias(rel_bias.reshape(DEPTH * H_B, CHUNK + REL_CLIP))
    for l in range(DEPTH):
        h = _ffn(h, w_ff1_in, w_ff1_out, row(ln1_g[l]), row(ln1_b[l]), l)
        wi, bi = w_in[l], b_in[l]
        pad = SMALL_W - 2 * H_A
        wa = jnp.concatenate([wi[:, 0:QKV_A], wi[:, o_z:o_b]], axis=1).astype(BF16)
        ws = jnp.pad(wi[:, o_small:o_z], ((0, 0), (0, pad))).astype(BF16)
        ba = jnp.concatenate([bi[0:QKV_A], bi[o_z:o_b]])
        bs = jnp.pad(bi[o_small:o_z], (0, pad))
        pa, ps, pb, pc = _proj(h, wa, ws, wi[:, o_b:o_c].astype(BF16), wi[:, o_c:].astype(BF16),
                               row(ba), row(bs), row(bi[o_b:o_c]), row(bi[o_c:]))
        o_a = _gdn(pa, ps, conv_w[l].astype(F32), _lane_row(a_log[l], H_A), _lane_row(dt_bias[l], H_A),
                   row(gdn_norm_g[l]), batch, seq)
        o_bb = _attn_b(pb, bias_strips, batch, seq, l)
        o_cc = _attn_c(pc, sinks[l].astype(F32), batch, seq)
        h = _merge(h, o_a, o_bb, o_cc, w_gate, row(b_gate[l]), w_br_a, w_br_b, w_br_c, w_out,
                   row(ln2_g[l]), row(ln2_b[l]), l)
        h = _ffn(h, w_ff2_in, w_ff2_out, row(ln3_g[l]), row(ln3_b[l]), l)
    return h.reshape(batch, seq, d)
```

```python
import functools
import math

import jax
import jax.numpy as jnp
import numpy as np
from jax import lax
from jax.experimental import pallas as pl
from jax.experimental.pallas import tpu as pltpu

F32 = jnp.float32
BF16 = jnp.bfloat16

D_MODEL = 1024
DEPTH = 2
CHUNK = 64
D_FF = 4096
LN_EPS = 1e-5
NORM_EPS = 1e-6
H_A, DK_A, DV_A, CONV_K = 4, 128, 128, 4
A_QK = H_A * DK_A
A_V = H_A * DV_A
H_B, D_B, B_PREV, REL_CLIP = 4, 128, 8, 128
B_W = H_B * D_B
H_C, HKV_C, D_C, WINDOW = 8, 2, 64, 128
G_C = H_C // HKV_C
C_PREV = WINDOW // CHUNK
C_Q = H_C * D_C
C_KV = HKV_C * D_C
N_BRANCH = 3
ALPHA = (2.0 * DEPTH) ** 0.25

LANES = 128
VMEM_LIMIT = 56 * 1024 * 1024
NEG_BIG = -1e30

SMALL_W = LANES
QKV_A = 2 * A_QK + A_V
A_W = QKV_A + A_V
C_W = C_Q + 2 * C_KV


def _dot(a, b):
    return jnp.dot(a, b, preferred_element_type=F32)


def _dot_nt(a, b):
    return lax.dot_general(a, b, (((1,), (1,)), ((), ())), preferred_element_type=F32)


def _layer_norm(y, g, b):
    mu = jnp.mean(y, axis=-1, keepdims=True)
    yc = y - mu
    var = jnp.mean(yc * yc, axis=-1, keepdims=True)
    return yc * lax.rsqrt(var + LN_EPS) * g + b


def _silu(x):
    return x * jax.nn.sigmoid(x)


def _params(*sem):
    return pltpu.CompilerParams(dimension_semantics=sem, vmem_limit_bytes=VMEM_LIMIT)


FFN_TM = 512
FFN_SUB = 256
FFN_TF = 1024


FFN_WCH = 1024


def _stage_cast(pairs, stage, sem):
    copies = [pltpu.make_async_copy(src, stage.at[c % 2], sem.at[c % 2])
              for c, (src, _) in enumerate(pairs)]
    copies[0].start()
    for c, (_, dst) in enumerate(pairs):
        if c + 1 < len(pairs):
            copies[c + 1].start()
        copies[c].wait()
        dst[...] = stage[c % 2].astype(BF16)


def _ffn_kernel(layer, x_ref, wi_hbm, wo_hbm, g_ref, b_ref, o_ref, wi_ref, wo_ref, h_ref, stage, sem):
    @pl.when(pl.program_id(0) == 0)
    def _():
        w = FFN_WCH
        pairs = [(wi_hbm.at[layer, :, pl.ds(c * w, w)], wi_ref.at[:, pl.ds(c * w, w)])
                 for c in range(2 * D_FF // w)]
        pairs += [(wo_hbm.at[layer, pl.ds(c * w, w), :], wo_ref.at[pl.ds(c * w, w), :])
                  for c in range(D_FF // w)]
        _stage_cast(pairs, stage, sem)

    for s in range(FFN_TM // FFN_SUB):
        rows = slice(s * FFN_SUB, (s + 1) * FFN_SUB)
        x = x_ref[rows, :]
        xb = x.astype(BF16)
        for f in range(D_FF // FFN_TF):
            gate = _dot(xb, wi_ref[:, f * FFN_TF:(f + 1) * FFN_TF])
            up = _dot(xb, wi_ref[:, D_FF + f * FFN_TF:D_FF + (f + 1) * FFN_TF])
            h_ref[rows, f * FFN_TF:(f + 1) * FFN_TF] = (_silu(gate) * up).astype(BF16)
        y = ALPHA * x + 0.5 * _dot(h_ref[rows, :], wo_ref[...])
        o_ref[rows, :] = _layer_norm(y, g_ref[...], b_ref[...])


def _ffn(x, w_in, w_out, g, b, layer):
    n = x.shape[0]
    resident = lambda r, w: pl.BlockSpec((r, w), lambda i: (0, 0), pipeline_mode=pl.Buffered(1))
    return pl.pallas_call(
        functools.partial(_ffn_kernel, layer),
        grid=(n // FFN_TM,),
        in_specs=[
            pl.BlockSpec((FFN_TM, D_MODEL), lambda i: (i, 0)),
            pl.BlockSpec(memory_space=pl.ANY), pl.BlockSpec(memory_space=pl.ANY),
            resident(1, D_MODEL), resident(1, D_MODEL),
        ],
        out_specs=pl.BlockSpec((FFN_TM, D_MODEL), lambda i: (i, 0)),
        out_shape=jax.ShapeDtypeStruct((n, D_MODEL), F32),
        scratch_shapes=[pltpu.VMEM((D_MODEL, 2 * D_FF), BF16), pltpu.VMEM((D_FF, D_MODEL), BF16),
                        pltpu.VMEM((FFN_TM, D_FF), BF16),
                        pltpu.VMEM((2, FFN_WCH, FFN_WCH), F32), pltpu.SemaphoreType.DMA((2,))],
        compiler_params=_params("arbitrary"),
        name="ffn_ln",
    )(x, w_in, w_out, g, b)


PROJ_TM = 512


def _proj_kernel(x_ref, wa_ref, ws_ref, wb_ref, wc_ref, ba_ref, bs_ref, bb_ref, bc_ref,
                 oa_ref, os_ref, ob_ref, oc_ref):
    xb = x_ref[...].astype(BF16)
    oa_ref[...] = _dot(xb, wa_ref[...]) + ba_ref[...]
    os_ref[...] = _dot(xb, ws_ref[...]) + bs_ref[...]
    yb = _dot(xb, wb_ref[...]) + bb_ref[...]
    ob_ref[:, 0:B_W] = (yb[:, 0:B_W] * (D_B ** -0.5)).astype(BF16)
    ob_ref[:, B_W:3 * B_W] = yb[:, B_W:3 * B_W].astype(BF16)
    yc = _dot(xb, wc_ref[...]) + bc_ref[...]
    oc_ref[:, 0:C_Q] = (yc[:, 0:C_Q] * (D_C ** -0.5)).astype(BF16)
    oc_ref[:, C_Q:C_W] = yc[:, C_Q:C_W].astype(BF16)


def _proj(x, wa, ws, wb, wc, ba, bs, bb, bc):
    n = x.shape[0]
    widths = (A_W, SMALL_W, 3 * B_W, C_W)
    row = lambda w: pl.BlockSpec((PROJ_TM, w), lambda i: (i, 0))
    const = lambda r, w: pl.BlockSpec((r, w), lambda i: (0, 0))
    return pl.pallas_call(
        _proj_kernel,
        grid=(n // PROJ_TM,),
        in_specs=[row(D_MODEL)] + [const(D_MODEL, w) for w in widths] + [const(1, w) for w in widths],
        out_specs=[row(w) for w in widths],
        out_shape=[jax.ShapeDtypeStruct((n, A_W), F32), jax.ShapeDtypeStruct((n, SMALL_W), F32),
                   jax.ShapeDtypeStruct((n, 3 * B_W), BF16), jax.ShapeDtypeStruct((n, C_W), BF16)],
        compiler_params=_params("parallel"),
        name="in_proj",
    )(x, wa, ws, wb, wc, ba, bs, bb, bc)


GDN_R = 256
GDN_NC = GDN_R // CHUNK
GDN_PAIR = 2 * CHUNK
CONV_PAD = 8
LOG2_CHUNK = CHUNK.bit_length() - 1
NEUMANN_STEPS = LOG2_CHUNK - 1


def _gdn_kernel(a_ref, s_ref, cw_ref, alog_ref, dtb_ref, ng_ref, o_ref, xbuf, st_ref):
    i = pl.program_id(1)

    @pl.when(i == 0)
    def _():
        st_ref[...] = jnp.zeros_like(st_ref)
        xbuf[...] = jnp.zeros_like(xbuf)

    x_new = a_ref[:, 0:QKV_A]
    n_grp = GDN_R // CONV_PAD
    x3 = x_new.reshape(n_grp, CONV_PAD, QKV_A)
    hist = xbuf[...].reshape(1, CONV_PAD, QKV_A)
    sub = lax.broadcasted_iota(jnp.int32, (n_grp, CONV_PAD, QKV_A), 1)
    acc = cw_ref[CONV_K - 1:CONV_K, :].reshape(1, 1, QKV_A) * x3
    for s in range(1, CONV_K):
        rot = pltpu.roll(x3, s, axis=1)
        prev = jnp.concatenate([pltpu.roll(hist, s, axis=1), rot[0:n_grp - 1]], axis=0)
        tap = cw_ref[CONV_K - 1 - s:CONV_K - s, :].reshape(1, 1, QKV_A)
        acc = acc + tap * jnp.where(sub >= s, rot, prev)
    xbuf[...] = x_new[GDN_R - CONV_PAD:GDN_R]
    qkv = _silu(acc.reshape(GDN_R, QKV_A))

    s = s_ref[...]
    beta_all = jax.nn.sigmoid(s)
    g_all = -jnp.exp(alog_ref[...]) * jax.nn.softplus(s + dtb_ref[...])
    row = lax.broadcasted_iota(jnp.int32, (GDN_PAIR, GDN_PAIR), 0)
    col = lax.broadcasted_iota(jnp.int32, (GDN_PAIR, GDN_PAIR), 1)
    same_chunk = jnp.right_shift(row, LOG2_CHUNK) == jnp.right_shift(col, LOG2_CHUNK)
    causal = same_chunk & (col <= row)
    strict = same_chunk & (col < row)
    eye = jnp.where(row == col, 1.0, 0.0).astype(F32)
    pairs = [slice(p * GDN_PAIR, (p + 1) * GDN_PAIR) for p in range(GDN_R // GDN_PAIR)]
    g_hi = g_all.astype(BF16)
    g_r1 = g_all - g_hi.astype(F32)
    g_mid = g_r1.astype(BF16)
    g_lo = (g_r1 - g_mid.astype(F32)).astype(BF16)
    g_split = jnp.concatenate([g_hi, g_mid, g_lo], axis=1)
    tri = jnp.where(causal, 1.0, 0.0).astype(BF16)
    g_parts = jnp.concatenate([_dot(tri, g_split[rows]) for rows in pairs], axis=0)
    gcum = g_parts[:, 0:LANES] + g_parts[:, LANES:2 * LANES] + g_parts[:, 2 * LANES:3 * LANES]
    gsum = jnp.concatenate(
        [jnp.broadcast_to(gcum[(c + 1) * CHUNK - 1:(c + 1) * CHUNK, :], (CHUNK, LANES))
         for c in range(GDN_NC)], axis=0)
    gcum_t = gcum.T
    ng = ng_ref[...]
    heads = range(H_A)

    jobs = [(h, pr) for h in heads for pr in range(len(pairs))]
    qn, rhs, glast, kdec = [], [], [], []
    kk, qk, decay = {}, {}, {}
    for h in heads:
        lo = h * DK_A
        qh = qkv[:, lo:lo + DK_A]
        kh = qkv[:, A_QK + lo:A_QK + lo + DK_A]
        vh = qkv[:, 2 * A_QK + lo:2 * A_QK + lo + DV_A]
        q_h = qh * (lax.rsqrt(jnp.sum(qh * qh, -1, keepdims=True) + NORM_EPS) * (DK_A ** -0.5))
        k_h = kh * lax.rsqrt(jnp.sum(kh * kh, -1, keepdims=True) + NORM_EPS)
        bcol = beta_all[:, h:h + 1]
        g_h = gcum[:, H_A + h:H_A + h + 1]
        gl_h = gsum[:, H_A + h:H_A + h + 1]
        eg = jnp.exp(g_h)
        kb = k_h * bcol
        k_b = k_h.astype(BF16)
        kb_b = kb.astype(BF16)
        q_b = q_h.astype(BF16)
        g_row = gcum_t[H_A + h:H_A + h + 1, :]
        for pr, rows in enumerate(pairs):
            job = (h, pr)
            kk[job] = _dot_nt(kb_b[rows], k_b[rows])
            qk[job] = _dot_nt(q_b[rows], k_b[rows])
            decay[job] = jnp.exp(jnp.where(causal, g_h[rows] - g_row[:, rows], -jnp.inf))
        rhs.append(jnp.concatenate([vh * bcol, kb * eg], axis=1).astype(BF16))
        qn.append(q_h * eg)
        kdec.append(k_h * jnp.exp(gl_h - g_h))
        glast.append(gl_h)

    p = {job: jnp.where(strict, -(kk[job] * decay[job]), 0.0) for job in jobs}
    inv = {job: eye + p[job] for job in jobs}
    for _ in range(NEUMANN_STEPS):
        pb = {job: p[job].astype(BF16) for job in jobs}
        p = {job: _dot(pb[job], pb[job]) for job in jobs}
        inv = {job: inv[job] + _dot(p[job].astype(BF16), inv[job].astype(BF16)) for job in jobs}
    uw_j = {(h, pr): _dot(inv[(h, pr)].astype(BF16), rhs[h][pairs[pr]]).astype(BF16)
            for h, pr in jobs}
    aw_j = {job: _dot((qk[job] * decay[job]).astype(BF16), uw_j[job]) for job in jobs}
    uw = [jnp.concatenate([uw_j[(h, pr)] for pr in range(len(pairs))], axis=0) for h in heads]
    aw = [jnp.concatenate([aw_j[(h, pr)] for pr in range(len(pairs))], axis=0) for h in heads]
    qt = [(qn[h] - aw[h][:, DV_A:DV_A + DK_A]).astype(BF16) for h in heads]

    bp = [[_dot(kdec[h][c * CHUNK:(c + 1) * CHUNK].T.astype(BF16), uw[h][c * CHUNK:(c + 1) * CHUNK])
           for c in range(GDN_NC)] for h in heads]
    state = [st_ref[h] for h in heads]
    for c in range(GDN_NC):
        r0 = c * CHUNK
        for h in heads:
            lo = h * DV_A
            s_b = state[h].astype(BF16)
            o = _dot(qt[h][r0:r0 + CHUNK], s_b) + aw[h][r0:r0 + CHUNK, 0:DV_A]
            decay_c = jnp.exp(glast[h][r0:r0 + 1])
            state[h] = (state[h] * decay_c + bp[h][c][:, 0:DV_A]
                        - _dot(bp[h][c][:, DV_A:DV_A + DK_A].astype(BF16), s_b))
            z = a_ref[r0:r0 + CHUNK, QKV_A + lo:QKV_A + lo + DV_A]
            on = o * lax.rsqrt(jnp.mean(o * o, -1, keepdims=True) + NORM_EPS) * ng * _silu(z)
            o_ref[r0:r0 + CHUNK, lo:lo + DV_A] = on.astype(BF16)
    for h in heads:
        st_ref[h] = state[h]


def _gdn(a, s, conv_w, alog_row, dtb_row, norm_g, batch, seq):
    n = a.shape[0]
    nblk = seq // GDN_R
    const = lambda r, w: pl.BlockSpec((r, w), lambda b, i: (0, 0))
    return pl.pallas_call(
        _gdn_kernel,
        grid=(batch, nblk),
        in_specs=[
            pl.BlockSpec((GDN_R, A_W), lambda b, i: (b * nblk + i, 0)),
            pl.BlockSpec((GDN_R, SMALL_W), lambda b, i: (b * nblk + i, 0)),
            const(CONV_K, QKV_A), const(1, SMALL_W), const(1, SMALL_W), const(1, DV_A),
        ],
        out_specs=pl.BlockSpec((GDN_R, A_V), lambda b, i: (b * nblk + i, 0)),
        out_shape=jax.ShapeDtypeStruct((n, A_V), BF16),
        scratch_shapes=[pltpu.VMEM((CONV_PAD, QKV_A), F32),
                        pltpu.VMEM((H_A, DK_A, DV_A), F32)],
        compiler_params=_params("parallel", "arbitrary"),
        name="gdn",
    )(a, s, conv_w, alog_row, dtb_row, norm_g)


ATT_R = 256
ATT_CH = ATT_R // CHUNK
B_KBLK = 3
assert (B_KBLK - 1) * ATT_CH == B_PREV
B_STRIP_W = B_KBLK * ATT_R + (ATT_CH - 2) * CHUNK


def _attn_b_kernel(q_ref, k0_ref, k1_ref, k2_ref, v0_ref, v1_ref, v2_ref, bias_ref, o_ref):
    i = pl.program_id(1)
    k_refs = (k0_ref, k1_ref, k2_ref)
    v_refs = (v0_ref, v1_ref, v2_ref)
    heads = range(H_B)
    cols = lambda h: slice(h * D_B, (h + 1) * D_B)
    scores = []
    for h in heads:
        q = q_ref[:, cols(h)]
        parts = []
        for j in range(B_KBLK):
            sj = _dot_nt(q, k_refs[j][:, cols(h)])
            if j < B_KBLK - 1:
                sj = jnp.where(i + j >= B_KBLK - 1, sj, NEG_BIG)
            parts.append(sj)
        bias = []
        for a in range(ATT_CH):
            shift = ATT_CH - 1 - a
            start = (shift - shift % 2) * CHUNK
            bias.append(bias_ref[h, shift % 2, :, start:start + B_KBLK * ATT_R])
        scores.append(jnp.concatenate(parts, axis=1) + jnp.concatenate(bias, axis=0))
    probs, denoms = [], []
    for s in scores:
        p = jnp.exp(s - jnp.max(s, axis=-1, keepdims=True))
        denoms.append(jnp.sum(p, axis=-1, keepdims=True))
        probs.append(p.astype(BF16))
    for h in heads:
        o = _dot(probs[h][:, 0:ATT_R], v_refs[0][:, cols(h)])
        for j in range(1, B_KBLK):
            o = o + _dot(probs[h][:, j * ATT_R:(j + 1) * ATT_R], v_refs[j][:, cols(h)])
        o_ref[:, cols(h)] = (o / denoms[h]).astype(BF16)


def _attn_b(qkv, bias, batch, seq, layer):
    n = qkv.shape[0]
    nblk = seq // ATT_R

    def kv_spec(j, colblk):
        return pl.BlockSpec(
            (ATT_R, B_W), lambda b, i: (b * nblk + jnp.maximum(i + j - (B_KBLK - 1), 0), colblk))

    return pl.pallas_call(
        _attn_b_kernel,
        grid=(batch, nblk),
        in_specs=[pl.BlockSpec((ATT_R, B_W), lambda b, i: (b * nblk + i, 0))]
        + [kv_spec(j, 1) for j in range(B_KBLK)] + [kv_spec(j, 2) for j in range(B_KBLK)]
        + [pl.BlockSpec((H_B, 2, CHUNK, B_STRIP_W), lambda b, i: (layer, 0, 0, 0))],
        out_specs=pl.BlockSpec((ATT_R, B_W), lambda b, i: (b * nblk + i, 0)),
        out_shape=jax.ShapeDtypeStruct((n, B_W), BF16),
        compiler_params=_params("parallel", "parallel"),
        name="attn_b",
    )(qkv, qkv, qkv, qkv, qkv, qkv, qkv, bias)


def _attn_b_bias(rel_bias):
    nh, nrel = rel_bias.shape
    assert nrel == CHUNK + REL_CLIP
    lead = (ATT_CH - 1) * CHUNK
    vis_end = lead + (B_PREV + 1) * CHUNK
    width = B_KBLK * ATT_R + lead
    period = 1 << (width + CHUNK - 1).bit_length()
    rb = rel_bias.astype(F32)
    far = jnp.broadcast_to(rb[:, -1:], (nh, period))
    f = jnp.concatenate([far[:, :vis_end - nrel], jnp.flip(rb, axis=1), far[:, :period - vis_end]], axis=1)
    g = jnp.concatenate([f[:, :1], jnp.flip(f[:, 1:], axis=1)], axis=1)
    skew = jnp.tile(g, (1, CHUNK + 1))[:, :CHUNK * (period + 1)].reshape(nh, CHUNK, period + 1)
    toep = jnp.flip(skew[:, :, period - width + 1:period + 1], axis=2)
    c = np.arange(width)
    strip = jnp.where(jnp.asarray((c >= lead) & (c < vis_end))[None, None, :], toep, NEG_BIG)
    return jnp.stack([strip[:, :, 0:B_STRIP_W], strip[:, :, CHUNK:CHUNK + B_STRIP_W]], axis=1)


C_KR = 2 * CHUNK
C_KBLK = 3


def _attn_c_kernel(sink_ref, q_ref, k0_ref, k1_ref, k2_ref, v0_ref, v1_ref, v2_ref, tbl_ref, o_ref):
    i = pl.program_id(1)
    k_refs = (k0_ref, k1_ref, k2_ref)
    v_refs = (v0_ref, v1_ref, v2_ref)
    jobs = [(h, half) for h in range(H_C) for half in range(ATT_R // C_KR)]
    kv_cols = lambda h: slice((h // G_C) * D_C, (h // G_C + 1) * D_C)
    scores = []
    for h, half in jobs:
        q = q_ref[half * C_KR:(half + 1) * C_KR, h * D_C:(h + 1) * D_C]
        s_prev = _dot_nt(q, k_refs[half][:, kv_cols(h)])
        if half == 0:
            s_prev = jnp.where(i > 0, s_prev, NEG_BIG)
        s_own = _dot_nt(q, k_refs[half + 1][:, kv_cols(h)])
        scores.append(jnp.concatenate([s_prev, s_own], axis=1) + tbl_ref[h])
    probs, denoms = [], []
    for (h, half), s in zip(jobs, scores):
        sink = sink_ref[h]
        mx = jnp.maximum(jnp.max(s, axis=-1, keepdims=True), sink)
        p = jnp.exp(s - mx)
        denoms.append(jnp.sum(p, axis=-1, keepdims=True) + jnp.exp(sink - mx))
        probs.append(p.astype(BF16))
    for (h, half), pb, denom in zip(jobs, probs, denoms):
        o = (_dot(pb[:, 0:C_KR], v_refs[half][:, kv_cols(h)])
             + _dot(pb[:, C_KR:2 * C_KR], v_refs[half + 1][:, kv_cols(h)]))
        o_ref[half * C_KR:(half + 1) * C_KR, h * D_C:(h + 1) * D_C] = (o / denom).astype(BF16)


def _attn_c_table():
    tq = np.arange(C_KR)[:, None] + C_KR
    tk = np.arange(2 * C_KR)[None, :]
    delta = tq // CHUNK - tk // CHUNK
    vis = (delta >= 0) & (delta <= C_PREV)
    slopes = 2.0 ** (-8.0 * np.arange(1, H_C + 1) / H_C)
    tbl = np.where(vis[None], -slopes[:, None, None] * np.abs(tq - tk)[None].astype(np.float64), NEG_BIG)
    return jnp.asarray(tbl, F32)


def _attn_c(qkv, sinks, batch, seq):
    n = qkv.shape[0]
    nblk = seq // ATT_R
    kper = ATT_R // C_KR
    kcol, vcol = C_Q // C_KV, C_Q // C_KV + 1

    def kv_spec(j, colblk):
        return pl.BlockSpec(
            (C_KR, C_KV),
            lambda b, i: (b * nblk * kper + jnp.maximum(i * kper + j - 1, 0), colblk))

    return pl.pallas_call(
        _attn_c_kernel,
        grid=(batch, nblk),
        in_specs=[pl.BlockSpec(memory_space=pltpu.SMEM),
                  pl.BlockSpec((ATT_R, C_Q), lambda b, i: (b * nblk + i, 0))]
        + [kv_spec(j, kcol) for j in range(C_KBLK)] + [kv_spec(j, vcol) for j in range(C_KBLK)]
        + [pl.BlockSpec((H_C, C_KR, 2 * C_KR), lambda b, i: (0, 0, 0))],
        out_specs=pl.BlockSpec((ATT_R, C_Q), lambda b, i: (b * nblk + i, 0)),
        out_shape=jax.ShapeDtypeStruct((n, C_Q), BF16),
        compiler_params=_params("parallel", "parallel"),
        name="attn_c",
    )(sinks, qkv, qkv, qkv, qkv, qkv, qkv, qkv, _attn_c_table())


MERGE_TM = 512
MERGE_SUB = 256


MERGE_WR = 512


def _merge_kernel(layer, x_ref, oa_ref, ob_ref, oc_ref, wg_hbm, bg_ref, wa_hbm, wb_hbm, wc_hbm, wo_hbm,
                  g_ref, b_ref, o_ref, wg_ref, wa_ref, wb_ref, wc_ref, wo_ref, stage, sem):
    @pl.when(pl.program_id(0) == 0)
    def _():
        r = MERGE_WR
        pairs = [(wg_hbm.at[layer, pl.ds(i * r, r), pl.ds(j * D_MODEL, D_MODEL)],
                  wg_ref.at[pl.ds(i * r, r), pl.ds(j * D_MODEL, D_MODEL)])
                 for i in range(D_MODEL // r) for j in range(N_BRANCH)]
        pairs += [(src.at[layer], dst) for src, dst in
                  ((wa_hbm, wa_ref), (wb_hbm, wb_ref), (wc_hbm, wc_ref))]
        pairs += [(wo_hbm.at[layer, pl.ds(i * r, r), :], wo_ref.at[pl.ds(i * r, r), :])
                  for i in range(D_MODEL // r)]
        _stage_cast(pairs, stage, sem)

    for s in range(MERGE_TM // MERGE_SUB):
        rows = slice(s * MERGE_SUB, (s + 1) * MERGE_SUB)
        x = x_ref[rows, :]
        xb = x.astype(BF16)
        merged = None
        for r, (br_ref, w_ref) in enumerate(((oa_ref, wa_ref), (ob_ref, wb_ref), (oc_ref, wc_ref))):
            lo = r * D_MODEL
            gate = jax.nn.sigmoid(_dot(xb, wg_ref[:, lo:lo + D_MODEL]) + bg_ref[:, lo:lo + D_MODEL])
            term = gate * _dot(br_ref[rows, :], w_ref[...])
            merged = term if merged is None else merged + term
        y = ALPHA * x + _dot(merged.astype(BF16), wo_ref[...])
        o_ref[rows, :] = _layer_norm(y, g_ref[...], b_ref[...])


def _merge(x, oa, ob, oc, w_gate, b_gate, w_a, w_b, w_c, w_out, g, b, layer):
    n = x.shape[0]
    assert A_V == B_W == C_Q == MERGE_WR
    row = lambda w: pl.BlockSpec((MERGE_TM, w), lambda i: (i, 0))
    const = lambda r, w: pl.BlockSpec((r, w), lambda i: (0, 0))
    hbm = pl.BlockSpec(memory_space=pl.ANY)
    return pl.pallas_call(
        functools.partial(_merge_kernel, layer),
        grid=(n // MERGE_TM,),
        in_specs=[row(D_MODEL), row(A_V), row(B_W), row(C_Q),
                  hbm, const(1, N_BRANCH * D_MODEL), hbm, hbm, hbm, hbm,
                  const(1, D_MODEL), const(1, D_MODEL)],
        out_specs=row(D_MODEL),
        out_shape=jax.ShapeDtypeStruct((n, D_MODEL), F32),
        scratch_shapes=[pltpu.VMEM((D_MODEL, N_BRANCH * D_MODEL), BF16),
                        pltpu.VMEM((A_V, D_MODEL), BF16), pltpu.VMEM((B_W, D_MODEL), BF16),
                        pltpu.VMEM((C_Q, D_MODEL), BF16), pltpu.VMEM((D_MODEL, D_MODEL), BF16),
                        pltpu.VMEM((2, MERGE_WR, D_MODEL), F32), pltpu.SemaphoreType.DMA((2,))],
        compiler_params=_params("arbitrary"),
        name="merge_ln",
    )(x, oa, ob, oc, w_gate, b_gate, w_a, w_b, w_c, w_out, g, b)


def _lane_row(vals, offset):
    return jnp.zeros((1, LANES), F32).at[0, offset:offset + vals.shape[0]].set(vals.astype(F32))


def kernel(x, ln1_g, ln1_b, w_ff1_in, w_ff1_out, w_in, b_in, conv_w, a_log, dt_bias, gdn_norm_g,
           rel_bias, sinks, w_gate, b_gate, w_br_a, w_br_b, w_br_c, w_out, ln2_g, ln2_b,
           w_ff2_in, w_ff2_out, ln3_g, ln3_b):
    batch, seq, d = x.shape
    assert d == D_MODEL and seq % ATT_R == 0 and (batch * seq) % FFN_TM == 0
    h = x.reshape(batch * seq, d)
    row = lambda v: v.reshape(1, -1).astype(F32)
    o_small = QKV_A
    o_z = o_small + 2 * H_A
    o_b = o_z + A_V
    o_c = o_b + 3 * B_W
    bias_strips = _attn_b_bias(rel_bias.reshape(DEPTH * H_B, CHUNK + REL_CLIP))
    for l in range(DEPTH):
        h = _ffn(h, w_ff1_in, w_ff1_out, row(ln1_g[l]), row(ln1_b[l]), l)
        wi, bi = w_in[l], b_in[l]
        pad = SMALL_W - 2 * H_A
        wa = jnp.concatenate([wi[:, 0:QKV_A], wi[:, o_z:o_b]], axis=1).astype(BF16)
        ws = jnp.pad(wi[:, o_small:o_z], ((0, 0), (0, pad))).astype(BF16)
        ba = jnp.concatenate([bi[0:QKV_A], bi[o_z:o_b]])
        bs = jnp.pad(bi[o_small:o_z], (0, pad))
        pa, ps, pb, pc = _proj(h, wa, ws, wi[:, o_b:o_c].astype(BF16), wi[:, o_c:].astype(BF16),
                               row(ba), row(bs), row(bi[o_b:o_c]), row(bi[o_c:]))
        o_a = _gdn(pa, ps, conv_w[l].astype(F32), _lane_row(a_log[l], H_A), _lane_row(dt_bias[l], H_A),
                   row(gdn_norm_g[l]), batch, seq)
        o_bb = _attn_b(pb, bias_strips, batch, seq, l)
        o_cc = _attn_c(pc, sinks[l].astype(F32), batch, seq)
        h = _merge(h, o_a, o_bb, o_cc, w_gate, row(b_gate[l]), w_br_a, w_br_b, w_br_c, w_out,
                   row(ln2_g[l]), row(ln2_b[l]), l)
        h = _ffn(h, w_ff2_in, w_ff2_out, row(ln3_g[l]), row(ln3_b[l]), l)
    return h.reshape(batch, seq, d)
```

```python
import functools
import math

import jax
import jax.numpy as jnp
import numpy as np
from jax import lax
from jax.experimental import pallas as pl
from jax.experimental.pallas import tpu as pltpu

F32 = jnp.float32
BF16 = jnp.bfloat16

D_MODEL = 1024
DEPTH = 2
CHUNK = 64
D_FF = 4096
LN_EPS = 1e-5
NORM_EPS = 1e-6
H_A, DK_A, DV_A, CONV_K = 4, 128, 128, 4
A_QK = H_A * DK_A
A_V = H_A * DV_A
H_B, D_B, B_PREV, REL_CLIP = 4, 128, 8, 128
B_W = H_B * D_B
H_C, HKV_C, D_C, WINDOW = 8, 2, 64, 128
G_C = H_C // HKV_C
C_PREV = WINDOW // CHUNK
C_Q = H_C * D_C
C_KV = HKV_C * D_C
N_BRANCH = 3
ALPHA = (2.0 * DEPTH) ** 0.25

LANES = 128
VMEM_LIMIT = 56 * 1024 * 1024
NEG_BIG = -1e30
LOG2E = math.log2(math.e)

SMALL_W = LANES
QKV_A = 2 * A_QK + A_V
A_W = QKV_A + A_V
C_W = C_Q + 2 * C_KV


def _dot(a, b):
    return jnp.dot(a, b, preferred_element_type=F32)


def _dot_nt(a, b):
    return lax.dot_general(a, b, (((1,), (1,)), ((), ())), preferred_element_type=F32)


def _layer_norm(y, g, b):
    mu = jnp.mean(y, axis=-1, keepdims=True)
    yc = y - mu
    var = jnp.mean(yc * yc, axis=-1, keepdims=True)
    return yc * lax.rsqrt(var + LN_EPS) * g + b


def _silu_of_half(xh):
    return xh + xh * jnp.tanh(xh)


def _silu(x):
    return _silu_of_half(0.5 * x)


def _params(*sem):
    return pltpu.CompilerParams(dimension_semantics=sem, vmem_limit_bytes=VMEM_LIMIT)


FFN_TM = 512
FFN_SUB = 256
FFN_TF = 1024


FFN_WCH = 1024


def _stage_cast(pairs, stage, sem):
    copies = [pltpu.make_async_copy(src, stage.at[c % 2], sem.at[c % 2])
              for c, (src, _) in enumerate(pairs)]
    copies[0].start()
    for c, (_, dst) in enumerate(pairs):
        if c + 1 < len(pairs):
            copies[c + 1].start()
        copies[c].wait()
        dst[...] = stage[c % 2].astype(BF16)


def _ffn_kernel(layer, x_ref, wi_hbm, wo_hbm, g_ref, b_ref, o_ref, wi_ref, wo_ref, h_ref, stage, sem):
    @pl.when(pl.program_id(0) == 0)
    def _():
        w = FFN_WCH
        pairs = [(wi_hbm.at[layer, :, pl.ds(c * w, w)], wi_ref.at[:, pl.ds(c * w, w)])
                 for c in range(2 * D_FF // w)]
        pairs += [(wo_hbm.at[layer, pl.ds(c * w, w), :], wo_ref.at[pl.ds(c * w, w), :])
                  for c in range(D_FF // w)]
        _stage_cast(pairs, stage, sem)

    for s in range(FFN_TM // FFN_SUB):
        rows = slice(s * FFN_SUB, (s + 1) * FFN_SUB)
        x = x_ref[rows, :]
        xb = x.astype(BF16)
        for f in range(D_FF // FFN_TF):
            gate = _dot(xb, wi_ref[:, f * FFN_TF:(f + 1) * FFN_TF])
            up = _dot(xb, wi_ref[:, D_FF + f * FFN_TF:D_FF + (f + 1) * FFN_TF])
            h_ref[rows, f * FFN_TF:(f + 1) * FFN_TF] = (_silu(gate) * up).astype(BF16)
        y = ALPHA * x + 0.5 * _dot(h_ref[rows, :], wo_ref[...])
        o_ref[rows, :] = _layer_norm(y, g_ref[...], b_ref[...])


def _ffn(x, w_in, w_out, g, b, layer):
    n = x.shape[0]
    resident = lambda r, w: pl.BlockSpec((r, w), lambda i: (0, 0), pipeline_mode=pl.Buffered(1))
    return pl.pallas_call(
        functools.partial(_ffn_kernel, layer),
        grid=(n // FFN_TM,),
        in_specs=[
            pl.BlockSpec((FFN_TM, D_MODEL), lambda i: (i, 0)),
            pl.BlockSpec(memory_space=pl.ANY), pl.BlockSpec(memory_space=pl.ANY),
            resident(1, D_MODEL), resident(1, D_MODEL),
        ],
        out_specs=pl.BlockSpec((FFN_TM, D_MODEL), lambda i: (i, 0)),
        out_shape=jax.ShapeDtypeStruct((n, D_MODEL), F32),
        scratch_shapes=[pltpu.VMEM((D_MODEL, 2 * D_FF), BF16), pltpu.VMEM((D_FF, D_MODEL), BF16),
                        pltpu.VMEM((FFN_TM, D_FF), BF16),
                        pltpu.VMEM((2, FFN_WCH, FFN_WCH), F32), pltpu.SemaphoreType.DMA((2,))],
        compiler_params=_params("arbitrary"),
        name="ffn_ln",
    )(x, w_in, w_out, g, b)


PROJ_TM = 512


def _proj_kernel(x_ref, wa_ref, ws_ref, wb_ref, wc_ref, ba_ref, bs_ref, bb_ref, bc_ref,
                 oa_ref, os_ref, ob_ref, oc_ref):
    xb = x_ref[...].astype(BF16)
    oa_ref[...] = _dot(xb, wa_ref[...]) + ba_ref[...]
    os_ref[...] = _dot(xb, ws_ref[...]) + bs_ref[...]
    yb = _dot(xb, wb_ref[...]) + bb_ref[...]
    ob_ref[:, 0:B_W] = (yb[:, 0:B_W] * (D_B ** -0.5 * LOG2E)).astype(BF16)
    ob_ref[:, B_W:3 * B_W] = yb[:, B_W:3 * B_W].astype(BF16)
    yc = _dot(xb, wc_ref[...]) + bc_ref[...]
    oc_ref[:, 0:C_Q] = (yc[:, 0:C_Q] * (D_C ** -0.5 * LOG2E)).astype(BF16)
    oc_ref[:, C_Q:C_W] = yc[:, C_Q:C_W].astype(BF16)


def _proj(x, wa, ws, wb, wc, ba, bs, bb, bc):
    n = x.shape[0]
    widths = (A_W, SMALL_W, 3 * B_W, C_W)
    row = lambda w: pl.BlockSpec((PROJ_TM, w), lambda i: (i, 0))
    const = lambda r, w: pl.BlockSpec((r, w), lambda i: (0, 0))
    return pl.pallas_call(
        _proj_kernel,
        grid=(n // PROJ_TM,),
        in_specs=[row(D_MODEL)] + [const(D_MODEL, w) for w in widths] + [const(1, w) for w in widths],
        out_specs=[row(w) for w in widths],
        out_shape=[jax.ShapeDtypeStruct((n, A_W), F32), jax.ShapeDtypeStruct((n, SMALL_W), F32),
                   jax.ShapeDtypeStruct((n, 3 * B_W), BF16), jax.ShapeDtypeStruct((n, C_W), BF16)],
        compiler_params=_params("parallel"),
        name="in_proj",
    )(x, wa, ws, wb, wc, ba, bs, bb, bc)


GDN_R = 256
GDN_NC = GDN_R // CHUNK
GDN_PAIR = 2 * CHUNK
CONV_PAD = 8
LOG2_CHUNK = CHUNK.bit_length() - 1
NEUMANN_STEPS = LOG2_CHUNK - 1


def _gdn_kernel(a_ref, s_ref, cw_ref, alog_ref, dtb_ref, ng_ref, o_ref, xbuf, st_ref):
    i = pl.program_id(1)

    @pl.when(i == 0)
    def _():
        st_ref[...] = jnp.zeros_like(st_ref)
        xbuf[...] = jnp.zeros_like(xbuf)

    assert CONV_K == 4
    x_new = a_ref[:, 0:QKV_A]
    n_grp = GDN_R // CONV_PAD
    x3 = x_new.reshape(n_grp, CONV_PAD, QKV_A)
    hist = xbuf[...].reshape(1, CONV_PAD, QKV_A)
    sub = lax.broadcasted_iota(jnp.int32, (n_grp, CONV_PAD, QKV_A), 1)
    w0, w1, w2, w3 = [0.5 * cw_ref[j:j + 1, :].reshape(1, 1, QKV_A) for j in range(CONV_K)]

    def delay(v, v_hist, s):
        rot = pltpu.roll(v, s, axis=1)
        prev = jnp.concatenate([pltpu.roll(v_hist, s, axis=1), rot[0:n_grp - 1]], axis=0)
        return jnp.where(sub >= s, rot, prev)

    dx = delay(x3, hist, 1)
    inner = w1 * x3 + w0 * dx
    inner_hist = w1 * hist + w0 * pltpu.roll(hist, 1, axis=1)
    y_half = w3 * x3 + w2 * dx + delay(inner, inner_hist, 2)
    xbuf[...] = x_new[GDN_R - CONV_PAD:GDN_R]
    qkv = _silu_of_half(y_half.reshape(GDN_R, QKV_A))

    s = s_ref[...]
    beta_all = jax.nn.sigmoid(s)
    g_all = -jnp.exp(alog_ref[...]) * jax.nn.softplus(s + dtb_ref[...])
    row = lax.broadcasted_iota(jnp.int32, (GDN_PAIR, GDN_PAIR), 0)
    col = lax.broadcasted_iota(jnp.int32, (GDN_PAIR, GDN_PAIR), 1)
    same_chunk = jnp.right_shift(row, LOG2_CHUNK) == jnp.right_shift(col, LOG2_CHUNK)
    causal = same_chunk & (col <= row)
    strict = same_chunk & (col < row)
    eye = jnp.where(row == col, 1.0, 0.0).astype(F32)
    pairs = [slice(p * GDN_PAIR, (p + 1) * GDN_PAIR) for p in range(GDN_R // GDN_PAIR)]
    g_hi = g_all.astype(BF16)
    g_r1 = g_all - g_hi.astype(F32)
    g_mid = g_r1.astype(BF16)
    g_lo = (g_r1 - g_mid.astype(F32)).astype(BF16)
    g_split = jnp.concatenate([g_hi, g_mid, g_lo], axis=1)
    tri = jnp.where(causal, 1.0, 0.0).astype(BF16)
    g_parts = jnp.concatenate([_dot(tri, g_split[rows]) for rows in pairs], axis=0)
    gcum = g_parts[:, 0:LANES] + g_parts[:, LANES:2 * LANES] + g_parts[:, 2 * LANES:3 * LANES]
    gsum = jnp.concatenate(
        [jnp.broadcast_to(gcum[(c + 1) * CHUNK - 1:(c + 1) * CHUNK, :], (CHUNK, LANES))
         for c in range(GDN_NC)], axis=0)
    gcum_t = gcum.T
    ng = ng_ref[...]
    heads = range(H_A)

    jobs = [(h, pr) for h in heads for pr in range(len(pairs))]
    qn, rhs, glast, kdec = [], [], [], []
    kk, qk, decay = {}, {}, {}
    for h in heads:
        lo = h * DK_A
        qh = qkv[:, lo:lo + DK_A]
        kh = qkv[:, A_QK + lo:A_QK + lo + DK_A]
        vh = qkv[:, 2 * A_QK + lo:2 * A_QK + lo + DV_A]
        q_h = qh * (lax.rsqrt(jnp.sum(qh * qh, -1, keepdims=True) + NORM_EPS) * (DK_A ** -0.5))
        k_h = kh * lax.rsqrt(jnp.sum(kh * kh, -1, keepdims=True) + NORM_EPS)
        bcol = beta_all[:, h:h + 1]
        g_h = gcum[:, H_A + h:H_A + h + 1]
        gl_h = gsum[:, H_A + h:H_A + h + 1]
        eg = jnp.exp(g_h)
        kb = k_h * bcol
        k_b = k_h.astype(BF16)
        kb_b = kb.astype(BF16)
        q_b = q_h.astype(BF16)
        g_row = gcum_t[H_A + h:H_A + h + 1, :]
        for pr, rows in enumerate(pairs):
            job = (h, pr)
            kk[job] = _dot_nt(kb_b[rows], k_b[rows])
            qk[job] = _dot_nt(q_b[rows], k_b[rows])
            decay[job] = jnp.exp(jnp.where(causal, g_h[rows] - g_row[:, rows], -jnp.inf))
        rhs.append(jnp.concatenate([vh * bcol, kb * eg], axis=1).astype(BF16))
        qn.append(q_h * eg)
        kdec.append(k_h * jnp.exp(gl_h - g_h))
        glast.append(gl_h)

    p = {job: jnp.where(strict, -(kk[job] * decay[job]), 0.0) for job in jobs}
    inv = {job: eye + p[job] for job in jobs}
    for _ in range(NEUMANN_STEPS):
        pb = {job: p[job].astype(BF16) for job in jobs}
        p = {job: _dot(pb[job], pb[job]) for job in jobs}
        inv = {job: inv[job] + _dot(p[job].astype(BF16), inv[job].astype(BF16)) for job in jobs}
    uw_j = {(h, pr): _dot(inv[(h, pr)].astype(BF16), rhs[h][pairs[pr]]).astype(BF16)
            for h, pr in jobs}
    aw_j = {job: _dot((qk[job] * decay[job]).astype(BF16), uw_j[job]) for job in jobs}
    uw = [jnp.concatenate([uw_j[(h, pr)] for pr in range(len(pairs))], axis=0) for h in heads]
    aw = [jnp.concatenate([aw_j[(h, pr)] for pr in range(len(pairs))], axis=0) for h in heads]
    qt = [(qn[h] - aw[h][:, DV_A:DV_A + DK_A]).astype(BF16) for h in heads]

    bp = [[_dot(kdec[h][c * CHUNK:(c + 1) * CHUNK].T.astype(BF16), uw[h][c * CHUNK:(c + 1) * CHUNK])
           for c in range(GDN_NC)] for h in heads]
    state = [st_ref[h] for h in heads]
    for c in range(GDN_NC):
        r0 = c * CHUNK
        for h in heads:
            lo = h * DV_A
            s_b = state[h].astype(BF16)
            o = _dot(qt[h][r0:r0 + CHUNK], s_b) + aw[h][r0:r0 + CHUNK, 0:DV_A]
            decay_c = jnp.exp(glast[h][r0:r0 + 1])
            state[h] = (state[h] * decay_c + bp[h][c][:, 0:DV_A]
                        - _dot(bp[h][c][:, DV_A:DV_A + DK_A].astype(BF16), s_b))
            z = a_ref[r0:r0 + CHUNK, QKV_A + lo:QKV_A + lo + DV_A]
            on = o * lax.rsqrt(jnp.mean(o * o, -1, keepdims=True) + NORM_EPS) * ng * _silu(z)
            o_ref[r0:r0 + CHUNK, lo:lo + DV_A] = on.astype(BF16)
    for h in heads:
        st_ref[h] = state[h]


def _gdn(a, s, conv_w, alog_row, dtb_row, norm_g, batch, seq):
    n = a.shape[0]
    nblk = seq // GDN_R
    const = lambda r, w: pl.BlockSpec((r, w), lambda b, i: (0, 0))
    return pl.pallas_call(
        _gdn_kernel,
        grid=(batch, nblk),
        in_specs=[
            pl.BlockSpec((GDN_R, A_W), lambda b, i: (b * nblk + i, 0)),
            pl.BlockSpec((GDN_R, SMALL_W), lambda b, i: (b * nblk + i, 0)),
            const(CONV_K, QKV_A), const(1, SMALL_W), const(1, SMALL_W), const(1, DV_A),
        ],
        out_specs=pl.BlockSpec((GDN_R, A_V), lambda b, i: (b * nblk + i, 0)),
        out_shape=jax.ShapeDtypeStruct((n, A_V), BF16),
        scratch_shapes=[pltpu.VMEM((CONV_PAD, QKV_A), F32),
                        pltpu.VMEM((H_A, DK_A, DV_A), F32)],
        compiler_params=_params("parallel", "arbitrary"),
        name="gdn",
    )(a, s, conv_w, alog_row, dtb_row, norm_g)


ATT_R = 256
ATT_CH = ATT_R // CHUNK
B_KBLK = 3
assert (B_KBLK - 1) * ATT_CH == B_PREV
B_STRIP_W = B_KBLK * ATT_R + (ATT_CH - 2) * CHUNK
B_QR = 2 * CHUNK
B_KW = (B_PREV + 2) * CHUNK
B_BIAS0 = (ATT_CH - 2) * CHUNK


def _attn_b_kernel(q_ref, k0_ref, k1_ref, k2_ref, v0_ref, v1_ref, v2_ref, bias_ref, o_ref):
    i = pl.program_id(1)
    k_refs = (k0_ref, k1_ref, k2_ref)
    v_refs = (v0_ref, v1_ref, v2_ref)
    cols = lambda h: slice(h * D_B, (h + 1) * D_B)
    pieces = []
    for half in range(ATT_R // B_QR):
        lo, cut = half * B_QR, []
        while lo < half * B_QR + B_KW:
            hi = min((lo // ATT_R + 1) * ATT_R, half * B_QR + B_KW)
            cut.append((lo // ATT_R, lo % ATT_R, hi - lo))
            lo = hi
        pieces.append(cut)

    def run(near_start):
        jobs = [(h, half) for h in range(H_B) for half in range(ATT_R // B_QR)]
        scores = []
        for h, half in jobs:
            q = q_ref[half * B_QR:(half + 1) * B_QR, cols(h)]
            parts = []
            for j, r0, nr in pieces[half]:
                sj = _dot_nt(q, k_refs[j][r0:r0 + nr, cols(h)])
                if near_start and j < B_KBLK - 1:
                    sj = jnp.where(i + j >= B_KBLK - 1, sj, NEG_BIG)
                parts.append(sj)
            bias = jnp.concatenate([bias_ref[h, 1, :, B_BIAS0:B_BIAS0 + B_KW],
                                    bias_ref[h, 0, :, B_BIAS0:B_BIAS0 + B_KW]], axis=0)
            scores.append(jnp.concatenate(parts, axis=1) + bias)
        probs, denoms = [], []
        for s in scores:
            p = jnp.exp2(s - jnp.max(s, axis=-1, keepdims=True))
            denoms.append(jnp.sum(p, axis=-1, keepdims=True))
            probs.append(p.astype(BF16))
        for (h, half), pb, denom in zip(jobs, probs, denoms):
            o, c0 = None, 0
            for j, r0, nr in pieces[half]:
                term = _dot(pb[:, c0:c0 + nr], v_refs[j][r0:r0 + nr, cols(h)])
                o = term if o is None else o + term
                c0 += nr
            o_ref[half * B_QR:(half + 1) * B_QR, cols(h)] = (o / denom).astype(BF16)

    @pl.when(i >= B_KBLK - 1)
    def _():
        run(False)

    @pl.when(i < B_KBLK - 1)
    def _():
        run(True)


def _attn_b(qkv, bias, batch, seq, layer):
    n = qkv.shape[0]
    nblk = seq // ATT_R

    def kv_spec(j, colblk):
        return pl.BlockSpec(
            (ATT_R, B_W), lambda b, i: (b * nblk + jnp.maximum(i + j - (B_KBLK - 1), 0), colblk))

    return pl.pallas_call(
        _attn_b_kernel,
        grid=(batch, nblk),
        in_specs=[pl.BlockSpec((ATT_R, B_W), lambda b, i: (b * nblk + i, 0))]
        + [kv_spec(j, 1) for j in range(B_KBLK)] + [kv_spec(j, 2) for j in range(B_KBLK)]
        + [pl.BlockSpec((H_B, 2, CHUNK, B_STRIP_W), lambda b, i: (layer, 0, 0, 0))],
        out_specs=pl.BlockSpec((ATT_R, B_W), lambda b, i: (b * nblk + i, 0)),
        out_shape=jax.ShapeDtypeStruct((n, B_W), BF16),
        compiler_params=_params("parallel", "parallel"),
        name="attn_b",
    )(qkv, qkv, qkv, qkv, qkv, qkv, qkv, bias)


def _attn_b_bias(rel_bias):
    nh, nrel = rel_bias.shape
    assert nrel == CHUNK + REL_CLIP
    lead = (ATT_CH - 1) * CHUNK
    vis_end = lead + (B_PREV + 1) * CHUNK
    width = B_KBLK * ATT_R + lead
    period = 1 << (width + CHUNK - 1).bit_length()
    rb = rel_bias.astype(F32)
    far = jnp.broadcast_to(rb[:, -1:], (nh, period))
    f = jnp.concatenate([far[:, :vis_end - nrel], jnp.flip(rb, axis=1), far[:, :period - vis_end]], axis=1)
    g = jnp.concatenate([f[:, :1], jnp.flip(f[:, 1:], axis=1)], axis=1)
    skew = jnp.tile(g, (1, CHUNK + 1))[:, :CHUNK * (period + 1)].reshape(nh, CHUNK, period + 1)
    toep = LOG2E * jnp.flip(skew[:, :, period - width + 1:period + 1], axis=2)
    c = np.arange(width)
    strip = jnp.where(jnp.asarray((c >= lead) & (c < vis_end))[None, None, :], toep, NEG_BIG)
    return jnp.stack([strip[:, :, 0:B_STRIP_W], strip[:, :, CHUNK:CHUNK + B_STRIP_W]], axis=1)


C_KR = 2 * CHUNK
C_KBLK = 3


def _attn_c_kernel(sink_ref, q_ref, k0_ref, k1_ref, k2_ref, v0_ref, v1_ref, v2_ref, tbl_ref, o_ref):
    i = pl.program_id(1)
    k_refs = (k0_ref, k1_ref, k2_ref)
    v_refs = (v0_ref, v1_ref, v2_ref)
    jobs = [(h, half) for h in range(H_C) for half in range(ATT_R // C_KR)]
    kv_cols = lambda h: slice((h // G_C) * D_C, (h // G_C + 1) * D_C)
    scores = []
    for h, half in jobs:
        q = q_ref[half * C_KR:(half + 1) * C_KR, h * D_C:(h + 1) * D_C]
        s_prev = _dot_nt(q, k_refs[half][:, kv_cols(h)])
        if half == 0:
            s_prev = jnp.where(i > 0, s_prev, NEG_BIG)
        s_own = _dot_nt(q, k_refs[half + 1][:, kv_cols(h)])
        scores.append(jnp.concatenate([s_prev, s_own], axis=1) + tbl_ref[h])
    probs, denoms = [], []
    for (h, half), s in zip(jobs, scores):
        sink = sink_ref[h] * LOG2E
        mx = jnp.maximum(jnp.max(s, axis=-1, keepdims=True), sink)
        p = jnp.exp2(s - mx)
        denoms.append(jnp.sum(p, axis=-1, keepdims=True) + jnp.exp2(sink - mx))
        probs.append(p.astype(BF16))
    for (h, half), pb, denom in zip(jobs, probs, denoms):
        o = (_dot(pb[:, 0:C_KR], v_refs[half][:, kv_cols(h)])
             + _dot(pb[:, C_KR:2 * C_KR], v_refs[half + 1][:, kv_cols(h)]))
        o_ref[half * C_KR:(half + 1) * C_KR, h * D_C:(h + 1) * D_C] = (o / denom).astype(BF16)


def _attn_c_table():
    tq = np.arange(C_KR)[:, None] + C_KR
    tk = np.arange(2 * C_KR)[None, :]
    delta = tq // CHUNK - tk // CHUNK
    vis = (delta >= 0) & (delta <= C_PREV)
    slopes = 2.0 ** (-8.0 * np.arange(1, H_C + 1) / H_C)
    alibi = -slopes[:, None, None] * np.abs(tq - tk)[None].astype(np.float64)
    tbl = np.where(vis[None], LOG2E * alibi, NEG_BIG)
    return jnp.asarray(tbl, F32)


def _attn_c(qkv, sinks, batch, seq):
    n = qkv.shape[0]
    nblk = seq // ATT_R
    kper = ATT_R // C_KR
    kcol, vcol = C_Q // C_KV, C_Q // C_KV + 1

    def kv_spec(j, colblk):
        return pl.BlockSpec(
            (C_KR, C_KV),
            lambda b, i: (b * nblk * kper + jnp.maximum(i * kper + j - 1, 0), colblk))

    return pl.pallas_call(
        _attn_c_kernel,
        grid=(batch, nblk),
        in_specs=[pl.BlockSpec(memory_space=pltpu.SMEM),
                  pl.BlockSpec((ATT_R, C_Q), lambda b, i: (b * nblk + i, 0))]
        + [kv_spec(j, kcol) for j in range(C_KBLK)] + [kv_spec(j, vcol) for j in range(C_KBLK)]
        + [pl.BlockSpec((H_C, C_KR, 2 * C_KR), lambda b, i: (0, 0, 0))],
        out_specs=pl.BlockSpec((ATT_R, C_Q), lambda b, i: (b * nblk + i, 0)),
        out_shape=jax.ShapeDtypeStruct((n, C_Q), BF16),
        compiler_params=_params("parallel", "parallel"),
        name="attn_c",
    )(sinks, qkv, qkv, qkv, qkv, qkv, qkv, qkv, _attn_c_table())


MERGE_TM = 512
MERGE_SUB = 256


MERGE_WR = 512


def _merge_kernel(layer, x_ref, oa_ref, ob_ref, oc_ref, wg_hbm, bg_ref, wa_hbm, wb_hbm, wc_hbm, wo_hbm,
                  g_ref, b_ref, o_ref, wg_ref, wa_ref, wb_ref, wc_ref, wo_ref, stage, sem):
    @pl.when(pl.program_id(0) == 0)
    def _():
        r = MERGE_WR
        pairs = [(wg_hbm.at[layer, pl.ds(i * r, r), pl.ds(j * D_MODEL, D_MODEL)],
                  wg_ref.at[pl.ds(i * r, r), pl.ds(j * D_MODEL, D_MODEL)])
                 for i in range(D_MODEL // r) for j in range(N_BRANCH)]
        pairs += [(src.at[layer], dst) for src, dst in
                  ((wa_hbm, wa_ref), (wb_hbm, wb_ref), (wc_hbm, wc_ref))]
        pairs += [(wo_hbm.at[layer, pl.ds(i * r, r), :], wo_ref.at[pl.ds(i * r, r), :])
                  for i in range(D_MODEL // r)]
        _stage_cast(pairs, stage, sem)

    for s in range(MERGE_TM // MERGE_SUB):
        rows = slice(s * MERGE_SUB, (s + 1) * MERGE_SUB)
        x = x_ref[rows, :]
        xb = x.astype(BF16)
        merged = None
        for r, (br_ref, w_ref) in enumerate(((oa_ref, wa_ref), (ob_ref, wb_ref), (oc_ref, wc_ref))):
            lo = r * D_MODEL
            gate = jax.nn.sigmoid(_dot(xb, wg_ref[:, lo:lo + D_MODEL]) + bg_ref[:, lo:lo + D_MODEL])
            term = gate * _dot(br_ref[rows, :], w_ref[...])
            merged = term if merged is None else merged + term
        y = ALPHA * x + _dot(merged.astype(BF16), wo_ref[...])
        o_ref[rows, :] = _layer_norm(y, g_ref[...], b_ref[...])


def _merge(x, oa, ob, oc, w_gate, b_gate, w_a, w_b, w_c, w_out, g, b, layer):
    n = x.shape[0]
    assert A_V == B_W == C_Q == MERGE_WR
    row = lambda w: pl.BlockSpec((MERGE_TM, w), lambda i: (i, 0))
    const = lambda r, w: pl.BlockSpec((r, w), lambda i: (0, 0))
    hbm = pl.BlockSpec(memory_space=pl.ANY)
    return pl.pallas_call(
        functools.partial(_merge_kernel, layer),
        grid=(n // MERGE_TM,),
        in_specs=[row(D_MODEL), row(A_V), row(B_W), row(C_Q),
                  hbm, const(1, N_BRANCH * D_MODEL), hbm, hbm, hbm, hbm,
                  const(1, D_MODEL), const(1, D_MODEL)],
        out_specs=row(D_MODEL),
        out_shape=jax.ShapeDtypeStruct((n, D_MODEL), F32),
        scratch_shapes=[pltpu.VMEM((D_MODEL, N_BRANCH * D_MODEL), BF16),
                        pltpu.VMEM((A_V, D_MODEL), BF16), pltpu.VMEM((B_W, D_MODEL), BF16),
                        pltpu.VMEM((C_Q, D_MODEL), BF16), pltpu.VMEM((D_MODEL, D_MODEL), BF16),
                        pltpu.VMEM((2, MERGE_WR, D_MODEL), F32), pltpu.SemaphoreType.DMA((2,))],
        compiler_params=_params("arbitrary"),
        name="merge_ln",
    )(x, oa, ob, oc, w_gate, b_gate, w_a, w_b, w_c, w_out, g, b)


def _lane_row(vals, offset):
    return jnp.zeros((1, LANES), F32).at[0, offset:offset + vals.shape[0]].set(vals.astype(F32))


def kernel(x, ln1_g, ln1_b, w_ff1_in, w_ff1_out, w_in, b_in, conv_w, a_log, dt_bias, gdn_norm_g,
           rel_bias, sinks, w_gate, b_gate, w_br_a, w_br_b, w_br_c, w_out, ln2_g, ln2_b,
           w_ff2_in, w_ff2_out, ln3_g, ln3_b):
    batch, seq, d = x.shape
    assert d == D_MODEL and seq % ATT_R == 0 and (batch * seq) % FFN_TM == 0
    h = x.reshape(batch * seq, d)
    row = lambda v: v.reshape(1, -1).astype(F32)
    o_small = QKV_A
    o_z = o_small + 2 * H_A
    o_b = o_z + A_V
    o_c = o_b + 3 * B_W
    bias_strips = _attn_b_bias(rel_bias.reshape(DEPTH * H_B, CHUNK + REL_CLIP))
    for l in range(DEPTH):
        h = _ffn(h, w_ff1_in, w_ff1_out, row(ln1_g[l]), row(ln1_b[l]), l)
        wi, bi = w_in[l], b_in[l]
        pad = SMALL_W - 2 * H_A
        wa = jnp.concatenate([wi[:, 0:QKV_A], wi[:, o_z:o_b]], axis=1).astype(BF16)
        ws = jnp.pad(wi[:, o_small:o_z], ((0, 0), (0, pad))).astype(BF16)
        ba = jnp.concatenate([bi[0:QKV_A], bi[o_z:o_b]])
        bs = jnp.pad(bi[o_small:o_z], (0, pad))
        pa, ps, pb, pc = _proj(h, wa, ws, wi[:, o_b:o_c].astype(BF16), wi[:, o_c:].astype(BF16),
                               row(ba), row(bs), row(bi[o_b:o_c]), row(bi[o_c:]))
        o_a = _gdn(pa, ps, conv_w[l].astype(F32), _lane_row(a_log[l], H_A), _lane_row(dt_bias[l], H_A),
                   row(gdn_norm_g[l]), batch, seq)
        o_bb = _attn_b(pb, bias_strips, batch, seq, l)
        o_cc = _attn_c(pc, sinks[l].astype(F32), batch, seq)
        h = _merge(h, o_a, o_bb, o_cc, w_gate, row(b_gate[l]), w_br_a, w_br_b, w_br_c, w_out,
                   row(ln2_g[l]), row(ln2_b[l]), l)
        h = _ffn(h, w_ff2_in, w_ff2_out, row(ln3_g[l]), row(ln3_b[l]), l)
    return h.reshape(batch, seq, d)
```

```python
import functools
import math

import jax
import jax.numpy as jnp
import numpy as np
from jax import lax
from jax.experimental import pallas as pl
from jax.experimental.pallas import tpu as pltpu

F32 = jnp.float32
BF16 = jnp.bfloat16

D_MODEL = 1024
DEPTH = 2
CHUNK = 64
D_FF = 4096
LN_EPS = 1e-5
NORM_EPS = 1e-6
H_A, DK_A, DV_A, CONV_K = 4, 128, 128, 4
A_QK = H_A * DK_A
A_V = H_A * DV_A
H_B, D_B, B_PREV, REL_CLIP = 4, 128, 8, 128
B_W = H_B * D_B
H_C, HKV_C, D_C, WINDOW = 8, 2, 64, 128
G_C = H_C // HKV_C
C_PREV = WINDOW // CHUNK
C_Q = H_C * D_C
C_KV = HKV_C * D_C
N_BRANCH = 3
ALPHA = (2.0 * DEPTH) ** 0.25

LANES = 128
VMEM_LIMIT = 56 * 1024 * 1024
NEG_BIG = -1e30
LOG2E = math.log2(math.e)

SMALL_W = LANES
QKV_A = 2 * A_QK + A_V
A_W = QKV_A + A_V
C_W = C_Q + 2 * C_KV


def _dot(a, b):
    return jnp.dot(a, b, preferred_element_type=F32)


def _dot_nt(a, b):
    return lax.dot_general(a, b, (((1,), (1,)), ((), ())), preferred_element_type=F32)


def _layer_norm(y, g, b):
    mu = jnp.mean(y, axis=-1, keepdims=True)
    yc = y - mu
    var = jnp.mean(yc * yc, axis=-1, keepdims=True)
    return yc * lax.rsqrt(var + LN_EPS) * g + b


def _silu_of_half(xh):
    return xh + xh * jnp.tanh(xh)


def _silu(x):
    return _silu_of_half(0.5 * x)


def _params(*sem):
    return pltpu.CompilerParams(dimension_semantics=sem, vmem_limit_bytes=VMEM_LIMIT)


FFN_TM = 512
FFN_SUB = 256
FFN_TF = 1024


FFN_WCH = 1024


def _stage_cast(pairs, stage, sem):
    copies = [pltpu.make_async_copy(src, stage.at[c % 2], sem.at[c % 2])
              for c, (src, _) in enumerate(pairs)]
    copies[0].start()
    for c, (_, dst) in enumerate(pairs):
        if c + 1 < len(pairs):
            copies[c + 1].start()
        copies[c].wait()
        dst[...] = stage[c % 2].astype(BF16)


def _ffn_kernel(layer, x_ref, wi_hbm, wo_hbm, g_ref, b_ref, o_ref, wi_ref, wo_ref, h_ref, stage, sem):
    @pl.when(pl.program_id(0) == 0)
    def _():
        w = FFN_WCH
        pairs = [(wi_hbm.at[layer, :, pl.ds(c * w, w)], wi_ref.at[:, pl.ds(c * w, w)])
                 for c in range(2 * D_FF // w)]
        pairs += [(wo_hbm.at[layer, pl.ds(c * w, w), :], wo_ref.at[pl.ds(c * w, w), :])
                  for c in range(D_FF // w)]
        _stage_cast(pairs, stage, sem)

    for s in range(FFN_TM // FFN_SUB):
        rows = slice(s * FFN_SUB, (s + 1) * FFN_SUB)
        x = x_ref[rows, :]
        xb = x.astype(BF16)
        for f in range(D_FF // FFN_TF):
            gate = _dot(xb, wi_ref[:, f * FFN_TF:(f + 1) * FFN_TF])
            up = _dot(xb, wi_ref[:, D_FF + f * FFN_TF:D_FF + (f + 1) * FFN_TF])
            h_ref[rows, f * FFN_TF:(f + 1) * FFN_TF] = (_silu(gate) * up).astype(BF16)
        y = ALPHA * x + 0.5 * _dot(h_ref[rows, :], wo_ref[...])
        o_ref[rows, :] = _layer_norm(y, g_ref[...], b_ref[...])


def _ffn(x, w_in, w_out, g, b, layer):
    n = x.shape[0]
    resident = lambda r, w: pl.BlockSpec((r, w), lambda i: (0, 0), pipeline_mode=pl.Buffered(1))
    return pl.pallas_call(
        functools.partial(_ffn_kernel, layer),
        grid=(n // FFN_TM,),
        in_specs=[
            pl.BlockSpec((FFN_TM, D_MODEL), lambda i: (i, 0)),
            pl.BlockSpec(memory_space=pl.ANY), pl.BlockSpec(memory_space=pl.ANY),
            resident(1, D_MODEL), resident(1, D_MODEL),
        ],
        out_specs=pl.BlockSpec((FFN_TM, D_MODEL), lambda i: (i, 0)),
        out_shape=jax.ShapeDtypeStruct((n, D_MODEL), F32),
        scratch_shapes=[pltpu.VMEM((D_MODEL, 2 * D_FF), BF16), pltpu.VMEM((D_FF, D_MODEL), BF16),
                        pltpu.VMEM((FFN_TM, D_FF), BF16),
                        pltpu.VMEM((2, FFN_WCH, FFN_WCH), F32), pltpu.SemaphoreType.DMA((2,))],
        compiler_params=_params("arbitrary"),
        name="ffn_ln",
    )(x, w_in, w_out, g, b)


PROJ_TM = 512


def _proj_kernel(x_ref, wa_ref, ws_ref, wb_ref, wc_ref, ba_ref, bs_ref, bb_ref, bc_ref,
                 oa_ref, os_ref, ob_ref, oc_ref):
    xb = x_ref[...].astype(BF16)
    oa_ref[...] = _dot(xb, wa_ref[...]) + ba_ref[...]
    os_ref[...] = _dot(xb, ws_ref[...]) + bs_ref[...]
    yb = _dot(xb, wb_ref[...]) + bb_ref[...]
    ob_ref[:, 0:B_W] = (yb[:, 0:B_W] * (D_B ** -0.5 * LOG2E)).astype(BF16)
    ob_ref[:, B_W:3 * B_W] = yb[:, B_W:3 * B_W].astype(BF16)
    yc = _dot(xb, wc_ref[...]) + bc_ref[...]
    oc_ref[:, 0:C_Q] = (yc[:, 0:C_Q] * (D_C ** -0.5 * LOG2E)).astype(BF16)
    oc_ref[:, C_Q:C_W] = yc[:, C_Q:C_W].astype(BF16)


def _proj(x, wa, ws, wb, wc, ba, bs, bb, bc):
    n = x.shape[0]
    widths = (A_W, SMALL_W, 3 * B_W, C_W)
    row = lambda w: pl.BlockSpec((PROJ_TM, w), lambda i: (i, 0))
    const = lambda r, w: pl.BlockSpec((r, w), lambda i: (0, 0))
    return pl.pallas_call(
        _proj_kernel,
        grid=(n // PROJ_TM,),
        in_specs=[row(D_MODEL)] + [const(D_MODEL, w) for w in widths] + [const(1, w) for w in widths],
        out_specs=[row(w) for w in widths],
        out_shape=[jax.ShapeDtypeStruct((n, A_W), F32), jax.ShapeDtypeStruct((n, SMALL_W), F32),
                   jax.ShapeDtypeStruct((n, 3 * B_W), BF16), jax.ShapeDtypeStruct((n, C_W), BF16)],
        compiler_params=_params("parallel"),
        name="in_proj",
    )(x, wa, ws, wb, wc, ba, bs, bb, bc)


GDN_R = 256
GDN_NC = GDN_R // CHUNK
GDN_PAIR = 2 * CHUNK
CONV_PAD = 8
LOG2_CHUNK = CHUNK.bit_length() - 1
NEUMANN_STEPS = LOG2_CHUNK - 1


def _gdn_kernel(a_ref, s_ref, cw_ref, alog_ref, dtb_ref, ng_ref, o_ref, xbuf, st_ref):
    i = pl.program_id(1)

    @pl.when(i == 0)
    def _():
        st_ref[...] = jnp.zeros_like(st_ref)
        xbuf[...] = jnp.zeros_like(xbuf)

    assert CONV_K == 4
    x_new = a_ref[:, 0:QKV_A]
    n_grp = GDN_R // CONV_PAD
    x3 = x_new.reshape(n_grp, CONV_PAD, QKV_A)
    hist = xbuf[...].reshape(1, CONV_PAD, QKV_A)
    sub = lax.broadcasted_iota(jnp.int32, (n_grp, CONV_PAD, QKV_A), 1)
    w0, w1, w2, w3 = [0.5 * cw_ref[j:j + 1, :].reshape(1, 1, QKV_A) for j in range(CONV_K)]

    def delay(v, v_hist, s):
        rot = pltpu.roll(v, s, axis=1)
        prev = jnp.concatenate([pltpu.roll(v_hist, s, axis=1), rot[0:n_grp - 1]], axis=0)
        return jnp.where(sub >= s, rot, prev)

    dx = delay(x3, hist, 1)
    inner = w1 * x3 + w0 * dx
    inner_hist = w1 * hist + w0 * pltpu.roll(hist, 1, axis=1)
    y_half = w3 * x3 + w2 * dx + delay(inner, inner_hist, 2)
    xbuf[...] = x_new[GDN_R - CONV_PAD:GDN_R]
    qkv = _silu_of_half(y_half.reshape(GDN_R, QKV_A))

    s = s_ref[...]
    beta_all = jax.nn.sigmoid(s)
    g_all = -jnp.exp(alog_ref[...]) * jax.nn.softplus(s + dtb_ref[...])
    row = lax.broadcasted_iota(jnp.int32, (GDN_PAIR, GDN_PAIR), 0)
    col = lax.broadcasted_iota(jnp.int32, (GDN_PAIR, GDN_PAIR), 1)
    same_chunk = jnp.right_shift(row, LOG2_CHUNK) == jnp.right_shift(col, LOG2_CHUNK)
    causal = same_chunk & (col <= row)
    strict = same_chunk & (col < row)
    eye = jnp.where(row == col, 1.0, 0.0).astype(F32)
    pairs = [slice(p * GDN_PAIR, (p + 1) * GDN_PAIR) for p in range(GDN_R // GDN_PAIR)]
    g_hi = g_all.astype(BF16)
    g_r1 = g_all - g_hi.astype(F32)
    g_mid = g_r1.astype(BF16)
    g_lo = (g_r1 - g_mid.astype(F32)).astype(BF16)
    g_split = jnp.concatenate([g_hi, g_mid, g_lo], axis=1)
    tri = jnp.where(causal, 1.0, 0.0).astype(BF16)
    g_parts = jnp.concatenate([_dot(tri, g_split[rows]) for rows in pairs], axis=0)
    gcum = g_parts[:, 0:LANES] + g_parts[:, LANES:2 * LANES] + g_parts[:, 2 * LANES:3 * LANES]
    gsum = jnp.concatenate(
        [jnp.broadcast_to(gcum[(c + 1) * CHUNK - 1:(c + 1) * CHUNK, :], (CHUNK, LANES))
         for c in range(GDN_NC)], axis=0)
    gcum_t = gcum.T
    ng = ng_ref[...]
    heads = range(H_A)

    jobs = [(h, pr) for h in heads for pr in range(len(pairs))]
    qn, rhs, glast, kdec = [], [], [], []
    kk, qk, decay = {}, {}, {}
    for h in heads:
        lo = h * DK_A
        qh = qkv[:, lo:lo + DK_A]
        kh = qkv[:, A_QK + lo:A_QK + lo + DK_A]
        vh = qkv[:, 2 * A_QK + lo:2 * A_QK + lo + DV_A]
        q_h = qh * (lax.rsqrt(jnp.sum(qh * qh, -1, keepdims=True) + NORM_EPS) * (DK_A ** -0.5))
        k_h = kh * lax.rsqrt(jnp.sum(kh * kh, -1, keepdims=True) + NORM_EPS)
        bcol = beta_all[:, h:h + 1]
        g_h = gcum[:, H_A + h:H_A + h + 1]
        gl_h = gsum[:, H_A + h:H_A + h + 1]
        eg = jnp.exp(g_h)
        kb = k_h * bcol
        k_b = k_h.astype(BF16)
        kb_b = kb.astype(BF16)
        q_b = q_h.astype(BF16)
        g_row = gcum_t[H_A + h:H_A + h + 1, :]
        for pr, rows in enumerate(pairs):
            job = (h, pr)
            kk[job] = _dot_nt(kb_b[rows], k_b[rows])
            qk[job] = _dot_nt(q_b[rows], k_b[rows])
            decay[job] = jnp.exp(jnp.where(causal, g_h[rows] - g_row[:, rows], -jnp.inf))
        rhs.append(jnp.concatenate([vh * bcol, kb * eg], axis=1).astype(BF16))
        qn.append(q_h * eg)
        kdec.append(k_h * jnp.exp(gl_h - g_h))
        glast.append(gl_h)

    p = {job: jnp.where(strict, -(kk[job] * decay[job]), 0.0) for job in jobs}
    inv = {job: eye + p[job] for job in jobs}
    for _ in range(NEUMANN_STEPS):
        pb = {job: p[job].astype(BF16) for job in jobs}
        p = {job: _dot(pb[job], pb[job]) for job in jobs}
        inv = {job: inv[job] + _dot(p[job].astype(BF16), inv[job].astype(BF16)) for job in jobs}
    uw_j = {(h, pr): _dot(inv[(h, pr)].astype(BF16), rhs[h][pairs[pr]]).astype(BF16)
            for h, pr in jobs}
    aw_j = {job: _dot((qk[job] * decay[job]).astype(BF16), uw_j[job]) for job in jobs}
    uw = [jnp.concatenate([uw_j[(h, pr)] for pr in range(len(pairs))], axis=0) for h in heads]
    aw = [jnp.concatenate([aw_j[(h, pr)] for pr in range(len(pairs))], axis=0) for h in heads]
    qt = [(qn[h] - aw[h][:, DV_A:DV_A + DK_A]).astype(BF16) for h in heads]

    bp = [[_dot(kdec[h][c * CHUNK:(c + 1) * CHUNK].T.astype(BF16), uw[h][c * CHUNK:(c + 1) * CHUNK])
           for c in range(GDN_NC)] for h in heads]
    state = [st_ref[h] for h in heads]
    for c in range(GDN_NC):
        r0 = c * CHUNK
        for h in heads:
            lo = h * DV_A
            s_b = state[h].astype(BF16)
            o = _dot(qt[h][r0:r0 + CHUNK], s_b) + aw[h][r0:r0 + CHUNK, 0:DV_A]
            decay_c = jnp.exp(glast[h][r0:r0 + 1])
            state[h] = (state[h] * decay_c + bp[h][c][:, 0:DV_A]
                        - _dot(bp[h][c][:, DV_A:DV_A + DK_A].astype(BF16), s_b))
            z = a_ref[r0:r0 + CHUNK, QKV_A + lo:QKV_A + lo + DV_A]
            on = o * lax.rsqrt(jnp.mean(o * o, -1, keepdims=True) + NORM_EPS) * ng * _silu(z)
            o_ref[r0:r0 + CHUNK, lo:lo + DV_A] = on.astype(BF16)
    for h in heads:
        st_ref[h] = state[h]


def _gdn(a, s, conv_w, alog_row, dtb_row, norm_g, batch, seq):
    n = a.shape[0]
    nblk = seq // GDN_R
    const = lambda r, w: pl.BlockSpec((r, w), lambda b, i: (0, 0))
    return pl.pallas_call(
        _gdn_kernel,
        grid=(batch, nblk),
        in_specs=[
            pl.BlockSpec((GDN_R, A_W), lambda b, i: (b * nblk + i, 0)),
            pl.BlockSpec((GDN_R, SMALL_W), lambda b, i: (b * nblk + i, 0)),
            const(CONV_K, QKV_A), const(1, SMALL_W), const(1, SMALL_W), const(1, DV_A),
        ],
        out_specs=pl.BlockSpec((GDN_R, A_V), lambda b, i: (b * nblk + i, 0)),
        out_shape=jax.ShapeDtypeStruct((n, A_V), BF16),
        scratch_shapes=[pltpu.VMEM((CONV_PAD, QKV_A), F32),
                        pltpu.VMEM((H_A, DK_A, DV_A), F32)],
        compiler_params=_params("parallel", "arbitrary"),
        name="gdn",
    )(a, s, conv_w, alog_row, dtb_row, norm_g)


ATT_R = 256
ATT_CH = ATT_R // CHUNK
B_KBLK = 3
assert (B_KBLK - 1) * ATT_CH == B_PREV
B_STRIP_W = B_KBLK * ATT_R + (ATT_CH - 2) * CHUNK
B_QR = 2 * CHUNK
B_KW = (B_PREV + 2) * CHUNK
B_BIAS0 = (ATT_CH - 2) * CHUNK


def _attn_b_pieces():
    pieces = []
    for half in range(ATT_R // B_QR):
        lo, cut = half * B_QR, []
        while lo < half * B_QR + B_KW:
            hi = min((lo // ATT_R + 1) * ATT_R, half * B_QR + B_KW)
            cut.append((lo // ATT_R, lo % ATT_R, hi - lo))
            lo = hi
        pieces.append(cut)
    return pieces


B_JOBS = [(h, half) for h in range(H_B) for half in range(ATT_R // B_QR)]


def _attn_b_scores(i, q_ref, k_refs, bias_ref):
    pieces = _attn_b_pieces()
    scores = []
    for h, half in B_JOBS:
        cols = slice(h * D_B, (h + 1) * D_B)
        q = q_ref[half * B_QR:(half + 1) * B_QR, cols]
        parts = []
        for j, r0, nr in pieces[half]:
            sj = _dot_nt(q, k_refs[j][r0:r0 + nr, cols])
            if j < B_KBLK - 1:
                sj = jnp.where(i + j >= B_KBLK - 1, sj, NEG_BIG)
            parts.append(sj)
        bias = jnp.concatenate([bias_ref[h, 1, :, B_BIAS0:B_BIAS0 + B_KW],
                                bias_ref[h, 0, :, B_BIAS0:B_BIAS0 + B_KW]], axis=0)
        scores.append(jnp.concatenate(parts, axis=1) + bias)
    return scores


def _attn_b_finish(scores, v_refs, o_ref):
    pieces = _attn_b_pieces()
    probs, denoms = [], []
    for s in scores:
        p = jnp.exp2(s - jnp.max(s, axis=-1, keepdims=True))
        denoms.append(jnp.sum(p, axis=-1, keepdims=True))
        probs.append(p.astype(BF16))
    for (h, half), pb, denom in zip(B_JOBS, probs, denoms):
        cols = slice(h * D_B, (h + 1) * D_B)
        o, c0 = None, 0
        for j, r0, nr in pieces[half]:
            term = _dot(pb[:, c0:c0 + nr], v_refs[j][r0:r0 + nr, cols])
            o = term if o is None else o + term
            c0 += nr
        o_ref[half * B_QR:(half + 1) * B_QR, cols] = (o / denom).astype(BF16)


def _attn_b_bias(rel_bias):
    nh, nrel = rel_bias.shape
    assert nrel == CHUNK + REL_CLIP
    lead = (ATT_CH - 1) * CHUNK
    vis_end = lead + (B_PREV + 1) * CHUNK
    width = B_KBLK * ATT_R + lead
    period = 1 << (width + CHUNK - 1).bit_length()
    rb = rel_bias.astype(F32)
    far = jnp.broadcast_to(rb[:, -1:], (nh, period))
    f = jnp.concatenate([far[:, :vis_end - nrel], jnp.flip(rb, axis=1), far[:, :period - vis_end]], axis=1)
    g = jnp.concatenate([f[:, :1], jnp.flip(f[:, 1:], axis=1)], axis=1)
    skew = jnp.tile(g, (1, CHUNK + 1))[:, :CHUNK * (period + 1)].reshape(nh, CHUNK, period + 1)
    toep = LOG2E * jnp.flip(skew[:, :, period - width + 1:period + 1], axis=2)
    c = np.arange(width)
    strip = jnp.where(jnp.asarray((c >= lead) & (c < vis_end))[None, None, :], toep, NEG_BIG)
    return jnp.stack([strip[:, :, 0:B_STRIP_W], strip[:, :, CHUNK:CHUNK + B_STRIP_W]], axis=1)


C_KR = 2 * CHUNK
C_KBLK = 3


C_JOBS = [(h, half) for h in range(H_C) for half in range(ATT_R // C_KR)]


def _kv_cols(h):
    return slice((h // G_C) * D_C, (h // G_C + 1) * D_C)


def _attn_c_scores(i, q_ref, k_refs, tbl_ref):
    scores = []
    for h, half in C_JOBS:
        q = q_ref[half * C_KR:(half + 1) * C_KR, h * D_C:(h + 1) * D_C]
        s_prev = _dot_nt(q, k_refs[half][:, _kv_cols(h)])
        if half == 0:
            s_prev = jnp.where(i > 0, s_prev, NEG_BIG)
        s_own = _dot_nt(q, k_refs[half + 1][:, _kv_cols(h)])
        scores.append(jnp.concatenate([s_prev, s_own], axis=1) + tbl_ref[h])
    return scores


def _attn_c_finish(scores, sink_ref, v_refs, o_ref):
    probs, denoms = [], []
    for (h, half), s in zip(C_JOBS, scores):
        sink = sink_ref[h] * LOG2E
        mx = jnp.maximum(jnp.max(s, axis=-1, keepdims=True), sink)
        p = jnp.exp2(s - mx)
        denoms.append(jnp.sum(p, axis=-1, keepdims=True) + jnp.exp2(sink - mx))
        probs.append(p.astype(BF16))
    for (h, half), pb, denom in zip(C_JOBS, probs, denoms):
        o = (_dot(pb[:, 0:C_KR], v_refs[half][:, _kv_cols(h)])
             + _dot(pb[:, C_KR:2 * C_KR], v_refs[half + 1][:, _kv_cols(h)]))
        o_ref[half * C_KR:(half + 1) * C_KR, h * D_C:(h + 1) * D_C] = (o / denom).astype(BF16)


def _attn_c_table():
    tq = np.arange(C_KR)[:, None] + C_KR
    tk = np.arange(2 * C_KR)[None, :]
    delta = tq // CHUNK - tk // CHUNK
    vis = (delta >= 0) & (delta <= C_PREV)
    slopes = 2.0 ** (-8.0 * np.arange(1, H_C + 1) / H_C)
    alibi = -slopes[:, None, None] * np.abs(tq - tk)[None].astype(np.float64)
    tbl = np.where(vis[None], LOG2E * alibi, NEG_BIG)
    return jnp.asarray(tbl, F32)


MERGE_WR = 512


def _mix_merge_kernel(layer, sink_ref, x_ref, oa_ref,
                      qb_ref, kb0_ref, kb1_ref, kb2_ref, vb0_ref, vb1_ref, vb2_ref, bias_ref,
                      qc_ref, kc0_ref, kc1_ref, kc2_ref, vc0_ref, vc1_ref, vc2_ref, tbl_ref,
                      wg_hbm, bg_ref, wa_hbm, wb_hbm, wc_hbm, wo_hbm, g_ref, b_ref,
                      o_ref,
                      wg_ref, wa_ref, wb_ref, wc_ref, wo_ref, stage, sem, ob_ref, oc_ref):
    i = pl.program_id(1)

    @pl.when((pl.program_id(0) == 0) & (i == 0))
    def _():
        r = MERGE_WR
        pairs = [(wg_hbm.at[layer, pl.ds(m * r, r), pl.ds(j * D_MODEL, D_MODEL)],
                  wg_ref.at[pl.ds(m * r, r), pl.ds(j * D_MODEL, D_MODEL)])
                 for m in range(D_MODEL // r) for j in range(N_BRANCH)]
        pairs += [(src.at[layer], dst) for src, dst in
                  ((wa_hbm, wa_ref), (wb_hbm, wb_ref), (wc_hbm, wc_ref))]
        pairs += [(wo_hbm.at[layer, pl.ds(m * r, r), :], wo_ref.at[pl.ds(m * r, r), :])
                  for m in range(D_MODEL // r)]
        _stage_cast(pairs, stage, sem)

    scores_b = _attn_b_scores(i, qb_ref, (kb0_ref, kb1_ref, kb2_ref), bias_ref)
    scores_c = _attn_c_scores(i, qc_ref, (kc0_ref, kc1_ref, kc2_ref), tbl_ref)
    x = x_ref[...]
    xb = x.astype(BF16)
    gates = [jax.nn.sigmoid(_dot(xb, wg_ref[:, r * D_MODEL:(r + 1) * D_MODEL])
                            + bg_ref[:, r * D_MODEL:(r + 1) * D_MODEL]) for r in range(N_BRANCH)]
    merged = gates[0] * _dot(oa_ref[...], wa_ref[...])
    _attn_b_finish(scores_b, (vb0_ref, vb1_ref, vb2_ref), ob_ref)
    _attn_c_finish(scores_c, sink_ref, (vc0_ref, vc1_ref, vc2_ref), oc_ref)
    merged = merged + gates[1] * _dot(ob_ref[...], wb_ref[...])
    merged = merged + gates[2] * _dot(oc_ref[...], wc_ref[...])
    y = ALPHA * x + _dot(merged.astype(BF16), wo_ref[...])
    o_ref[...] = _layer_norm(y, g_ref[...], b_ref[...])


def _mix_merge(x, oa, qkv_b, qkv_c, bias, sinks, w_gate, b_gate, w_a, w_b, w_c, w_out, g, b,
               batch, seq, layer):
    n = x.shape[0]
    assert A_V == B_W == C_Q == MERGE_WR
    nblk = seq // ATT_R
    kper = ATT_R // C_KR
    kcol, vcol = C_Q // C_KV, C_Q // C_KV + 1
    row = lambda w: pl.BlockSpec((ATT_R, w), lambda bb, i: (bb * nblk + i, 0))
    const = lambda *shape: pl.BlockSpec(shape, lambda bb, i: (0,) * len(shape))
    hbm = pl.BlockSpec(memory_space=pl.ANY)

    def kv_b(j, colblk):
        return pl.BlockSpec(
            (ATT_R, B_W), lambda bb, i: (bb * nblk + jnp.maximum(i + j - (B_KBLK - 1), 0), colblk))

    def kv_c(j, colblk):
        return pl.BlockSpec(
            (C_KR, C_KV),
            lambda bb, i: (bb * nblk * kper + jnp.maximum(i * kper + j - 1, 0), colblk))

    in_specs = ([pl.BlockSpec(memory_space=pltpu.SMEM), row(D_MODEL), row(A_V), row(B_W)]
                + [kv_b(j, 1) for j in range(B_KBLK)] + [kv_b(j, 2) for j in range(B_KBLK)]
                + [pl.BlockSpec((H_B, 2, CHUNK, B_STRIP_W), lambda bb, i: (layer, 0, 0, 0)), row(C_Q)]
                + [kv_c(j, kcol) for j in range(C_KBLK)] + [kv_c(j, vcol) for j in range(C_KBLK)]
                + [const(H_C, C_KR, 2 * C_KR),
                   hbm, const(1, N_BRANCH * D_MODEL), hbm, hbm, hbm, hbm,
                   const(1, D_MODEL), const(1, D_MODEL)])
    return pl.pallas_call(
        functools.partial(_mix_merge_kernel, layer),
        grid=(batch, nblk),
        in_specs=in_specs,
        out_specs=row(D_MODEL),
        out_shape=jax.ShapeDtypeStruct((n, D_MODEL), F32),
        scratch_shapes=[pltpu.VMEM((D_MODEL, N_BRANCH * D_MODEL), BF16),
                        pltpu.VMEM((A_V, D_MODEL), BF16), pltpu.VMEM((B_W, D_MODEL), BF16),
                        pltpu.VMEM((C_Q, D_MODEL), BF16), pltpu.VMEM((D_MODEL, D_MODEL), BF16),
                        pltpu.VMEM((2, MERGE_WR, D_MODEL), F32), pltpu.SemaphoreType.DMA((2,)),
                        pltpu.VMEM((ATT_R, B_W), BF16), pltpu.VMEM((ATT_R, C_Q), BF16)],
        compiler_params=_params("arbitrary", "arbitrary"),
        name="mix_merge",
    )(sinks, x, oa, qkv_b, qkv_b, qkv_b, qkv_b, qkv_b, qkv_b, qkv_b, bias,
      qkv_c, qkv_c, qkv_c, qkv_c, qkv_c, qkv_c, qkv_c, _attn_c_table(),
      w_gate, b_gate, w_a, w_b, w_c, w_out, g, b)


def _lane_row(vals, offset):
    return jnp.zeros((1, LANES), F32).at[0, offset:offset + vals.shape[0]].set(vals.astype(F32))


def kernel(x, ln1_g, ln1_b, w_ff1_in, w_ff1_out, w_in, b_in, conv_w, a_log, dt_bias, gdn_norm_g,
           rel_bias, sinks, w_gate, b_gate, w_br_a, w_br_b, w_br_c, w_out, ln2_g, ln2_b,
           w_ff2_in, w_ff2_out, ln3_g, ln3_b):
    batch, seq, d = x.shape
    assert d == D_MODEL and seq % ATT_R == 0 and (batch * seq) % FFN_TM == 0
    h = x.reshape(batch * seq, d)
    row = lambda v: v.reshape(1, -1).astype(F32)
    o_small = QKV_A
    o_z = o_small + 2 * H_A
    o_b = o_z + A_V
    o_c = o_b + 3 * B_W
    bias_strips = _attn_b_bias(rel_bias.reshape(DEPTH * H_B, CHUNK + REL_CLIP))
    for l in range(DEPTH):
        h = _ffn(h, w_ff1_in, w_ff1_out, row(ln1_g[l]), row(ln1_b[l]), l)
        wi, bi = w_in[l], b_in[l]
        pad = SMALL_W - 2 * H_A
        wa = jnp.concatenate([wi[:, 0:QKV_A], wi[:, o_z:o_b]], axis=1).astype(BF16)
        ws = jnp.pad(wi[:, o_small:o_z], ((0, 0), (0, pad))).astype(BF16)
        ba = jnp.concatenate([bi[0:QKV_A], bi[o_z:o_b]])
        bs = jnp.pad(bi[o_small:o_z], (0, pad))
        pa, ps, pb, pc = _proj(h, wa, ws, wi[:, o_b:o_c].astype(BF16), wi[:, o_c:].astype(BF16),
                               row(ba), row(bs), row(bi[o_b:o_c]), row(bi[o_c:]))
        o_a = _gdn(pa, ps, conv_w[l].astype(F32), _lane_row(a_log[l], H_A), _lane_row(dt_bias[l], H_A),
                   row(gdn_norm_g[l]), batch, seq)
        h = _mix_merge(h, o_a, pb, pc, bias_strips, sinks[l].astype(F32), w_gate, row(b_gate[l]),
                       w_br_a, w_br_b, w_br_c, w_out, row(ln2_g[l]), row(ln2_b[l]), batch, seq, l)
        h = _ffn(h, w_ff2_in, w_ff2_out, row(ln3_g[l]), row(ln3_b[l]), l)
    return h.reshape(batch, seq, d)
```

```python
import functools
import math

import jax
import jax.numpy as jnp
import numpy as np
from jax import lax
from jax.experimental import pallas as pl
from jax.experimental.pallas import tpu as pltpu

F32 = jnp.float32
BF16 = jnp.bfloat16

D_MODEL = 1024
DEPTH = 2
CHUNK = 64
D_FF = 4096
LN_EPS = 1e-5
NORM_EPS = 1e-6
H_A, DK_A, DV_A, CONV_K = 4, 128, 128, 4
A_QK = H_A * DK_A
A_V = H_A * DV_A
H_B, D_B, B_PREV, REL_CLIP = 4, 128, 8, 128
B_W = H_B * D_B
H_C, HKV_C, D_C, WINDOW = 8, 2, 64, 128
G_C = H_C // HKV_C
C_PREV = WINDOW // CHUNK
C_Q = H_C * D_C
C_KV = HKV_C * D_C
N_BRANCH = 3
ALPHA = (2.0 * DEPTH) ** 0.25

LANES = 128
VMEM_LIMIT = 56 * 1024 * 1024
NEG_BIG = -1e30
LOG2E = math.log2(math.e)

SMALL_W = LANES
QKV_A = 2 * A_QK + A_V
A_W = QKV_A + A_V
C_W = C_Q + 2 * C_KV


def _dot(a, b):
    return jnp.dot(a, b, preferred_element_type=F32)


def _dot_nt(a, b):
    return lax.dot_general(a, b, (((1,), (1,)), ((), ())), preferred_element_type=F32)


def _layer_norm(y, g, b):
    mu = jnp.mean(y, axis=-1, keepdims=True)
    yc = y - mu
    var = jnp.mean(yc * yc, axis=-1, keepdims=True)
    return yc * lax.rsqrt(var + LN_EPS) * g + b


def _silu_of_half(xh):
    return xh + xh * jnp.tanh(xh)


def _silu(x):
    return _silu_of_half(0.5 * x)


def _params(*sem):
    return pltpu.CompilerParams(dimension_semantics=sem, vmem_limit_bytes=VMEM_LIMIT)


FFN_TM = 512
FFN_SUB = 256
FFN_TF = 1024


FFN_WCH = 1024


def _stage_cast(pairs, stage, sem):
    copies = [pltpu.make_async_copy(src, stage.at[c % 2], sem.at[c % 2])
              for c, (src, _) in enumerate(pairs)]
    copies[0].start()
    for c, (_, dst) in enumerate(pairs):
        if c + 1 < len(pairs):
            copies[c + 1].start()
        copies[c].wait()
        dst[...] = stage[c % 2].astype(BF16)


def _ffn_kernel(layer, x_ref, wi_hbm, wo_hbm, g_ref, b_ref, o_ref, wi_ref, wo_ref, h_ref, stage, sem):
    @pl.when(pl.program_id(0) == 0)
    def _():
        w = FFN_WCH
        pairs = [(wi_hbm.at[layer, :, pl.ds(c * w, w)], wi_ref.at[:, pl.ds(c * w, w)])
                 for c in range(2 * D_FF // w)]
        pairs += [(wo_hbm.at[layer, pl.ds(c * w, w), :], wo_ref.at[pl.ds(c * w, w), :])
                  for c in range(D_FF // w)]
        _stage_cast(pairs, stage, sem)

    for s in range(FFN_TM // FFN_SUB):
        rows = slice(s * FFN_SUB, (s + 1) * FFN_SUB)
        x = x_ref[rows, :]
        xb = x.astype(BF16)
        for f in range(D_FF // FFN_TF):
            gate = _dot(xb, wi_ref[:, f * FFN_TF:(f + 1) * FFN_TF])
            up = _dot(xb, wi_ref[:, D_FF + f * FFN_TF:D_FF + (f + 1) * FFN_TF])
            h_ref[rows, f * FFN_TF:(f + 1) * FFN_TF] = (_silu(gate) * up).astype(BF16)
        y = ALPHA * x + 0.5 * _dot(h_ref[rows, :], wo_ref[...])
        o_ref[rows, :] = _layer_norm(y, g_ref[...], b_ref[...])


def _ffn(x, w_in, w_out, g, b, layer):
    n = x.shape[0]
    resident = lambda r, w: pl.BlockSpec((r, w), lambda i: (0, 0), pipeline_mode=pl.Buffered(1))
    return pl.pallas_call(
        functools.partial(_ffn_kernel, layer),
        grid=(n // FFN_TM,),
        in_specs=[
            pl.BlockSpec((FFN_TM, D_MODEL), lambda i: (i, 0)),
            pl.BlockSpec(memory_space=pl.ANY), pl.BlockSpec(memory_space=pl.ANY),
            resident(1, D_MODEL), resident(1, D_MODEL),
        ],
        out_specs=pl.BlockSpec((FFN_TM, D_MODEL), lambda i: (i, 0)),
        out_shape=jax.ShapeDtypeStruct((n, D_MODEL), F32),
        scratch_shapes=[pltpu.VMEM((D_MODEL, 2 * D_FF), BF16), pltpu.VMEM((D_FF, D_MODEL), BF16),
                        pltpu.VMEM((FFN_TM, D_FF), BF16),
                        pltpu.VMEM((2, FFN_WCH, FFN_WCH), F32), pltpu.SemaphoreType.DMA((2,))],
        compiler_params=_params("arbitrary"),
        name="ffn_ln",
    )(x, w_in, w_out, g, b)


GDN_R = 256
GDN_NC = GDN_R // CHUNK
GDN_PAIR = 2 * CHUNK
CONV_PAD = 8
LOG2_CHUNK = CHUNK.bit_length() - 1
NEUMANN_STEPS = LOG2_CHUNK - 1


def _proj_gdn_kernel(x_ref, wa_ref, ws_ref, wb_ref, wc_ref, ba_ref, bs_ref, bb_ref, bc_ref,
                     cw_ref, alog_ref, dtb_ref, ng_ref, o_ref, ob_ref, oc_ref,
                     a_ref, s_ref, xbuf, st_ref):
    @pl.when(pl.program_id(1) == 0)
    def _():
        st_ref[...] = jnp.zeros_like(st_ref)
        xbuf[...] = jnp.zeros_like(xbuf)

    xb = x_ref[...].astype(BF16)
    a_ref[...] = _dot(xb, wa_ref[...]) + ba_ref[...]
    s_ref[...] = _dot(xb, ws_ref[...]) + bs_ref[...]
    yb = _dot(xb, wb_ref[...]) + bb_ref[...]
    ob_ref[:, 0:B_W] = (yb[:, 0:B_W] * (D_B ** -0.5 * LOG2E)).astype(BF16)
    ob_ref[:, B_W:3 * B_W] = yb[:, B_W:3 * B_W].astype(BF16)
    yc = _dot(xb, wc_ref[...]) + bc_ref[...]
    oc_ref[:, 0:C_Q] = (yc[:, 0:C_Q] * (D_C ** -0.5 * LOG2E)).astype(BF16)
    oc_ref[:, C_Q:C_W] = yc[:, C_Q:C_W].astype(BF16)

    assert CONV_K == 4
    x_new = a_ref[:, 0:QKV_A]
    n_grp = GDN_R // CONV_PAD
    x3 = x_new.reshape(n_grp, CONV_PAD, QKV_A)
    hist = xbuf[...].reshape(1, CONV_PAD, QKV_A)
    sub = lax.broadcasted_iota(jnp.int32, (n_grp, CONV_PAD, QKV_A), 1)
    w0, w1, w2, w3 = [0.5 * cw_ref[j:j + 1, :].reshape(1, 1, QKV_A) for j in range(CONV_K)]

    def delay(v, v_hist, s):
        rot = pltpu.roll(v, s, axis=1)
        prev = jnp.concatenate([pltpu.roll(v_hist, s, axis=1), rot[0:n_grp - 1]], axis=0)
        return jnp.where(sub >= s, rot, prev)

    dx = delay(x3, hist, 1)
    inner = w1 * x3 + w0 * dx
    inner_hist = w1 * hist + w0 * pltpu.roll(hist, 1, axis=1)
    y_half = w3 * x3 + w2 * dx + delay(inner, inner_hist, 2)
    xbuf[...] = x_new[GDN_R - CONV_PAD:GDN_R]
    qkv = _silu_of_half(y_half.reshape(GDN_R, QKV_A))

    s = s_ref[...]
    beta_all = jax.nn.sigmoid(s)
    g_all = -jnp.exp(alog_ref[...]) * jax.nn.softplus(s + dtb_ref[...])
    row = lax.broadcasted_iota(jnp.int32, (GDN_PAIR, GDN_PAIR), 0)
    col = lax.broadcasted_iota(jnp.int32, (GDN_PAIR, GDN_PAIR), 1)
    same_chunk = jnp.right_shift(row, LOG2_CHUNK) == jnp.right_shift(col, LOG2_CHUNK)
    causal = same_chunk & (col <= row)
    strict = same_chunk & (col < row)
    eye = jnp.where(row == col, 1.0, 0.0).astype(F32)
    pairs = [slice(p * GDN_PAIR, (p + 1) * GDN_PAIR) for p in range(GDN_R // GDN_PAIR)]
    g_hi = g_all.astype(BF16)
    g_r1 = g_all - g_hi.astype(F32)
    g_mid = g_r1.astype(BF16)
    g_lo = (g_r1 - g_mid.astype(F32)).astype(BF16)
    g_split = jnp.concatenate([g_hi, g_mid, g_lo], axis=1)
    tri = jnp.where(causal, 1.0, 0.0).astype(BF16)
    g_parts = jnp.concatenate([_dot(tri, g_split[rows]) for rows in pairs], axis=0)
    gcum = g_parts[:, 0:LANES] + g_parts[:, LANES:2 * LANES] + g_parts[:, 2 * LANES:3 * LANES]
    gsum = jnp.concatenate(
        [jnp.broadcast_to(gcum[(c + 1) * CHUNK - 1:(c + 1) * CHUNK, :], (CHUNK, LANES))
         for c in range(GDN_NC)], axis=0)
    gcum_t = gcum.T
    ng = ng_ref[...]
    heads = range(H_A)

    jobs = [(h, pr) for h in heads for pr in range(len(pairs))]
    qn, rhs, glast, kdec = [], [], [], []
    kk, qk, decay = {}, {}, {}
    for h in heads:
        lo = h * DK_A
        qh = qkv[:, lo:lo + DK_A]
        kh = qkv[:, A_QK + lo:A_QK + lo + DK_A]
        vh = qkv[:, 2 * A_QK + lo:2 * A_QK + lo + DV_A]
        q_h = qh * (lax.rsqrt(jnp.sum(qh * qh, -1, keepdims=True) + NORM_EPS) * (DK_A ** -0.5))
        k_h = kh * lax.rsqrt(jnp.sum(kh * kh, -1, keepdims=True) + NORM_EPS)
        bcol = beta_all[:, h:h + 1]
        g_h = gcum[:, H_A + h:H_A + h + 1]
        gl_h = gsum[:, H_A + h:H_A + h + 1]
        eg = jnp.exp(g_h)
        kb = k_h * bcol
        k_b = k_h.astype(BF16)
        kb_b = kb.astype(BF16)
        q_b = q_h.astype(BF16)
        g_row = gcum_t[H_A + h:H_A + h + 1, :]
        for pr, rows in enumerate(pairs):
            job = (h, pr)
            kk[job] = _dot_nt(kb_b[rows], k_b[rows])
            qk[job] = _dot_nt(q_b[rows], k_b[rows])
            decay[job] = jnp.exp(jnp.where(causal, g_h[rows] - g_row[:, rows], -jnp.inf))
        rhs.append(jnp.concatenate([vh * bcol, kb * eg], axis=1).astype(BF16))
        qn.append(q_h * eg)
        kdec.append(k_h * jnp.exp(gl_h - g_h))
        glast.append(gl_h)

    p = {job: jnp.where(strict, -(kk[job] * decay[job]), 0.0) for job in jobs}
    inv = {job: eye + p[job] for job in jobs}
    for _ in range(NEUMANN_STEPS):
        pb = {job: p[job].astype(BF16) for job in jobs}
        p = {job: _dot(pb[job], pb[job]) for job in jobs}
        inv = {job: inv[job] + _dot(p[job].astype(BF16), inv[job].astype(BF16)) for job in jobs}
    uw_j = {(h, pr): _dot(inv[(h, pr)].astype(BF16), rhs[h][pairs[pr]]).astype(BF16)
            for h, pr in jobs}
    aw_j = {job: _dot((qk[job] * decay[job]).astype(BF16), uw_j[job]) for job in jobs}
    uw = [jnp.concatenate([uw_j[(h, pr)] for pr in range(len(pairs))], axis=0) for h in heads]
    aw = [jnp.concatenate([aw_j[(h, pr)] for pr in range(len(pairs))], axis=0) for h in heads]
    qt = [(qn[h] - aw[h][:, DV_A:DV_A + DK_A]).astype(BF16) for h in heads]

    bp = [[_dot(kdec[h][c * CHUNK:(c + 1) * CHUNK].T.astype(BF16), uw[h][c * CHUNK:(c + 1) * CHUNK])
           for c in range(GDN_NC)] for h in heads]
    state = [st_ref[h] for h in heads]
    for c in range(GDN_NC):
        r0 = c * CHUNK
        for h in heads:
            lo = h * DV_A
            s_b = state[h].astype(BF16)
            o = _dot(qt[h][r0:r0 + CHUNK], s_b) + aw[h][r0:r0 + CHUNK, 0:DV_A]
            decay_c = jnp.exp(glast[h][r0:r0 + 1])
            state[h] = (state[h] * decay_c + bp[h][c][:, 0:DV_A]
                        - _dot(bp[h][c][:, DV_A:DV_A + DK_A].astype(BF16), s_b))
            z = a_ref[r0:r0 + CHUNK, QKV_A + lo:QKV_A + lo + DV_A]
            on = o * lax.rsqrt(jnp.mean(o * o, -1, keepdims=True) + NORM_EPS) * ng * _silu(z)
            o_ref[r0:r0 + CHUNK, lo:lo + DV_A] = on.astype(BF16)
    for h in heads:
        st_ref[h] = state[h]


def _proj_gdn(x, wa, ws, wb, wc, ba, bs, bb, bc, conv_w, alog_row, dtb_row, norm_g, batch, seq):
    n = x.shape[0]
    nblk = seq // GDN_R
    widths = (A_W, SMALL_W, 3 * B_W, C_W)
    row = lambda w: pl.BlockSpec((GDN_R, w), lambda b, i: (b * nblk + i, 0))
    const = lambda r, w: pl.BlockSpec((r, w), lambda b, i: (0, 0))
    return pl.pallas_call(
        _proj_gdn_kernel,
        grid=(batch, nblk),
        in_specs=[row(D_MODEL)] + [const(D_MODEL, w) for w in widths] + [const(1, w) for w in widths]
        + [const(CONV_K, QKV_A), const(1, SMALL_W), const(1, SMALL_W), const(1, DV_A)],
        out_specs=[row(A_V), row(3 * B_W), row(C_W)],
        out_shape=[jax.ShapeDtypeStruct((n, A_V), BF16), jax.ShapeDtypeStruct((n, 3 * B_W), BF16),
                   jax.ShapeDtypeStruct((n, C_W), BF16)],
        scratch_shapes=[pltpu.VMEM((GDN_R, A_W), F32), pltpu.VMEM((GDN_R, SMALL_W), F32),
                        pltpu.VMEM((CONV_PAD, QKV_A), F32), pltpu.VMEM((H_A, DK_A, DV_A), F32)],
        compiler_params=_params("parallel", "arbitrary"),
        name="proj_gdn",
    )(x, wa, ws, wb, wc, ba, bs, bb, bc, conv_w, alog_row, dtb_row, norm_g)


ATT_R = 256
ATT_CH = ATT_R // CHUNK
B_KBLK = 3
assert (B_KBLK - 1) * ATT_CH == B_PREV
B_STRIP_W = B_KBLK * ATT_R + (ATT_CH - 2) * CHUNK
B_QR = 2 * CHUNK
B_KW = (B_PREV + 2) * CHUNK
B_BIAS0 = (ATT_CH - 2) * CHUNK


def _attn_b_pieces():
    pieces = []
    for half in range(ATT_R // B_QR):
        lo, cut = half * B_QR, []
        while lo < half * B_QR + B_KW:
            hi = min((lo // ATT_R + 1) * ATT_R, half * B_QR + B_KW)
            cut.append((lo // ATT_R, lo % ATT_R, hi - lo))
            lo = hi
        pieces.append(cut)
    return pieces


B_JOBS = [(h, half) for h in range(H_B) for half in range(ATT_R // B_QR)]


def _attn_b_scores(i, q_ref, k_refs, bias_ref):
    pieces = _attn_b_pieces()
    scores = []
    for h, half in B_JOBS:
        cols = slice(h * D_B, (h + 1) * D_B)
        q = q_ref[half * B_QR:(half + 1) * B_QR, cols]
        parts = []
        for j, r0, nr in pieces[half]:
            sj = _dot_nt(q, k_refs[j][r0:r0 + nr, cols])
            if j < B_KBLK - 1:
                sj = jnp.where(i + j >= B_KBLK - 1, sj, NEG_BIG)
            parts.append(sj)
        bias = jnp.concatenate([bias_ref[h, 1, :, B_BIAS0:B_BIAS0 + B_KW],
                                bias_ref[h, 0, :, B_BIAS0:B_BIAS0 + B_KW]], axis=0)
        scores.append(jnp.concatenate(parts, axis=1) + bias)
    return scores


def _attn_b_finish(scores, v_refs, o_ref):
    pieces = _attn_b_pieces()
    probs, denoms = [], []
    for s in scores:
        p = jnp.exp2(s - jnp.max(s, axis=-1, keepdims=True))
        denoms.append(jnp.sum(p, axis=-1, keepdims=True))
        probs.append(p.astype(BF16))
    for (h, half), pb, denom in zip(B_JOBS, probs, denoms):
        cols = slice(h * D_B, (h + 1) * D_B)
        o, c0 = None, 0
        for j, r0, nr in pieces[half]:
            term = _dot(pb[:, c0:c0 + nr], v_refs[j][r0:r0 + nr, cols])
            o = term if o is None else o + term
            c0 += nr
        o_ref[half * B_QR:(half + 1) * B_QR, cols] = (o / denom).astype(BF16)


def _attn_b_bias(rel_bias):
    nh, nrel = rel_bias.shape
    assert nrel == CHUNK + REL_CLIP
    lead = (ATT_CH - 1) * CHUNK
    vis_end = lead + (B_PREV + 1) * CHUNK
    width = B_KBLK * ATT_R + lead
    period = 1 << (width + CHUNK - 1).bit_length()
    rb = rel_bias.astype(F32)
    far = jnp.broadcast_to(rb[:, -1:], (nh, period))
    f = jnp.concatenate([far[:, :vis_end - nrel], jnp.flip(rb, axis=1), far[:, :period - vis_end]], axis=1)
    g = jnp.concatenate([f[:, :1], jnp.flip(f[:, 1:], axis=1)], axis=1)
    skew = jnp.tile(g, (1, CHUNK + 1))[:, :CHUNK * (period + 1)].reshape(nh, CHUNK, period + 1)
    toep = LOG2E * jnp.flip(skew[:, :, period - width + 1:period + 1], axis=2)
    c = np.arange(width)
    strip = jnp.where(jnp.asarray((c >= lead) & (c < vis_end))[None, None, :], toep, NEG_BIG)
    return jnp.stack([strip[:, :, 0:B_STRIP_W], strip[:, :, CHUNK:CHUNK + B_STRIP_W]], axis=1)


C_KR = 2 * CHUNK
C_KBLK = 3


C_JOBS = [(h, half) for h in range(H_C) for half in range(ATT_R // C_KR)]


def _kv_cols(h):
    return slice((h // G_C) * D_C, (h // G_C + 1) * D_C)


def _attn_c_scores(i, q_ref, k_refs, tbl_ref):
    scores = []
    for h, half in C_JOBS:
        q = q_ref[half * C_KR:(half + 1) * C_KR, h * D_C:(h + 1) * D_C]
        s_prev = _dot_nt(q, k_refs[half][:, _kv_cols(h)])
        if half == 0:
            s_prev = jnp.where(i > 0, s_prev, NEG_BIG)
        s_own = _dot_nt(q, k_refs[half + 1][:, _kv_cols(h)])
        scores.append(jnp.concatenate([s_prev, s_own], axis=1) + tbl_ref[h])
    return scores


def _attn_c_finish(scores, sink_ref, v_refs, o_ref):
    probs, denoms = [], []
    for (h, half), s in zip(C_JOBS, scores):
        sink = sink_ref[h] * LOG2E
        mx = jnp.maximum(jnp.max(s, axis=-1, keepdims=True), sink)
        p = jnp.exp2(s - mx)
        denoms.append(jnp.sum(p, axis=-1, keepdims=True) + jnp.exp2(sink - mx))
        probs.append(p.astype(BF16))
    for (h, half), pb, denom in zip(C_JOBS, probs, denoms):
        o = (_dot(pb[:, 0:C_KR], v_refs[half][:, _kv_cols(h)])
             + _dot(pb[:, C_KR:2 * C_KR], v_refs[half + 1][:, _kv_cols(h)]))
        o_ref[half * C_KR:(half + 1) * C_KR, h * D_C:(h + 1) * D_C] = (o / denom).astype(BF16)


def _attn_c_table():
    tq = np.arange(C_KR)[:, None] + C_KR
    tk = np.arange(2 * C_KR)[None, :]
    delta = tq // CHUNK - tk // CHUNK
    vis = (delta >= 0) & (delta <= C_PREV)
    slopes = 2.0 ** (-8.0 * np.arange(1, H_C + 1) / H_C)
    alibi = -slopes[:, None, None] * np.abs(tq - tk)[None].astype(np.float64)
    tbl = np.where(vis[None], LOG2E * alibi, NEG_BIG)
    return jnp.asarray(tbl, F32)


MERGE_WR = 512


def _mix_merge_kernel(layer, sink_ref, x_ref, oa_ref,
                      qb_ref, kb0_ref, kb1_ref, kb2_ref, vb0_ref, vb1_ref, vb2_ref, bias_ref,
                      qc_ref, kc0_ref, kc1_ref, kc2_ref, vc0_ref, vc1_ref, vc2_ref, tbl_ref,
                      wg_hbm, bg_ref, wa_hbm, wb_hbm, wc_hbm, wo_hbm, g_ref, b_ref,
                      o_ref,
                      wg_ref, wa_ref, wb_ref, wc_ref, wo_ref, stage, sem, ob_ref, oc_ref):
    i = pl.program_id(1)

    @pl.when((pl.program_id(0) == 0) & (i == 0))
    def _():
        r = MERGE_WR
        pairs = [(wg_hbm.at[layer, pl.ds(m * r, r), pl.ds(j * D_MODEL, D_MODEL)],
                  wg_ref.at[pl.ds(m * r, r), pl.ds(j * D_MODEL, D_MODEL)])
                 for m in range(D_MODEL // r) for j in range(N_BRANCH)]
        pairs += [(src.at[layer], dst) for src, dst in
                  ((wa_hbm, wa_ref), (wb_hbm, wb_ref), (wc_hbm, wc_ref))]
        pairs += [(wo_hbm.at[layer, pl.ds(m * r, r), :], wo_ref.at[pl.ds(m * r, r), :])
                  for m in range(D_MODEL // r)]
        _stage_cast(pairs, stage, sem)

    scores_b = _attn_b_scores(i, qb_ref, (kb0_ref, kb1_ref, kb2_ref), bias_ref)
    scores_c = _attn_c_scores(i, qc_ref, (kc0_ref, kc1_ref, kc2_ref), tbl_ref)
    x = x_ref[...]
    xb = x.astype(BF16)
    gates = [jax.nn.sigmoid(_dot(xb, wg_ref[:, r * D_MODEL:(r + 1) * D_MODEL])
                            + bg_ref[:, r * D_MODEL:(r + 1) * D_MODEL]) for r in range(N_BRANCH)]
    merged = gates[0] * _dot(oa_ref[...], wa_ref[...])
    _attn_b_finish(scores_b, (vb0_ref, vb1_ref, vb2_ref), ob_ref)
    _attn_c_finish(scores_c, sink_ref, (vc0_ref, vc1_ref, vc2_ref), oc_ref)
    merged = merged + gates[1] * _dot(ob_ref[...], wb_ref[...])
    merged = merged + gates[2] * _dot(oc_ref[...], wc_ref[...])
    y = ALPHA * x + _dot(merged.astype(BF16), wo_ref[...])
    o_ref[...] = _layer_norm(y, g_ref[...], b_ref[...])


def _mix_merge(x, oa, qkv_b, qkv_c, bias, sinks, w_gate, b_gate, w_a, w_b, w_c, w_out, g, b,
               batch, seq, layer):
    n = x.shape[0]
    assert A_V == B_W == C_Q == MERGE_WR
    nblk = seq // ATT_R
    kper = ATT_R // C_KR
    kcol, vcol = C_Q // C_KV, C_Q // C_KV + 1
    row = lambda w: pl.BlockSpec((ATT_R, w), lambda bb, i: (bb * nblk + i, 0))
    const = lambda *shape: pl.BlockSpec(shape, lambda bb, i: (0,) * len(shape))
    hbm = pl.BlockSpec(memory_space=pl.ANY)

    def kv_b(j, colblk):
        return pl.BlockSpec(
            (ATT_R, B_W), lambda bb, i: (bb * nblk + jnp.maximum(i + j - (B_KBLK - 1), 0), colblk))

    def kv_c(j, colblk):
        return pl.BlockSpec(
            (C_KR, C_KV),
            lambda bb, i: (bb * nblk * kper + jnp.maximum(i * kper + j - 1, 0), colblk))

    in_specs = ([pl.BlockSpec(memory_space=pltpu.SMEM), row(D_MODEL), row(A_V), row(B_W)]
                + [kv_b(j, 1) for j in range(B_KBLK)] + [kv_b(j, 2) for j in range(B_KBLK)]
                + [pl.BlockSpec((H_B, 2, CHUNK, B_STRIP_W), lambda bb, i: (layer, 0, 0, 0)), row(C_Q)]
                + [kv_c(j, kcol) for j in range(C_KBLK)] + [kv_c(j, vcol) for j in range(C_KBLK)]
                + [const(H_C, C_KR, 2 * C_KR),
                   hbm, const(1, N_BRANCH * D_MODEL), hbm, hbm, hbm, hbm,
                   const(1, D_MODEL), const(1, D_MODEL)])
    return pl.pallas_call(
        functools.partial(_mix_merge_kernel, layer),
        grid=(batch, nblk),
        in_specs=in_specs,
        out_specs=row(D_MODEL),
        out_shape=jax.ShapeDtypeStruct((n, D_MODEL), F32),
        scratch_shapes=[pltpu.VMEM((D_MODEL, N_BRANCH * D_MODEL), BF16),
                        pltpu.VMEM((A_V, D_MODEL), BF16), pltpu.VMEM((B_W, D_MODEL), BF16),
                        pltpu.VMEM((C_Q, D_MODEL), BF16), pltpu.VMEM((D_MODEL, D_MODEL), BF16),
                        pltpu.VMEM((2, MERGE_WR, D_MODEL), F32), pltpu.SemaphoreType.DMA((2,)),
                        pltpu.VMEM((ATT_R, B_W), BF16), pltpu.VMEM((ATT_R, C_Q), BF16)],
        compiler_params=_params("arbitrary", "arbitrary"),
        name="mix_merge",
    )(sinks, x, oa, qkv_b, qkv_b, qkv_b, qkv_b, qkv_b, qkv_b, qkv_b, bias,
      qkv_c, qkv_c, qkv_c, qkv_c, qkv_c, qkv_c, qkv_c, _attn_c_table(),
      w_gate, b_gate, w_a, w_b, w_c, w_out, g, b)


def _lane_row(vals, offset):
    return jnp.zeros((1, LANES), F32).at[0, offset:offset + vals.shape[0]].set(vals.astype(F32))


def kernel(x, ln1_g, ln1_b, w_ff1_in, w_ff1_out, w_in, b_in, conv_w, a_log, dt_bias, gdn_norm_g,
           rel_bias, sinks, w_gate, b_gate, w_br_a, w_br_b, w_br_c, w_out, ln2_g, ln2_b,
           w_ff2_in, w_ff2_out, ln3_g, ln3_b):
    batch, seq, d = x.shape
    assert d == D_MODEL and seq % ATT_R == 0 and (batch * seq) % FFN_TM == 0
    h = x.reshape(batch * seq, d)
    row = lambda v: v.reshape(1, -1).astype(F32)
    o_small = QKV_A
    o_z = o_small + 2 * H_A
    o_b = o_z + A_V
    o_c = o_b + 3 * B_W
    bias_strips = _attn_b_bias(rel_bias.reshape(DEPTH * H_B, CHUNK + REL_CLIP))
    for l in range(DEPTH):
        h = _ffn(h, w_ff1_in, w_ff1_out, row(ln1_g[l]), row(ln1_b[l]), l)
        wi, bi = w_in[l], b_in[l]
        pad = SMALL_W - 2 * H_A
        wa = jnp.concatenate([wi[:, 0:QKV_A], wi[:, o_z:o_b]], axis=1).astype(BF16)
        ws = jnp.pad(wi[:, o_small:o_z], ((0, 0), (0, pad))).astype(BF16)
        ba = jnp.concatenate([bi[0:QKV_A], bi[o_z:o_b]])
        bs = jnp.pad(bi[o_small:o_z], (0, pad))
        o_a, pb, pc = _proj_gdn(h, wa, ws, wi[:, o_b:o_c].astype(BF16), wi[:, o_c:].astype(BF16),
                                row(ba), row(bs), row(bi[o_b:o_c]), row(bi[o_c:]),
                                conv_w[l].astype(F32), _lane_row(a_log[l], H_A),
                                _lane_row(dt_bias[l], H_A), row(gdn_norm_g[l]), batch, seq)
        h = _mix_merge(h, o_a, pb, pc, bias_strips, sinks[l].astype(F32), w_gate, row(b_gate[l]),
                       w_br_a, w_br_b, w_br_c, w_out, row(ln2_g[l]), row(ln2_b[l]), batch, seq, l)
        h = _ffn(h, w_ff2_in, w_ff2_out, row(ln3_g[l]), row(ln3_b[l]), l)
    return h.reshape(batch, seq, d)
```

```python
import functools
import math

import jax
import jax.numpy as jnp
import numpy as np
from jax import lax
from jax.experimental import pallas as pl
from jax.experimental.pallas import tpu as pltpu

F32 = jnp.float32
BF16 = jnp.bfloat16

D_MODEL = 1024
DEPTH = 2
CHUNK = 64
D_FF = 4096
LN_EPS = 1e-5
NORM_EPS = 1e-6
H_A, DK_A, DV_A, CONV_K = 4, 128, 128, 4
A_QK = H_A * DK_A
A_V = H_A * DV_A
H_B, D_B, B_PREV, REL_CLIP = 4, 128, 8, 128
B_W = H_B * D_B
H_C, HKV_C, D_C, WINDOW = 8, 2, 64, 128
G_C = H_C // HKV_C
C_PREV = WINDOW // CHUNK
C_Q = H_C * D_C
C_KV = HKV_C * D_C
N_BRANCH = 3
ALPHA = (2.0 * DEPTH) ** 0.25

LANES = 128
VMEM_LIMIT = 56 * 1024 * 1024
NEG_BIG = -1e30
LOG2E = math.log2(math.e)

SMALL_W = LANES
QKV_A = 2 * A_QK + A_V
A_W = QKV_A + A_V
C_W = C_Q + 2 * C_KV


def _dot(a, b):
    return jnp.dot(a, b, preferred_element_type=F32)


def _dot_nt(a, b):
    return lax.dot_general(a, b, (((1,), (1,)), ((), ())), preferred_element_type=F32)


def _layer_norm(y, g, b):
    mu = jnp.mean(y, axis=-1, keepdims=True)
    yc = y - mu
    var = jnp.mean(yc * yc, axis=-1, keepdims=True)
    return yc * lax.rsqrt(var + LN_EPS) * g + b


def _silu_of_half(xh):
    return xh + xh * jnp.tanh(xh)


def _silu(x):
    return _silu_of_half(0.5 * x)


def _params(*sem):
    return pltpu.CompilerParams(dimension_semantics=sem, vmem_limit_bytes=VMEM_LIMIT)


FFN_TM = 512
FFN_SUB = 256
FFN_TF = 1024


FFN_WCH = 1024


def _stage_cast(pairs, stage, sem):
    copies = [pltpu.make_async_copy(src, stage.at[c % 2], sem.at[c % 2])
              for c, (src, _) in enumerate(pairs)]
    copies[0].start()
    for c, (_, dst) in enumerate(pairs):
        if c + 1 < len(pairs):
            copies[c + 1].start()
        copies[c].wait()
        dst[...] = stage[c % 2].astype(BF16)


def _ffn_kernel(layer, x_ref, wi_hbm, wo_hbm, g_ref, b_ref, o_ref, wi_ref, wo_ref, h_ref, stage, sem):
    @pl.when(pl.program_id(0) == 0)
    def _():
        w = FFN_WCH
        pairs = [(wi_hbm.at[layer, :, pl.ds(c * w, w)], wi_ref.at[:, pl.ds(c * w, w)])
                 for c in range(2 * D_FF // w)]
        pairs += [(wo_hbm.at[layer, pl.ds(c * w, w), :], wo_ref.at[pl.ds(c * w, w), :])
                  for c in range(D_FF // w)]
        _stage_cast(pairs, stage, sem)

    for s in range(FFN_TM // FFN_SUB):
        rows = slice(s * FFN_SUB, (s + 1) * FFN_SUB)
        x = x_ref[rows, :]
        xb = x.astype(BF16)
        for f in range(D_FF // FFN_TF):
            gate = _dot(xb, wi_ref[:, f * FFN_TF:(f + 1) * FFN_TF])
            up = _dot(xb, wi_ref[:, D_FF + f * FFN_TF:D_FF + (f + 1) * FFN_TF])
            h_ref[rows, f * FFN_TF:(f + 1) * FFN_TF] = (_silu(gate) * up).astype(BF16)
        y = ALPHA * x + 0.5 * _dot(h_ref[rows, :], wo_ref[...])
        o_ref[rows, :] = _layer_norm(y, g_ref[...], b_ref[...])


def _ffn(x, w_in, w_out, g, b, layer):
    n = x.shape[0]
    resident = lambda r, w: pl.BlockSpec((r, w), lambda i: (0, 0), pipeline_mode=pl.Buffered(1))
    return pl.pallas_call(
        functools.partial(_ffn_kernel, layer),
        grid=(n // FFN_TM,),
        in_specs=[
            pl.BlockSpec((FFN_TM, D_MODEL), lambda i: (i, 0)),
            pl.BlockSpec(memory_space=pl.ANY), pl.BlockSpec(memory_space=pl.ANY),
            resident(1, D_MODEL), resident(1, D_MODEL),
        ],
        out_specs=pl.BlockSpec((FFN_TM, D_MODEL), lambda i: (i, 0)),
        out_shape=jax.ShapeDtypeStruct((n, D_MODEL), F32),
        scratch_shapes=[pltpu.VMEM((D_MODEL, 2 * D_FF), BF16), pltpu.VMEM((D_FF, D_MODEL), BF16),
                        pltpu.VMEM((FFN_TM, D_FF), BF16),
                        pltpu.VMEM((2, FFN_WCH, FFN_WCH), F32), pltpu.SemaphoreType.DMA((2,))],
        compiler_params=_params("arbitrary"),
        name="ffn_ln",
    )(x, w_in, w_out, g, b)


GDN_R = 256
GDN_NC = GDN_R // CHUNK
GDN_PAIR = 2 * CHUNK
CONV_PAD = 8
LOG2_CHUNK = CHUNK.bit_length() - 1
NEUMANN_STEPS = LOG2_CHUNK - 1


def _proj_gdn_kernel(x_ref, wa_ref, ws_ref, wb_ref, wc_ref, ba_ref, bs_ref, bb_ref, bc_ref,
                     cw_ref, alog_ref, dtb_ref, ng_ref, o_ref, ob_ref, oc_ref,
                     a_ref, s_ref, xbuf, st_ref):
    @pl.when(pl.program_id(1) == 0)
    def _():
        st_ref[...] = jnp.zeros_like(st_ref)
        xbuf[...] = jnp.zeros_like(xbuf)

    xb = x_ref[...].astype(BF16)
    a_ref[...] = _dot(xb, wa_ref[...]) + ba_ref[...]
    s_ref[...] = _dot(xb, ws_ref[...]) + bs_ref[...]
    yb = _dot(xb, wb_ref[...]) + bb_ref[...]
    ob_ref[:, 0:B_W] = (yb[:, 0:B_W] * (D_B ** -0.5 * LOG2E)).astype(BF16)
    ob_ref[:, B_W:3 * B_W] = yb[:, B_W:3 * B_W].astype(BF16)
    yc = _dot(xb, wc_ref[...]) + bc_ref[...]
    oc_ref[:, 0:C_Q] = (yc[:, 0:C_Q] * (D_C ** -0.5 * LOG2E)).astype(BF16)
    oc_ref[:, C_Q:C_W] = yc[:, C_Q:C_W].astype(BF16)

    assert CONV_K == 4
    x_new = a_ref[:, 0:QKV_A]
    n_grp = GDN_R // CONV_PAD
    x3 = x_new.reshape(n_grp, CONV_PAD, QKV_A)
    hist = xbuf[...].reshape(1, CONV_PAD, QKV_A)
    sub = lax.broadcasted_iota(jnp.int32, (n_grp, CONV_PAD, QKV_A), 1)
    w0, w1, w2, w3 = [0.5 * cw_ref[j:j + 1, :].reshape(1, 1, QKV_A) for j in range(CONV_K)]

    def delay(v, v_hist, s):
        rot = pltpu.roll(v, s, axis=1)
        prev = jnp.concatenate([pltpu.roll(v_hist, s, axis=1), rot[0:n_grp - 1]], axis=0)
        return jnp.where(sub >= s, rot, prev)

    dx = delay(x3, hist, 1)
    inner = w1 * x3 + w0 * dx
    inner_hist = w1 * hist + w0 * pltpu.roll(hist, 1, axis=1)
    y_half = w3 * x3 + w2 * dx + delay(inner, inner_hist, 2)
    xbuf[...] = x_new[GDN_R - CONV_PAD:GDN_R]
    qkv = _silu_of_half(y_half.reshape(GDN_R, QKV_A))

    s = s_ref[...]
    beta_all = jax.nn.sigmoid(s)
    g_all = -jnp.exp(alog_ref[...]) * jax.nn.softplus(s + dtb_ref[...])
    row = lax.broadcasted_iota(jnp.int32, (GDN_PAIR, GDN_PAIR), 0)
    col = lax.broadcasted_iota(jnp.int32, (GDN_PAIR, GDN_PAIR), 1)
    same_chunk = jnp.right_shift(row, LOG2_CHUNK) == jnp.right_shift(col, LOG2_CHUNK)
    causal = same_chunk & (col <= row)
    strict = same_chunk & (col < row)
    eye = jnp.where(row == col, 1.0, 0.0).astype(F32)
    pairs = [slice(p * GDN_PAIR, (p + 1) * GDN_PAIR) for p in range(GDN_R // GDN_PAIR)]
    g_hi = g_all.astype(BF16)
    g_r1 = g_all - g_hi.astype(F32)
    g_mid = g_r1.astype(BF16)
    g_lo = (g_r1 - g_mid.astype(F32)).astype(BF16)
    g_split = jnp.concatenate([g_hi, g_mid, g_lo], axis=1)
    tri = jnp.where(causal, 1.0, 0.0).astype(BF16)
    g_parts = jnp.concatenate([_dot(tri, g_split[rows]) for rows in pairs], axis=0)
    gcum = g_parts[:, 0:LANES] + g_parts[:, LANES:2 * LANES] + g_parts[:, 2 * LANES:3 * LANES]
    gsum = jnp.concatenate(
        [jnp.broadcast_to(gcum[(c + 1) * CHUNK - 1:(c + 1) * CHUNK, :], (CHUNK, LANES))
         for c in range(GDN_NC)], axis=0)
    gcum_t = gcum.T
    ng = ng_ref[...]
    heads = range(H_A)

    jobs = [(h, pr) for h in heads for pr in range(len(pairs))]
    qn, rhs, glast, kdec = [], [], [], []
    kk, qk, decay = {}, {}, {}
    for h in heads:
        lo = h * DK_A
        qh = qkv[:, lo:lo + DK_A]
        kh = qkv[:, A_QK + lo:A_QK + lo + DK_A]
        vh = qkv[:, 2 * A_QK + lo:2 * A_QK + lo + DV_A]
        q_h = qh * (lax.rsqrt(jnp.sum(qh * qh, -1, keepdims=True) + NORM_EPS) * (DK_A ** -0.5))
        k_h = kh * lax.rsqrt(jnp.sum(kh * kh, -1, keepdims=True) + NORM_EPS)
        bcol = beta_all[:, h:h + 1]
        g_h = gcum[:, H_A + h:H_A + h + 1]
        gl_h = gsum[:, H_A + h:H_A + h + 1]
        eg = jnp.exp(g_h)
        kb = k_h * bcol
        k_b = k_h.astype(BF16)
        kb_b = kb.astype(BF16)
        q_b = q_h.astype(BF16)
        g_row = gcum_t[H_A + h:H_A + h + 1, :]
        for pr, rows in enumerate(pairs):
            job = (h, pr)
            kk[job] = _dot_nt(kb_b[rows], k_b[rows])
            qk[job] = _dot_nt(q_b[rows], k_b[rows])
            decay[job] = jnp.exp(jnp.where(causal, g_h[rows] - g_row[:, rows], -jnp.inf))
        rhs.append(jnp.concatenate([vh * bcol, kb * eg], axis=1).astype(BF16))
        qn.append(q_h * eg)
        kdec.append(k_h * jnp.exp(gl_h - g_h))
        glast.append(gl_h)

    p = {job: jnp.where(strict, -(kk[job] * decay[job]), 0.0) for job in jobs}
    inv = {job: eye + p[job] for job in jobs}
    for _ in range(NEUMANN_STEPS):
        pb = {job: p[job].astype(BF16) for job in jobs}
        p = {job: _dot(pb[job], pb[job]) for job in jobs}
        inv = {job: inv[job] + _dot(p[job].astype(BF16), inv[job].astype(BF16)) for job in jobs}
    uw_j = {(h, pr): _dot(inv[(h, pr)].astype(BF16), rhs[h][pairs[pr]]).astype(BF16)
            for h, pr in jobs}
    aw_j = {job: _dot((qk[job] * decay[job]).astype(BF16), uw_j[job]) for job in jobs}
    uw = [jnp.concatenate([uw_j[(h, pr)] for pr in range(len(pairs))], axis=0) for h in heads]
    aw = [jnp.concatenate([aw_j[(h, pr)] for pr in range(len(pairs))], axis=0) for h in heads]
    qt = [(qn[h] - aw[h][:, DV_A:DV_A + DK_A]).astype(BF16) for h in heads]

    bp = [[_dot(kdec[h][c * CHUNK:(c + 1) * CHUNK].T.astype(BF16), uw[h][c * CHUNK:(c + 1) * CHUNK])
           for c in range(GDN_NC)] for h in heads]
    state = [st_ref[h] for h in heads]
    for c in range(GDN_NC):
        r0 = c * CHUNK
        for h in heads:
            lo = h * DV_A
            s_b = state[h].astype(BF16)
            o = _dot(qt[h][r0:r0 + CHUNK], s_b) + aw[h][r0:r0 + CHUNK, 0:DV_A]
            decay_c = jnp.exp(glast[h][r0:r0 + 1])
            state[h] = (state[h] * decay_c + bp[h][c][:, 0:DV_A]
                        - _dot(bp[h][c][:, DV_A:DV_A + DK_A].astype(BF16), s_b))
            z = a_ref[r0:r0 + CHUNK, QKV_A + lo:QKV_A + lo + DV_A]
            on = o * lax.rsqrt(jnp.mean(o * o, -1, keepdims=True) + NORM_EPS) * ng * _silu(z)
            o_ref[r0:r0 + CHUNK, lo:lo + DV_A] = on.astype(BF16)
    for h in heads:
        st_ref[h] = state[h]


def _proj_gdn(x, wa, ws, wb, wc, ba, bs, bb, bc, conv_w, alog_row, dtb_row, norm_g, batch, seq):
    n = x.shape[0]
    nblk = seq // GDN_R
    widths = (A_W, SMALL_W, 3 * B_W, C_W)
    row = lambda w: pl.BlockSpec((GDN_R, w), lambda b, i: (b * nblk + i, 0))
    const = lambda r, w: pl.BlockSpec((r, w), lambda b, i: (0, 0))
    return pl.pallas_call(
        _proj_gdn_kernel,
        grid=(batch, nblk),
        in_specs=[row(D_MODEL)] + [const(D_MODEL, w) for w in widths] + [const(1, w) for w in widths]
        + [const(CONV_K, QKV_A), const(1, SMALL_W), const(1, SMALL_W), const(1, DV_A)],
        out_specs=[row(A_V), row(3 * B_W), row(C_W)],
        out_shape=[jax.ShapeDtypeStruct((n, A_V), BF16), jax.ShapeDtypeStruct((n, 3 * B_W), BF16),
                   jax.ShapeDtypeStruct((n, C_W), BF16)],
        scratch_shapes=[pltpu.VMEM((GDN_R, A_W), F32), pltpu.VMEM((GDN_R, SMALL_W), F32),
                        pltpu.VMEM((CONV_PAD, QKV_A), F32), pltpu.VMEM((H_A, DK_A, DV_A), F32)],
        compiler_params=_params("parallel", "arbitrary"),
        name="proj_gdn",
    )(x, wa, ws, wb, wc, ba, bs, bb, bc, conv_w, alog_row, dtb_row, norm_g)


ATT_R = 256
ATT_CH = ATT_R // CHUNK
B_KBLK = 3
assert (B_KBLK - 1) * ATT_CH == B_PREV
B_STRIP_W = B_KBLK * ATT_R + (ATT_CH - 2) * CHUNK
B_QR = 2 * CHUNK
B_KW = (B_PREV + 2) * CHUNK
B_BIAS0 = (ATT_CH - 2) * CHUNK


def _attn_b_pieces():
    pieces = []
    for half in range(ATT_R // B_QR):
        lo, cut = half * B_QR, []
        while lo < half * B_QR + B_KW:
            hi = min((lo // ATT_R + 1) * ATT_R, half * B_QR + B_KW)
            cut.append((lo // ATT_R, lo % ATT_R, hi - lo))
            lo = hi
        pieces.append(cut)
    return pieces


B_JOBS = [(h, half) for h in range(H_B) for half in range(ATT_R // B_QR)]


def _attn_b_scores(i, q_ref, k_refs, bias_ref, row0):
    pieces = _attn_b_pieces()
    scores = []
    for h, half in B_JOBS:
        cols = slice(h * D_B, (h + 1) * D_B)
        q = q_ref[row0 + half * B_QR:row0 + (half + 1) * B_QR, cols]
        parts = []
        for j, r0, nr in pieces[half]:
            sj = _dot_nt(q, k_refs[j][r0:r0 + nr, cols])
            if j < B_KBLK - 1:
                sj = jnp.where(i + j >= B_KBLK - 1, sj, NEG_BIG)
            parts.append(sj)
        bias = jnp.concatenate([bias_ref[h, 1, :, B_BIAS0:B_BIAS0 + B_KW],
                                bias_ref[h, 0, :, B_BIAS0:B_BIAS0 + B_KW]], axis=0)
        scores.append(jnp.concatenate(parts, axis=1) + bias)
    return scores


def _attn_b_finish(scores, v_refs, o_ref, row0):
    pieces = _attn_b_pieces()
    probs, denoms = [], []
    for s in scores:
        p = jnp.exp2(s - jnp.max(s, axis=-1, keepdims=True))
        denoms.append(jnp.sum(p, axis=-1, keepdims=True))
        probs.append(p.astype(BF16))
    for (h, half), pb, denom in zip(B_JOBS, probs, denoms):
        cols = slice(h * D_B, (h + 1) * D_B)
        o, c0 = None, 0
        for j, r0, nr in pieces[half]:
            term = _dot(pb[:, c0:c0 + nr], v_refs[j][r0:r0 + nr, cols])
            o = term if o is None else o + term
            c0 += nr
        o_ref[row0 + half * B_QR:row0 + (half + 1) * B_QR, cols] = (o / denom).astype(BF16)


def _attn_b_bias(rel_bias):
    nh, nrel = rel_bias.shape
    assert nrel == CHUNK + REL_CLIP
    lead = (ATT_CH - 1) * CHUNK
    vis_end = lead + (B_PREV + 1) * CHUNK
    width = B_KBLK * ATT_R + lead
    period = 1 << (width + CHUNK - 1).bit_length()
    rb = rel_bias.astype(F32)
    far = jnp.broadcast_to(rb[:, -1:], (nh, period))
    f = jnp.concatenate([far[:, :vis_end - nrel], jnp.flip(rb, axis=1), far[:, :period - vis_end]], axis=1)
    g = jnp.concatenate([f[:, :1], jnp.flip(f[:, 1:], axis=1)], axis=1)
    skew = jnp.tile(g, (1, CHUNK + 1))[:, :CHUNK * (period + 1)].reshape(nh, CHUNK, period + 1)
    toep = LOG2E * jnp.flip(skew[:, :, period - width + 1:period + 1], axis=2)
    c = np.arange(width)
    strip = jnp.where(jnp.asarray((c >= lead) & (c < vis_end))[None, None, :], toep, NEG_BIG)
    return jnp.stack([strip[:, :, 0:B_STRIP_W], strip[:, :, CHUNK:CHUNK + B_STRIP_W]], axis=1)


C_KR = 2 * CHUNK
C_KBLK = 3


C_JOBS = [(h, half) for h in range(H_C) for half in range(ATT_R // C_KR)]


def _kv_cols(h):
    return slice((h // G_C) * D_C, (h // G_C + 1) * D_C)


def _attn_c_scores(i, q_ref, k_refs, tbl_ref, row0):
    scores = []
    for h, half in C_JOBS:
        q = q_ref[row0 + half * C_KR:row0 + (half + 1) * C_KR, h * D_C:(h + 1) * D_C]
        s_prev = _dot_nt(q, k_refs[half][:, _kv_cols(h)])
        if half == 0:
            s_prev = jnp.where(i > 0, s_prev, NEG_BIG)
        s_own = _dot_nt(q, k_refs[half + 1][:, _kv_cols(h)])
        scores.append(jnp.concatenate([s_prev, s_own], axis=1) + tbl_ref[h])
    return scores


def _attn_c_finish(scores, sink_ref, v_refs, o_ref, row0):
    probs, denoms = [], []
    for (h, half), s in zip(C_JOBS, scores):
        sink = sink_ref[h] * LOG2E
        mx = jnp.maximum(jnp.max(s, axis=-1, keepdims=True), sink)
        p = jnp.exp2(s - mx)
        denoms.append(jnp.sum(p, axis=-1, keepdims=True) + jnp.exp2(sink - mx))
        probs.append(p.astype(BF16))
    for (h, half), pb, denom in zip(C_JOBS, probs, denoms):
        o = (_dot(pb[:, 0:C_KR], v_refs[half][:, _kv_cols(h)])
             + _dot(pb[:, C_KR:2 * C_KR], v_refs[half + 1][:, _kv_cols(h)]))
        o_ref[row0 + half * C_KR:row0 + (half + 1) * C_KR, h * D_C:(h + 1) * D_C] = (
            o / denom).astype(BF16)


def _attn_c_table():
    tq = np.arange(C_KR)[:, None] + C_KR
    tk = np.arange(2 * C_KR)[None, :]
    delta = tq // CHUNK - tk // CHUNK
    vis = (delta >= 0) & (delta <= C_PREV)
    slopes = 2.0 ** (-8.0 * np.arange(1, H_C + 1) / H_C)
    alibi = -slopes[:, None, None] * np.abs(tq - tk)[None].astype(np.float64)
    tbl = np.where(vis[None], LOG2E * alibi, NEG_BIG)
    return jnp.asarray(tbl, F32)


MERGE_WR = 512
MIX_SUB = 2


def _mix_merge_kernel(layer, sink_ref, x_ref, oa_ref, qb_ref, *rest):
    kper = ATT_R // C_KR
    nkb, nkc = MIX_SUB + B_KBLK - 1, kper * MIX_SUB + C_KBLK - 2
    kb_refs, vb_refs = rest[0:nkb], rest[nkb:2 * nkb]
    bias_ref, qc_ref = rest[2 * nkb:2 * nkb + 2]
    rest = rest[2 * nkb + 2:]
    kc_refs, vc_refs = rest[0:nkc], rest[nkc:2 * nkc]
    (tbl_ref, wg_hbm, bg_ref, wa_hbm, wb_hbm, wc_hbm, wo_hbm, g_ref, b_ref, o_ref,
     wg_ref, wa_ref, wb_ref, wc_ref, wo_ref, stage, sem, ob_ref, oc_ref) = rest[2 * nkc:]
    i = pl.program_id(1)

    @pl.when((pl.program_id(0) == 0) & (i == 0))
    def _():
        r = MERGE_WR
        pairs = [(wg_hbm.at[layer, pl.ds(m * r, r), pl.ds(j * D_MODEL, D_MODEL)],
                  wg_ref.at[pl.ds(m * r, r), pl.ds(j * D_MODEL, D_MODEL)])
                 for m in range(D_MODEL // r) for j in range(N_BRANCH)]
        pairs += [(src.at[layer], dst) for src, dst in
                  ((wa_hbm, wa_ref), (wb_hbm, wb_ref), (wc_hbm, wc_ref))]
        pairs += [(wo_hbm.at[layer, pl.ds(m * r, r), :], wo_ref.at[pl.ds(m * r, r), :])
                  for m in range(D_MODEL // r)]
        _stage_cast(pairs, stage, sem)

    for s in range(MIX_SUB):
        blk = i * MIX_SUB + s
        row0 = s * ATT_R
        rows = slice(row0, row0 + ATT_R)
        scores_b = _attn_b_scores(blk, qb_ref, kb_refs[s:s + B_KBLK], bias_ref, row0)
        scores_c = _attn_c_scores(blk, qc_ref, kc_refs[kper * s:kper * s + C_KBLK], tbl_ref, row0)
        x = x_ref[rows, :]
        xb = x.astype(BF16)
        gates = [jax.nn.sigmoid(_dot(xb, wg_ref[:, r * D_MODEL:(r + 1) * D_MODEL])
                                + bg_ref[:, r * D_MODEL:(r + 1) * D_MODEL]) for r in range(N_BRANCH)]
        merged = gates[0] * _dot(oa_ref[rows, :], wa_ref[...])
        _attn_b_finish(scores_b, vb_refs[s:s + B_KBLK], ob_ref, row0)
        _attn_c_finish(scores_c, sink_ref, vc_refs[kper * s:kper * s + C_KBLK], oc_ref, row0)
        merged = merged + gates[1] * _dot(ob_ref[rows, :], wb_ref[...])
        merged = merged + gates[2] * _dot(oc_ref[rows, :], wc_ref[...])
        y = ALPHA * x + _dot(merged.astype(BF16), wo_ref[...])
        o_ref[rows, :] = _layer_norm(y, g_ref[...], b_ref[...])


def _mix_merge(x, oa, qkv_b, qkv_c, bias, sinks, w_gate, b_gate, w_a, w_b, w_c, w_out, g, b,
               batch, seq, layer):
    n = x.shape[0]
    assert A_V == B_W == C_Q == MERGE_WR
    tm = MIX_SUB * ATT_R
    nstep = seq // tm
    nblk = seq // ATT_R
    kper = ATT_R // C_KR
    kcol, vcol = C_Q // C_KV, C_Q // C_KV + 1
    row = lambda w: pl.BlockSpec((tm, w), lambda bb, i: (bb * nstep + i, 0))
    const = lambda *shape: pl.BlockSpec(shape, lambda bb, i: (0,) * len(shape))
    hbm = pl.BlockSpec(memory_space=pl.ANY)

    def kv_b(j, colblk):
        return pl.BlockSpec(
            (ATT_R, B_W),
            lambda bb, i: (bb * nblk + jnp.maximum(i * MIX_SUB + j - (B_KBLK - 1), 0), colblk))

    def kv_c(j, colblk):
        return pl.BlockSpec(
            (C_KR, C_KV),
            lambda bb, i: (bb * nblk * kper + jnp.maximum(i * MIX_SUB * kper + j - 1, 0), colblk))

    nkb, nkc = MIX_SUB + B_KBLK - 1, kper * MIX_SUB + C_KBLK - 2
    in_specs = ([pl.BlockSpec(memory_space=pltpu.SMEM), row(D_MODEL), row(A_V), row(B_W)]
                + [kv_b(j, 1) for j in range(nkb)] + [kv_b(j, 2) for j in range(nkb)]
                + [pl.BlockSpec((H_B, 2, CHUNK, B_STRIP_W), lambda bb, i: (layer, 0, 0, 0)), row(C_Q)]
                + [kv_c(j, kcol) for j in range(nkc)] + [kv_c(j, vcol) for j in range(nkc)]
                + [const(H_C, C_KR, 2 * C_KR),
                   hbm, const(1, N_BRANCH * D_MODEL), hbm, hbm, hbm, hbm,
                   const(1, D_MODEL), const(1, D_MODEL)])
    return pl.pallas_call(
        functools.partial(_mix_merge_kernel, layer),
        grid=(batch, nstep),
        in_specs=in_specs,
        out_specs=row(D_MODEL),
        out_shape=jax.ShapeDtypeStruct((n, D_MODEL), F32),
        scratch_shapes=[pltpu.VMEM((D_MODEL, N_BRANCH * D_MODEL), BF16),
                        pltpu.VMEM((A_V, D_MODEL), BF16), pltpu.VMEM((B_W, D_MODEL), BF16),
                        pltpu.VMEM((C_Q, D_MODEL), BF16), pltpu.VMEM((D_MODEL, D_MODEL), BF16),
                        pltpu.VMEM((2, MERGE_WR, D_MODEL), F32), pltpu.SemaphoreType.DMA((2,)),
                        pltpu.VMEM((tm, B_W), BF16), pltpu.VMEM((tm, C_Q), BF16)],
        compiler_params=_params("arbitrary", "arbitrary"),
        name="mix_merge",
    )(sinks, x, oa, qkv_b, *([qkv_b] * (2 * nkb)), bias,
      qkv_c, *([qkv_c] * (2 * nkc)), _attn_c_table(),
      w_gate, b_gate, w_a, w_b, w_c, w_out, g, b)


def _lane_row(vals, offset):
    return jnp.zeros((1, LANES), F32).at[0, offset:offset + vals.shape[0]].set(vals.astype(F32))


def kernel(x, ln1_g, ln1_b, w_ff1_in, w_ff1_out, w_in, b_in, conv_w, a_log, dt_bias, gdn_norm_g,
           rel_bias, sinks, w_gate, b_gate, w_br_a, w_br_b, w_br_c, w_out, ln2_g, ln2_b,
           w_ff2_in, w_ff2_out, ln3_g, ln3_b):
    batch, seq, d = x.shape
    assert d == D_MODEL and seq % ATT_R == 0 and (batch * seq) % FFN_TM == 0
    h = x.reshape(batch * seq, d)
    row = lambda v: v.reshape(1, -1).astype(F32)
    o_small = QKV_A
    o_z = o_small + 2 * H_A
    o_b = o_z + A_V
    o_c = o_b + 3 * B_W
    bias_strips = _attn_b_bias(rel_bias.reshape(DEPTH * H_B, CHUNK + REL_CLIP))
    for l in range(DEPTH):
        h = _ffn(h, w_ff1_in, w_ff1_out, row(ln1_g[l]), row(ln1_b[l]), l)
        wi, bi = w_in[l], b_in[l]
        pad = SMALL_W - 2 * H_A
        wa = jnp.concatenate([wi[:, 0:QKV_A], wi[:, o_z:o_b]], axis=1).astype(BF16)
        ws = jnp.pad(wi[:, o_small:o_z], ((0, 0), (0, pad))).astype(BF16)
        ba = jnp.concatenate([bi[0:QKV_A], bi[o_z:o_b]])
        bs = jnp.pad(bi[o_small:o_z], (0, pad))
        o_a, pb, pc = _proj_gdn(h, wa, ws, wi[:, o_b:o_c].astype(BF16), wi[:, o_c:].astype(BF16),
                                row(ba), row(bs), row(bi[o_b:o_c]), row(bi[o_c:]),
                                conv_w[l].astype(F32), _lane_row(a_log[l], H_A),
                                _lane_row(dt_bias[l], H_A), row(gdn_norm_g[l]), batch, seq)
        h = _mix_merge(h, o_a, pb, pc, bias_strips, sinks[l].astype(F32), w_gate, row(b_gate[l]),
                       w_br_a, w_br_b, w_br_c, w_out, row(ln2_g[l]), row(ln2_b[l]), batch, seq, l)
        h = _ffn(h, w_ff2_in, w_ff2_out, row(ln3_g[l]), row(ln3_b[l]), l)
    return h.reshape(batch, seq, d)
```

```python
import functools
import math

import jax
import jax.numpy as jnp
import numpy as np
from jax import lax
from jax.experimental import pallas as pl
from jax.experimental.pallas import tpu as pltpu

F32 = jnp.float32
BF16 = jnp.bfloat16

D_MODEL = 1024
DEPTH = 2
CHUNK = 64
D_FF = 4096
LN_EPS = 1e-5
NORM_EPS = 1e-6
H_A, DK_A, DV_A, CONV_K = 4, 128, 128, 4
A_QK = H_A * DK_A
A_V = H_A * DV_A
H_B, D_B, B_PREV, REL_CLIP = 4, 128, 8, 128
B_W = H_B * D_B
H_C, HKV_C, D_C, WINDOW = 8, 2, 64, 128
G_C = H_C // HKV_C
C_PREV = WINDOW // CHUNK
C_Q = H_C * D_C
C_KV = HKV_C * D_C
N_BRANCH = 3
ALPHA = (2.0 * DEPTH) ** 0.25

LANES = 128
VMEM_LIMIT = 56 * 1024 * 1024
NEG_BIG = -1e30
LOG2E = math.log2(math.e)

SMALL_W = LANES
QKV_A = 2 * A_QK + A_V
A_W = QKV_A + A_V
C_W = C_Q + 2 * C_KV
D_IN = A_W + 2 * H_A + 3 * B_W + C_W


def _dot(a, b):
    return jnp.dot(a, b, preferred_element_type=F32)


def _dot_nt(a, b):
    return lax.dot_general(a, b, (((1,), (1,)), ((), ())), preferred_element_type=F32)


def _layer_norm(y, g, b):
    mu = jnp.mean(y, axis=-1, keepdims=True)
    yc = y - mu
    var = jnp.mean(yc * yc, axis=-1, keepdims=True)
    return yc * lax.rsqrt(var + LN_EPS) * g + b


def _silu_of_half(xh):
    return xh + xh * jnp.tanh(xh)


def _silu(x):
    return _silu_of_half(0.5 * x)


def _params(*sem):
    return pltpu.CompilerParams(dimension_semantics=sem, vmem_limit_bytes=VMEM_LIMIT)


FFN_TM = 512
FFN_SUB = 256
FFN_TF = 1024


FFN_WCH = 1024


def _stage_cast(pairs, stage, sem):
    copies = [pltpu.make_async_copy(src, stage.at[c % 2], sem.at[c % 2])
              for c, (src, _) in enumerate(pairs)]
    copies[0].start()
    for c, (_, dst) in enumerate(pairs):
        if c + 1 < len(pairs):
            copies[c + 1].start()
        copies[c].wait()
        if callable(dst):
            dst(stage[c % 2])
        else:
            dst[...] = stage[c % 2].astype(BF16)


def _ffn_kernel(layer, x_ref, wi_hbm, wo_hbm, g_ref, b_ref, o_ref, wi_ref, wo_ref, h_ref, stage, sem):
    @pl.when(pl.program_id(0) == 0)
    def _():
        w = FFN_WCH
        pairs = [(wi_hbm.at[layer, :, pl.ds(c * w, w)], wi_ref.at[:, pl.ds(c * w, w)])
                 for c in range(2 * D_FF // w)]
        pairs += [(wo_hbm.at[layer, pl.ds(c * w, w), :], wo_ref.at[pl.ds(c * w, w), :])
                  for c in range(D_FF // w)]
        _stage_cast(pairs, stage, sem)

    for s in range(FFN_TM // FFN_SUB):
        rows = slice(s * FFN_SUB, (s + 1) * FFN_SUB)
        x = x_ref[rows, :]
        xb = x.astype(BF16)
        for f in range(D_FF // FFN_TF):
            gate = _dot(xb, wi_ref[:, f * FFN_TF:(f + 1) * FFN_TF])
            up = _dot(xb, wi_ref[:, D_FF + f * FFN_TF:D_FF + (f + 1) * FFN_TF])
            h_ref[rows, f * FFN_TF:(f + 1) * FFN_TF] = (_silu(gate) * up).astype(BF16)
        y = ALPHA * x + 0.5 * _dot(h_ref[rows, :], wo_ref[...])
        o_ref[rows, :] = _layer_norm(y, g_ref[...], b_ref[...])


def _ffn(x, w_in, w_out, g, b, layer):
    n = x.shape[0]
    resident = lambda r, w: pl.BlockSpec((r, w), lambda i: (0, 0), pipeline_mode=pl.Buffered(1))
    return pl.pallas_call(
        functools.partial(_ffn_kernel, layer),
        grid=(n // FFN_TM,),
        in_specs=[
            pl.BlockSpec((FFN_TM, D_MODEL), lambda i: (i, 0)),
            pl.BlockSpec(memory_space=pl.ANY), pl.BlockSpec(memory_space=pl.ANY),
            resident(1, D_MODEL), resident(1, D_MODEL),
        ],
        out_specs=pl.BlockSpec((FFN_TM, D_MODEL), lambda i: (i, 0)),
        out_shape=jax.ShapeDtypeStruct((n, D_MODEL), F32),
        scratch_shapes=[pltpu.VMEM((D_MODEL, 2 * D_FF), BF16), pltpu.VMEM((D_FF, D_MODEL), BF16),
                        pltpu.VMEM((FFN_TM, D_FF), BF16),
                        pltpu.VMEM((2, FFN_WCH, FFN_WCH), F32), pltpu.SemaphoreType.DMA((2,))],
        compiler_params=_params("arbitrary"),
        name="ffn_ln",
    )(x, w_in, w_out, g, b)


GDN_R = 256
GDN_NC = GDN_R // CHUNK
GDN_PAIR = 2 * CHUNK
CONV_PAD = 8
LOG2_CHUNK = CHUNK.bit_length() - 1
NEUMANN_STEPS = LOG2_CHUNK - 1


PROJ_WR = 128


def _proj_gdn_kernel(layer, x_ref, win_hbm, bin_ref, cw_ref, alog_ref, dtb_ref, ng_ref,
                     o_ref, ob_ref, oc_ref,
                     wa_ref, ws_ref, wb_ref, wc_ref, stage, sem, a_ref, s_ref, xbuf, st_ref):
    o_small, o_z, o_b, o_c = QKV_A, QKV_A + 2 * H_A, QKV_A + 2 * H_A + A_V, QKV_A + 2 * H_A + A_V + 3 * B_W

    @pl.when((pl.program_id(0) == 0) & (pl.program_id(1) == 0))
    def _():
        ws_ref[...] = jnp.zeros_like(ws_ref)

        def regroup(r0):
            def store(chunk):
                rows = slice(r0, r0 + PROJ_WR)
                wa_ref[rows, 0:QKV_A] = chunk[:, 0:QKV_A].astype(BF16)
                wa_ref[rows, QKV_A:A_W] = chunk[:, o_z:o_b].astype(BF16)
                ws_ref[rows, 0:2 * H_A] = chunk[:, o_small:o_z].astype(BF16)
                wb_ref[rows, :] = chunk[:, o_b:o_c].astype(BF16)
                wc_ref[rows, :] = chunk[:, o_c:D_IN].astype(BF16)
            return store

        _stage_cast([(win_hbm.at[layer, pl.ds(r0, PROJ_WR), :], regroup(r0))
                     for r0 in range(0, D_MODEL, PROJ_WR)], stage, sem)

    @pl.when(pl.program_id(1) == 0)
    def _():
        st_ref[...] = jnp.zeros_like(st_ref)
        xbuf[...] = jnp.zeros_like(xbuf)

    b_in = bin_ref[...]
    b_small = jnp.concatenate(
        [b_in[:, o_small:o_z], jnp.zeros((1, SMALL_W - 2 * H_A), F32)], axis=1)
    xb = x_ref[...].astype(BF16)
    a_ref[...] = _dot(xb, wa_ref[...]) + jnp.concatenate([b_in[:, 0:QKV_A], b_in[:, o_z:o_b]], axis=1)
    s_ref[...] = _dot(xb, ws_ref[...]) + b_small
    yb = _dot(xb, wb_ref[...]) + b_in[:, o_b:o_c]
    ob_ref[:, 0:B_W] = (yb[:, 0:B_W] * (D_B ** -0.5 * LOG2E)).astype(BF16)
    ob_ref[:, B_W:3 * B_W] = yb[:, B_W:3 * B_W].astype(BF16)
    yc = _dot(xb, wc_ref[...]) + b_in[:, o_c:D_IN]
    oc_ref[:, 0:C_Q] = (yc[:, 0:C_Q] * (D_C ** -0.5 * LOG2E)).astype(BF16)
    oc_ref[:, C_Q:C_W] = yc[:, C_Q:C_W].astype(BF16)

    assert CONV_K == 4
    x_new = a_ref[:, 0:QKV_A]
    n_grp = GDN_R // CONV_PAD
    x3 = x_new.reshape(n_grp, CONV_PAD, QKV_A)
    hist = xbuf[...].reshape(1, CONV_PAD, QKV_A)
    sub = lax.broadcasted_iota(jnp.int32, (n_grp, CONV_PAD, QKV_A), 1)
    w0, w1, w2, w3 = [0.5 * cw_ref[j:j + 1, :].reshape(1, 1, QKV_A) for j in range(CONV_K)]

    def delay(v, v_hist, s):
        rot = pltpu.roll(v, s, axis=1)
        prev = jnp.concatenate([pltpu.roll(v_hist, s, axis=1), rot[0:n_grp - 1]], axis=0)
        return jnp.where(sub >= s, rot, prev)

    dx = delay(x3, hist, 1)
    inner = w1 * x3 + w0 * dx
    inner_hist = w1 * hist + w0 * pltpu.roll(hist, 1, axis=1)
    y_half = w3 * x3 + w2 * dx + delay(inner, inner_hist, 2)
    xbuf[...] = x_new[GDN_R - CONV_PAD:GDN_R]
    qkv = _silu_of_half(y_half.reshape(GDN_R, QKV_A))

    s = s_ref[...]
    beta_all = jax.nn.sigmoid(s)
    g_all = -jnp.exp(alog_ref[...]) * jax.nn.softplus(s + dtb_ref[...])
    row = lax.broadcasted_iota(jnp.int32, (GDN_PAIR, GDN_PAIR), 0)
    col = lax.broadcasted_iota(jnp.int32, (GDN_PAIR, GDN_PAIR), 1)
    same_chunk = jnp.right_shift(row, LOG2_CHUNK) == jnp.right_shift(col, LOG2_CHUNK)
    causal = same_chunk & (col <= row)
    strict = same_chunk & (col < row)
    eye = jnp.where(row == col, 1.0, 0.0).astype(F32)
    pairs = [slice(p * GDN_PAIR, (p + 1) * GDN_PAIR) for p in range(GDN_R // GDN_PAIR)]
    g_hi = g_all.astype(BF16)
    g_r1 = g_all - g_hi.astype(F32)
    g_mid = g_r1.astype(BF16)
    g_lo = (g_r1 - g_mid.astype(F32)).astype(BF16)
    g_split = jnp.concatenate([g_hi, g_mid, g_lo], axis=1)
    tri = jnp.where(causal, 1.0, 0.0).astype(BF16)
    g_parts = jnp.concatenate([_dot(tri, g_split[rows]) for rows in pairs], axis=0)
    gcum = g_parts[:, 0:LANES] + g_parts[:, LANES:2 * LANES] + g_parts[:, 2 * LANES:3 * LANES]
    gsum = jnp.concatenate(
        [jnp.broadcast_to(gcum[(c + 1) * CHUNK - 1:(c + 1) * CHUNK, :], (CHUNK, LANES))
         for c in range(GDN_NC)], axis=0)
    gcum_t = gcum.T
    ng = ng_ref[...]
    heads = range(H_A)

    jobs = [(h, pr) for h in heads for pr in range(len(pairs))]
    qn, rhs, glast, kdec = [], [], [], []
    kk, qk, decay = {}, {}, {}
    for h in heads:
        lo = h * DK_A
        qh = qkv[:, lo:lo + DK_A]
        kh = qkv[:, A_QK + lo:A_QK + lo + DK_A]
        vh = qkv[:, 2 * A_QK + lo:2 * A_QK + lo + DV_A]
        q_h = qh * (lax.rsqrt(jnp.sum(qh * qh, -1, keepdims=True) + NORM_EPS) * (DK_A ** -0.5))
        k_h = kh * lax.rsqrt(jnp.sum(kh * kh, -1, keepdims=True) + NORM_EPS)
        bcol = beta_all[:, h:h + 1]
        g_h = gcum[:, H_A + h:H_A + h + 1]
        gl_h = gsum[:, H_A + h:H_A + h + 1]
        eg = jnp.exp(g_h)
        kb = k_h * bcol
        k_b = k_h.astype(BF16)
        kb_b = kb.astype(BF16)
        q_b = q_h.astype(BF16)
        g_row = gcum_t[H_A + h:H_A + h + 1, :]
        for pr, rows in enumerate(pairs):
            job = (h, pr)
            kk[job] = _dot_nt(kb_b[rows], k_b[rows])
            qk[job] = _dot_nt(q_b[rows], k_b[rows])
            decay[job] = jnp.exp(jnp.where(causal, g_h[rows] - g_row[:, rows], -jnp.inf))
        rhs.append(jnp.concatenate([vh * bcol, kb * eg], axis=1).astype(BF16))
        qn.append(q_h * eg)
        kdec.append(k_h * jnp.exp(gl_h - g_h))
        glast.append(gl_h)

    p = {job: jnp.where(strict, -(kk[job] * decay[job]), 0.0) for job in jobs}
    inv = {job: eye + p[job] for job in jobs}
    for _ in range(NEUMANN_STEPS):
        pb = {job: p[job].astype(BF16) for job in jobs}
        p = {job: _dot(pb[job], pb[job]) for job in jobs}
        inv = {job: inv[job] + _dot(p[job].astype(BF16), inv[job].astype(BF16)) for job in jobs}
    uw_j = {(h, pr): _dot(inv[(h, pr)].astype(BF16), rhs[h][pairs[pr]]).astype(BF16)
            for h, pr in jobs}
    aw_j = {job: _dot((qk[job] * decay[job]).astype(BF16), uw_j[job]) for job in jobs}
    uw = [jnp.concatenate([uw_j[(h, pr)] for pr in range(len(pairs))], axis=0) for h in heads]
    aw = [jnp.concatenate([aw_j[(h, pr)] for pr in range(len(pairs))], axis=0) for h in heads]
    qt = [(qn[h] - aw[h][:, DV_A:DV_A + DK_A]).astype(BF16) for h in heads]

    bp = [[_dot(kdec[h][c * CHUNK:(c + 1) * CHUNK].T.astype(BF16), uw[h][c * CHUNK:(c + 1) * CHUNK])
           for c in range(GDN_NC)] for h in heads]
    state = [st_ref[h] for h in heads]
    for c in range(GDN_NC):
        r0 = c * CHUNK
        for h in heads:
            lo = h * DV_A
            s_b = state[h].astype(BF16)
            o = _dot(qt[h][r0:r0 + CHUNK], s_b) + aw[h][r0:r0 + CHUNK, 0:DV_A]
            decay_c = jnp.exp(glast[h][r0:r0 + 1])
            state[h] = (state[h] * decay_c + bp[h][c][:, 0:DV_A]
                        - _dot(bp[h][c][:, DV_A:DV_A + DK_A].astype(BF16), s_b))
            z = a_ref[r0:r0 + CHUNK, QKV_A + lo:QKV_A + lo + DV_A]
            on = o * lax.rsqrt(jnp.mean(o * o, -1, keepdims=True) + NORM_EPS) * ng * _silu(z)
            o_ref[r0:r0 + CHUNK, lo:lo + DV_A] = on.astype(BF16)
    for h in heads:
        st_ref[h] = state[h]


def _proj_gdn(x, w_in, b_in, conv_w, alog_row, dtb_row, norm_g, batch, seq, layer):
    n = x.shape[0]
    assert w_in.shape[1:] == (D_MODEL, D_IN) and D_MODEL % PROJ_WR == 0
    nblk = seq // GDN_R
    row = lambda w: pl.BlockSpec((GDN_R, w), lambda b, i: (b * nblk + i, 0))
    const = lambda r, w: pl.BlockSpec((r, w), lambda b, i: (0, 0))
    return pl.pallas_call(
        functools.partial(_proj_gdn_kernel, layer),
        grid=(batch, nblk),
        in_specs=[row(D_MODEL), pl.BlockSpec(memory_space=pl.ANY), const(1, D_IN),
                  const(CONV_K, QKV_A), const(1, SMALL_W), const(1, SMALL_W), const(1, DV_A)],
        out_specs=[row(A_V), row(3 * B_W), row(C_W)],
        out_shape=[jax.ShapeDtypeStruct((n, A_V), BF16), jax.ShapeDtypeStruct((n, 3 * B_W), BF16),
                   jax.ShapeDtypeStruct((n, C_W), BF16)],
        scratch_shapes=[pltpu.VMEM((D_MODEL, A_W), BF16), pltpu.VMEM((D_MODEL, SMALL_W), BF16),
                        pltpu.VMEM((D_MODEL, 3 * B_W), BF16), pltpu.VMEM((D_MODEL, C_W), BF16),
                        pltpu.VMEM((2, PROJ_WR, D_IN), F32), pltpu.SemaphoreType.DMA((2,)),
                        pltpu.VMEM((GDN_R, A_W), F32), pltpu.VMEM((GDN_R, SMALL_W), F32),
                        pltpu.VMEM((CONV_PAD, QKV_A), F32), pltpu.VMEM((H_A, DK_A, DV_A), F32)],
        compiler_params=_params("arbitrary", "arbitrary"),
        name="proj_gdn",
    )(x, w_in, b_in, conv_w, alog_row, dtb_row, norm_g)


ATT_R = 256
ATT_CH = ATT_R // CHUNK
B_KBLK = 3
assert (B_KBLK - 1) * ATT_CH == B_PREV
B_STRIP_W = B_KBLK * ATT_R + (ATT_CH - 2) * CHUNK
B_QR = 2 * CHUNK
B_KW = (B_PREV + 2) * CHUNK
B_BIAS0 = (ATT_CH - 2) * CHUNK


def _attn_b_pieces():
    pieces = []
    for half in range(ATT_R // B_QR):
        lo, cut = half * B_QR, []
        while lo < half * B_QR + B_KW:
            hi = min((lo // ATT_R + 1) * ATT_R, half * B_QR + B_KW)
            cut.append((lo // ATT_R, lo % ATT_R, hi - lo))
            lo = hi
        pieces.append(cut)
    return pieces


B_JOBS = [(h, half) for h in range(H_B) for half in range(ATT_R // B_QR)]


def _attn_b_scores(i, q_ref, k_refs, bias_ref, row0):
    pieces = _attn_b_pieces()
    scores = []
    for h, half in B_JOBS:
        cols = slice(h * D_B, (h + 1) * D_B)
        q = q_ref[row0 + half * B_QR:row0 + (half + 1) * B_QR, cols]
        parts = []
        for j, r0, nr in pieces[half]:
            sj = _dot_nt(q, k_refs[j][r0:r0 + nr, cols])
            if j < B_KBLK - 1:
                sj = jnp.where(i + j >= B_KBLK - 1, sj, NEG_BIG)
            parts.append(sj)
        bias = jnp.concatenate([bias_ref[h, 1, :, B_BIAS0:B_BIAS0 + B_KW],
                                bias_ref[h, 0, :, B_BIAS0:B_BIAS0 + B_KW]], axis=0)
        scores.append(jnp.concatenate(parts, axis=1) + bias)
    return scores


def _attn_b_finish(scores, v_refs, o_ref, row0):
    pieces = _attn_b_pieces()
    probs, denoms = [], []
    for s in scores:
        p = jnp.exp2(s - jnp.max(s, axis=-1, keepdims=True))
        denoms.append(jnp.sum(p, axis=-1, keepdims=True))
        probs.append(p.astype(BF16))
    for (h, half), pb, denom in zip(B_JOBS, probs, denoms):
        cols = slice(h * D_B, (h + 1) * D_B)
        o, c0 = None, 0
        for j, r0, nr in pieces[half]:
            term = _dot(pb[:, c0:c0 + nr], v_refs[j][r0:r0 + nr, cols])
            o = term if o is None else o + term
            c0 += nr
        o_ref[row0 + half * B_QR:row0 + (half + 1) * B_QR, cols] = (o / denom).astype(BF16)


def _attn_b_bias(rel_bias):
    nh, nrel = rel_bias.shape
    assert nrel == CHUNK + REL_CLIP
    lead = (ATT_CH - 1) * CHUNK
    vis_end = lead + (B_PREV + 1) * CHUNK
    width = B_KBLK * ATT_R + lead
    period = 1 << (width + CHUNK - 1).bit_length()
    rb = rel_bias.astype(F32)
    far = jnp.broadcast_to(rb[:, -1:], (nh, period))
    f = jnp.concatenate([far[:, :vis_end - nrel], jnp.flip(rb, axis=1), far[:, :period - vis_end]], axis=1)
    g = jnp.concatenate([f[:, :1], jnp.flip(f[:, 1:], axis=1)], axis=1)
    skew = jnp.tile(g, (1, CHUNK + 1))[:, :CHUNK * (period + 1)].reshape(nh, CHUNK, period + 1)
    toep = LOG2E * jnp.flip(skew[:, :, period - width + 1:period + 1], axis=2)
    c = np.arange(width)
    strip = jnp.where(jnp.asarray((c >= lead) & (c < vis_end))[None, None, :], toep, NEG_BIG)
    return jnp.stack([strip[:, :, 0:B_STRIP_W], strip[:, :, CHUNK:CHUNK + B_STRIP_W]], axis=1)


C_KR = 2 * CHUNK
C_KBLK = 3


C_JOBS = [(h, half) for h in range(H_C) for half in range(ATT_R // C_KR)]


def _kv_cols(h):
    return slice((h // G_C) * D_C, (h // G_C + 1) * D_C)


def _attn_c_scores(i, q_ref, k_refs, tbl_ref, row0):
    scores = []
    for h, half in C_JOBS:
        q = q_ref[row0 + half * C_KR:row0 + (half + 1) * C_KR, h * D_C:(h + 1) * D_C]
        s_prev = _dot_nt(q, k_refs[half][:, _kv_cols(h)])
        if half == 0:
            s_prev = jnp.where(i > 0, s_prev, NEG_BIG)
        s_own = _dot_nt(q, k_refs[half + 1][:, _kv_cols(h)])
        scores.append(jnp.concatenate([s_prev, s_own], axis=1) + tbl_ref[h])
    return scores


def _attn_c_finish(scores, sink_ref, v_refs, o_ref, row0):
    probs, denoms = [], []
    for (h, half), s in zip(C_JOBS, scores):
        sink = sink_ref[h] * LOG2E
        mx = jnp.maximum(jnp.max(s, axis=-1, keepdims=True), sink)
        p = jnp.exp2(s - mx)
        denoms.append(jnp.sum(p, axis=-1, keepdims=True) + jnp.exp2(sink - mx))
        probs.append(p.astype(BF16))
    for (h, half), pb, denom in zip(C_JOBS, probs, denoms):
        o = (_dot(pb[:, 0:C_KR], v_refs[half][:, _kv_cols(h)])
             + _dot(pb[:, C_KR:2 * C_KR], v_refs[half + 1][:, _kv_cols(h)]))
        o_ref[row0 + half * C_KR:row0 + (half + 1) * C_KR, h * D_C:(h + 1) * D_C] = (
            o / denom).astype(BF16)


def _attn_c_table():
    tq = np.arange(C_KR)[:, None] + C_KR
    tk = np.arange(2 * C_KR)[None, :]
    delta = tq // CHUNK - tk // CHUNK
    vis = (delta >= 0) & (delta <= C_PREV)
    slopes = 2.0 ** (-8.0 * np.arange(1, H_C + 1) / H_C)
    alibi = -slopes[:, None, None] * np.abs(tq - tk)[None].astype(np.float64)
    tbl = np.where(vis[None], LOG2E * alibi, NEG_BIG)
    return jnp.asarray(tbl, F32)


MERGE_WR = 512
MIX_SUB = 2


def _mix_merge_kernel(layer, sink_ref, x_ref, oa_ref, qb_ref, *rest):
    kper = ATT_R // C_KR
    nkb, nkc = MIX_SUB + B_KBLK - 1, kper * MIX_SUB + C_KBLK - 2
    kb_refs, vb_refs = rest[0:nkb], rest[nkb:2 * nkb]
    bias_ref, qc_ref = rest[2 * nkb:2 * nkb + 2]
    rest = rest[2 * nkb + 2:]
    kc_refs, vc_refs = rest[0:nkc], rest[nkc:2 * nkc]
    (tbl_ref, wg_hbm, bg_ref, wa_hbm, wb_hbm, wc_hbm, wo_hbm, g_ref, b_ref, o_ref,
     wg_ref, wa_ref, wb_ref, wc_ref, wo_ref, stage, sem, ob_ref, oc_ref) = rest[2 * nkc:]
    i = pl.program_id(1)

    @pl.when((pl.program_id(0) == 0) & (i == 0))
    def _():
        r = MERGE_WR
        pairs = [(wg_hbm.at[layer, pl.ds(m * r, r), pl.ds(j * D_MODEL, D_MODEL)],
                  wg_ref.at[pl.ds(m * r, r), pl.ds(j * D_MODEL, D_MODEL)])
                 for m in range(D_MODEL // r) for j in range(N_BRANCH)]
        pairs += [(src.at[layer], dst) for src, dst in
                  ((wa_hbm, wa_ref), (wb_hbm, wb_ref), (wc_hbm, wc_ref))]
        pairs += [(wo_hbm.at[layer, pl.ds(m * r, r), :], wo_ref.at[pl.ds(m * r, r), :])
                  for m in range(D_MODEL // r)]
        _stage_cast(pairs, stage, sem)

    st = [dict() for _ in range(MIX_SUB)]

    def qk(s):
        blk = i * MIX_SUB + s
        st[s]["b"] = _attn_b_scores(blk, qb_ref, kb_refs[s:s + B_KBLK], bias_ref, s * ATT_R)
        st[s]["c"] = _attn_c_scores(blk, qc_ref, kc_refs[kper * s:kper * s + C_KBLK], tbl_ref,
                                    s * ATT_R)

    def gates(s):
        rows = slice(s * ATT_R, (s + 1) * ATT_R)
        x = x_ref[rows, :]
        xb = x.astype(BF16)
        st[s]["x"] = x
        st[s]["g"] = [jax.nn.sigmoid(_dot(xb, wg_ref[:, r * D_MODEL:(r + 1) * D_MODEL])
                                     + bg_ref[:, r * D_MODEL:(r + 1) * D_MODEL])
                      for r in range(N_BRANCH)]
        st[s]["m"] = st[s]["g"][0] * _dot(oa_ref[rows, :], wa_ref[...])

    def attend(s):
        _attn_b_finish(st[s]["b"], vb_refs[s:s + B_KBLK], ob_ref, s * ATT_R)
        _attn_c_finish(st[s]["c"], sink_ref, vc_refs[kper * s:kper * s + C_KBLK], oc_ref, s * ATT_R)

    def finish(s):
        rows = slice(s * ATT_R, (s + 1) * ATT_R)
        merged = st[s]["m"] + st[s]["g"][1] * _dot(ob_ref[rows, :], wb_ref[...])
        merged = merged + st[s]["g"][2] * _dot(oc_ref[rows, :], wc_ref[...])
        y = ALPHA * st[s]["x"] + _dot(merged.astype(BF16), wo_ref[...])
        o_ref[rows, :] = _layer_norm(y, g_ref[...], b_ref[...])

    qk(0)
    gates(0)
    for s in range(MIX_SUB):
        if s + 1 < MIX_SUB:
            qk(s + 1)
        attend(s)
        if s + 1 < MIX_SUB:
            gates(s + 1)
        finish(s)


def _mix_merge(x, oa, qkv_b, qkv_c, bias, sinks, w_gate, b_gate, w_a, w_b, w_c, w_out, g, b,
               batch, seq, layer):
    n = x.shape[0]
    assert A_V == B_W == C_Q == MERGE_WR
    tm = MIX_SUB * ATT_R
    nstep = seq // tm
    nblk = seq // ATT_R
    kper = ATT_R // C_KR
    kcol, vcol = C_Q // C_KV, C_Q // C_KV + 1
    row = lambda w: pl.BlockSpec((tm, w), lambda bb, i: (bb * nstep + i, 0))
    const = lambda *shape: pl.BlockSpec(shape, lambda bb, i: (0,) * len(shape))
    hbm = pl.BlockSpec(memory_space=pl.ANY)

    def kv_b(j, colblk):
        return pl.BlockSpec(
            (ATT_R, B_W),
            lambda bb, i: (bb * nblk + jnp.maximum(i * MIX_SUB + j - (B_KBLK - 1), 0), colblk))

    def kv_c(j, colblk):
        return pl.BlockSpec(
            (C_KR, C_KV),
            lambda bb, i: (bb * nblk * kper + jnp.maximum(i * MIX_SUB * kper + j - 1, 0), colblk))

    nkb, nkc = MIX_SUB + B_KBLK - 1, kper * MIX_SUB + C_KBLK - 2
    in_specs = ([pl.BlockSpec(memory_space=pltpu.SMEM), row(D_MODEL), row(A_V), row(B_W)]
                + [kv_b(j, 1) for j in range(nkb)] + [kv_b(j, 2) for j in range(nkb)]
                + [pl.BlockSpec((H_B, 2, CHUNK, B_STRIP_W), lambda bb, i: (layer, 0, 0, 0)), row(C_Q)]
                + [kv_c(j, kcol) for j in range(nkc)] + [kv_c(j, vcol) for j in range(nkc)]
                + [const(H_C, C_KR, 2 * C_KR),
                   hbm, const(1, N_BRANCH * D_MODEL), hbm, hbm, hbm, hbm,
                   const(1, D_MODEL), const(1, D_MODEL)])
    return pl.pallas_call(
        functools.partial(_mix_merge_kernel, layer),
        grid=(batch, nstep),
        in_specs=in_specs,
        out_specs=row(D_MODEL),
        out_shape=jax.ShapeDtypeStruct((n, D_MODEL), F32),
        scratch_shapes=[pltpu.VMEM((D_MODEL, N_BRANCH * D_MODEL), BF16),
                        pltpu.VMEM((A_V, D_MODEL), BF16), pltpu.VMEM((B_W, D_MODEL), BF16),
                        pltpu.VMEM((C_Q, D_MODEL), BF16), pltpu.VMEM((D_MODEL, D_MODEL), BF16),
                        pltpu.VMEM((2, MERGE_WR, D_MODEL), F32), pltpu.SemaphoreType.DMA((2,)),
                        pltpu.VMEM((tm, B_W), BF16), pltpu.VMEM((tm, C_Q), BF16)],
        compiler_params=_params("arbitrary", "arbitrary"),
        name="mix_merge",
    )(sinks, x, oa, qkv_b, *([qkv_b] * (2 * nkb)), bias,
      qkv_c, *([qkv_c] * (2 * nkc)), _attn_c_table(),
      w_gate, b_gate, w_a, w_b, w_c, w_out, g, b)


def _lane_row(vals, offset):
    return jnp.pad(vals.astype(F32), (offset, LANES - offset - vals.shape[0])).reshape(1, LANES)


def kernel(x, ln1_g, ln1_b, w_ff1_in, w_ff1_out, w_in, b_in, conv_w, a_log, dt_bias, gdn_norm_g,
           rel_bias, sinks, w_gate, b_gate, w_br_a, w_br_b, w_br_c, w_out, ln2_g, ln2_b,
           w_ff2_in, w_ff2_out, ln3_g, ln3_b):
    batch, seq, d = x.shape
    assert d == D_MODEL and seq % ATT_R == 0 and (batch * seq) % FFN_TM == 0
    h = x.reshape(batch * seq, d)
    row = lambda v: v.reshape(1, -1).astype(F32)
    bias_strips = _attn_b_bias(rel_bias.reshape(DEPTH * H_B, CHUNK + REL_CLIP))
    for l in range(DEPTH):
        h = _ffn(h, w_ff1_in, w_ff1_out, row(ln1_g[l]), row(ln1_b[l]), l)
        o_a, pb, pc = _proj_gdn(h, w_in, row(b_in[l]), conv_w[l].astype(F32),
                                _lane_row(a_log[l], H_A), _lane_row(dt_bias[l], H_A),
                                row(gdn_norm_g[l]), batch, seq, l)
        h = _mix_merge(h, o_a, pb, pc, bias_strips, sinks[l].astype(F32), w_gate, row(b_gate[l]),
                       w_br_a, w_br_b, w_br_c, w_out, row(ln2_g[l]), row(ln2_b[l]), batch, seq, l)
        h = _ffn(h, w_ff2_in, w_ff2_out, row(ln3_g[l]), row(ln3_b[l]), l)
    return h.reshape(batch, seq, d)
```

```python
import functools
import math

import jax
import jax.numpy as jnp
import numpy as np
from jax import lax
from jax.experimental import pallas as pl
from jax.experimental.pallas import tpu as pltpu

F32 = jnp.float32
BF16 = jnp.bfloat16

D_MODEL = 1024
DEPTH = 2
CHUNK = 64
D_FF = 4096
LN_EPS = 1e-5
NORM_EPS = 1e-6
H_A, DK_A, DV_A, CONV_K = 4, 128, 128, 4
A_QK = H_A * DK_A
A_V = H_A * DV_A
H_B, D_B, B_PREV, REL_CLIP = 4, 128, 8, 128
B_W = H_B * D_B
H_C, HKV_C, D_C, WINDOW = 8, 2, 64, 128
G_C = H_C // HKV_C
C_PREV = WINDOW // CHUNK
C_Q = H_C * D_C
C_KV = HKV_C * D_C
N_BRANCH = 3
ALPHA = (2.0 * DEPTH) ** 0.25

LANES = 128
VMEM_LIMIT = 56 * 1024 * 1024
NEG_BIG = -1e30
LOG2E = math.log2(math.e)

SMALL_W = LANES
QKV_A = 2 * A_QK + A_V
A_W = QKV_A + A_V
C_W = C_Q + 2 * C_KV


def _dot(a, b):
    return jnp.dot(a, b, preferred_element_type=F32)


def _dot_nt(a, b):
    return lax.dot_general(a, b, (((1,), (1,)), ((), ())), preferred_element_type=F32)


def _layer_norm(y, g, b):
    mu = jnp.mean(y, axis=-1, keepdims=True)
    yc = y - mu
    var = jnp.mean(yc * yc, axis=-1, keepdims=True)
    return yc * lax.rsqrt(var + LN_EPS) * g + b


def _silu_of_half(xh):
    return xh + xh * jnp.tanh(xh)


def _silu(x):
    return _silu_of_half(0.5 * x)


def _params(*sem):
    return pltpu.CompilerParams(dimension_semantics=sem, vmem_limit_bytes=VMEM_LIMIT)


FFN_TM = 512
FFN_SUB = 256
FFN_TF = 1024


FFN_WCH = 1024


def _stage_cast(pairs, stage, sem):
    copies = [pltpu.make_async_copy(src, stage.at[c % 2], sem.at[c % 2])
              for c, (src, _) in enumerate(pairs)]
    copies[0].start()
    for c, (_, dst) in enumerate(pairs):
        if c + 1 < len(pairs):
            copies[c + 1].start()
        copies[c].wait()
        dst[...] = stage[c % 2].astype(BF16)


def _ffn_kernel(layer, x_ref, wi_hbm, wo_hbm, g_ref, b_ref, o_ref, wi_ref, wo_ref, h_ref, stage, sem):
    @pl.when(pl.program_id(0) == 0)
    def _():
        w = FFN_WCH
        pairs = [(wi_hbm.at[layer, :, pl.ds(c * w, w)], wi_ref.at[:, pl.ds(c * w, w)])
                 for c in range(2 * D_FF // w)]
        pairs += [(wo_hbm.at[layer, pl.ds(c * w, w), :], wo_ref.at[pl.ds(c * w, w), :])
                  for c in range(D_FF // w)]
        _stage_cast(pairs, stage, sem)

    for s in range(FFN_TM // FFN_SUB):
        rows = slice(s * FFN_SUB, (s + 1) * FFN_SUB)
        x = x_ref[rows, :]
        xb = x.astype(BF16)
        for f in range(D_FF // FFN_TF):
            gate = _dot(xb, wi_ref[:, f * FFN_TF:(f + 1) * FFN_TF])
            up = _dot(xb, wi_ref[:, D_FF + f * FFN_TF:D_FF + (f + 1) * FFN_TF])
            h_ref[rows, f * FFN_TF:(f + 1) * FFN_TF] = (_silu(gate) * up).astype(BF16)
        y = ALPHA * x + 0.5 * _dot(h_ref[rows, :], wo_ref[...])
        o_ref[rows, :] = _layer_norm(y, g_ref[...], b_ref[...])


def _ffn(x, w_in, w_out, g, b, layer):
    n = x.shape[0]
    resident = lambda r, w: pl.BlockSpec((r, w), lambda i: (0, 0), pipeline_mode=pl.Buffered(1))
    return pl.pallas_call(
        functools.partial(_ffn_kernel, layer),
        grid=(n // FFN_TM,),
        in_specs=[
            pl.BlockSpec((FFN_TM, D_MODEL), lambda i: (i, 0)),
            pl.BlockSpec(memory_space=pl.ANY), pl.BlockSpec(memory_space=pl.ANY),
            resident(1, D_MODEL), resident(1, D_MODEL),
        ],
        out_specs=pl.BlockSpec((FFN_TM, D_MODEL), lambda i: (i, 0)),
        out_shape=jax.ShapeDtypeStruct((n, D_MODEL), F32),
        scratch_shapes=[pltpu.VMEM((D_MODEL, 2 * D_FF), BF16), pltpu.VMEM((D_FF, D_MODEL), BF16),
                        pltpu.VMEM((FFN_TM, D_FF), BF16),
                        pltpu.VMEM((2, FFN_WCH, FFN_WCH), F32), pltpu.SemaphoreType.DMA((2,))],
        compiler_params=_params("arbitrary"),
        name="ffn_ln",
    )(x, w_in, w_out, g, b)


GDN_R = 256
GDN_NC = GDN_R // CHUNK
GDN_PAIR = 2 * CHUNK
CONV_PAD = 8
LOG2_CHUNK = CHUNK.bit_length() - 1
NEUMANN_STEPS = LOG2_CHUNK - 1


def _proj_gdn_kernel(x_ref, wa_ref, ws_ref, wb_ref, wc_ref, ba_ref, bs_ref, bb_ref, bc_ref,
                     cw_ref, alog_ref, dtb_ref, ng_ref, o_ref, ob_ref, oc_ref,
                     a_ref, s_ref, xbuf, st_ref):
    @pl.when(pl.program_id(1) == 0)
    def _():
        st_ref[...] = jnp.zeros_like(st_ref)
        xbuf[...] = jnp.zeros_like(xbuf)

    xb = x_ref[...].astype(BF16)
    a_ref[...] = _dot(xb, wa_ref[...]) + ba_ref[...]
    s_ref[...] = _dot(xb, ws_ref[...]) + bs_ref[...]
    yb = _dot(xb, wb_ref[...]) + bb_ref[...]
    ob_ref[:, 0:B_W] = (yb[:, 0:B_W] * (D_B ** -0.5 * LOG2E)).astype(BF16)
    ob_ref[:, B_W:3 * B_W] = yb[:, B_W:3 * B_W].astype(BF16)
    yc = _dot(xb, wc_ref[...]) + bc_ref[...]
    oc_ref[:, 0:C_Q] = (yc[:, 0:C_Q] * (D_C ** -0.5 * LOG2E)).astype(BF16)
    oc_ref[:, C_Q:C_W] = yc[:, C_Q:C_W].astype(BF16)

    assert CONV_K == 4
    x_new = a_ref[:, 0:QKV_A]
    n_grp = GDN_R // CONV_PAD
    x3 = x_new.reshape(n_grp, CONV_PAD, QKV_A)
    hist = xbuf[...].reshape(1, CONV_PAD, QKV_A)
    sub = lax.broadcasted_iota(jnp.int32, (n_grp, CONV_PAD, QKV_A), 1)
    w0, w1, w2, w3 = [0.5 * cw_ref[j:j + 1, :].reshape(1, 1, QKV_A) for j in range(CONV_K)]

    def delay(v, v_hist, s):
        rot = pltpu.roll(v, s, axis=1)
        prev = jnp.concatenate([pltpu.roll(v_hist, s, axis=1), rot[0:n_grp - 1]], axis=0)
        return jnp.where(sub >= s, rot, prev)

    dx = delay(x3, hist, 1)
    inner = w1 * x3 + w0 * dx
    inner_hist = w1 * hist + w0 * pltpu.roll(hist, 1, axis=1)
    y_half = w3 * x3 + w2 * dx + delay(inner, inner_hist, 2)
    xbuf[...] = x_new[GDN_R - CONV_PAD:GDN_R]
    qkv = _silu_of_half(y_half.reshape(GDN_R, QKV_A))

    s = s_ref[...]
    beta_all = jax.nn.sigmoid(s)
    g_all = -jnp.exp(alog_ref[...]) * jax.nn.softplus(s + dtb_ref[...])
    row = lax.broadcasted_iota(jnp.int32, (GDN_PAIR, GDN_PAIR), 0)
    col = lax.broadcasted_iota(jnp.int32, (GDN_PAIR, GDN_PAIR), 1)
    same_chunk = jnp.right_shift(row, LOG2_CHUNK) == jnp.right_shift(col, LOG2_CHUNK)
    causal = same_chunk & (col <= row)
    strict = same_chunk & (col < row)
    eye = jnp.where(row == col, 1.0, 0.0).astype(F32)
    pairs = [slice(p * GDN_PAIR, (p + 1) * GDN_PAIR) for p in range(GDN_R // GDN_PAIR)]
    g_hi = g_all.astype(BF16)
    g_r1 = g_all - g_hi.astype(F32)
    g_mid = g_r1.astype(BF16)
    g_lo = (g_r1 - g_mid.astype(F32)).astype(BF16)
    g_split = jnp.concatenate([g_hi, g_mid, g_lo], axis=1)
    tri = jnp.where(causal, 1.0, 0.0).astype(BF16)
    g_parts = jnp.concatenate([_dot(tri, g_split[rows]) for rows in pairs], axis=0)
    gcum = g_parts[:, 0:LANES] + g_parts[:, LANES:2 * LANES] + g_parts[:, 2 * LANES:3 * LANES]
    gsum = jnp.concatenate(
        [jnp.broadcast_to(gcum[(c + 1) * CHUNK - 1:(c + 1) * CHUNK, :], (CHUNK, LANES))
         for c in range(GDN_NC)], axis=0)
    gcum_t = gcum.T
    ng = ng_ref[...]
    heads = range(H_A)

    jobs = [(h, pr) for h in heads for pr in range(len(pairs))]
    qn, rhs, glast, kdec = [], [], [], []
    kk, qk, decay = {}, {}, {}
    for h in heads:
        lo = h * DK_A
        qh = qkv[:, lo:lo + DK_A]
        kh = qkv[:, A_QK + lo:A_QK + lo + DK_A]
        vh = qkv[:, 2 * A_QK + lo:2 * A_QK + lo + DV_A]
        q_h = qh * (lax.rsqrt(jnp.sum(qh * qh, -1, keepdims=True) + NORM_EPS) * (DK_A ** -0.5))
        k_h = kh * lax.rsqrt(jnp.sum(kh * kh, -1, keepdims=True) + NORM_EPS)
        bcol = beta_all[:, h:h + 1]
        g_h = gcum[:, H_A + h:H_A + h + 1]
        gl_h = gsum[:, H_A + h:H_A + h + 1]
        eg = jnp.exp(g_h)
        kb = k_h * bcol
        k_b = k_h.astype(BF16)
        kb_b = kb.astype(BF16)
        q_b = q_h.astype(BF16)
        g_row = gcum_t[H_A + h:H_A + h + 1, :]
        for pr, rows in enumerate(pairs):
            job = (h, pr)
            kk[job] = _dot_nt(kb_b[rows], k_b[rows])
            qk[job] = _dot_nt(q_b[rows], k_b[rows])
            decay[job] = jnp.exp(jnp.where(causal, g_h[rows] - g_row[:, rows], -jnp.inf))
        rhs.append(jnp.concatenate([vh * bcol, kb * eg], axis=1).astype(BF16))
        qn.append(q_h * eg)
        kdec.append(k_h * jnp.exp(gl_h - g_h))
        glast.append(gl_h)

    p = {job: jnp.where(strict, -(kk[job] * decay[job]), 0.0) for job in jobs}
    inv = {job: eye + p[job] for job in jobs}
    for _ in range(NEUMANN_STEPS):
        pb = {job: p[job].astype(BF16) for job in jobs}
        p = {job: _dot(pb[job], pb[job]) for job in jobs}
        inv = {job: inv[job] + _dot(p[job].astype(BF16), inv[job].astype(BF16)) for job in jobs}
    uw_j = {(h, pr): _dot(inv[(h, pr)].astype(BF16), rhs[h][pairs[pr]]).astype(BF16)
            for h, pr in jobs}
    aw_j = {job: _dot((qk[job] * decay[job]).astype(BF16), uw_j[job]) for job in jobs}
    uw = [jnp.concatenate([uw_j[(h, pr)] for pr in range(len(pairs))], axis=0) for h in heads]
    aw = [jnp.concatenate([aw_j[(h, pr)] for pr in range(len(pairs))], axis=0) for h in heads]
    qt = [(qn[h] - aw[h][:, DV_A:DV_A + DK_A]).astype(BF16) for h in heads]

    bp = [[_dot(kdec[h][c * CHUNK:(c + 1) * CHUNK].T.astype(BF16), uw[h][c * CHUNK:(c + 1) * CHUNK])
           for c in range(GDN_NC)] for h in heads]
    state = [st_ref[h] for h in heads]
    for c in range(GDN_NC):
        r0 = c * CHUNK
        for h in heads:
            lo = h * DV_A
            s_b = state[h].astype(BF16)
            o = _dot(qt[h][r0:r0 + CHUNK], s_b) + aw[h][r0:r0 + CHUNK, 0:DV_A]
            decay_c = jnp.exp(glast[h][r0:r0 + 1])
            state[h] = (state[h] * decay_c + bp[h][c][:, 0:DV_A]
                        - _dot(bp[h][c][:, DV_A:DV_A + DK_A].astype(BF16), s_b))
            z = a_ref[r0:r0 + CHUNK, QKV_A + lo:QKV_A + lo + DV_A]
            on = o * lax.rsqrt(jnp.mean(o * o, -1, keepdims=True) + NORM_EPS) * ng * _silu(z)
            o_ref[r0:r0 + CHUNK, lo:lo + DV_A] = on.astype(BF16)
    for h in heads:
        st_ref[h] = state[h]


def _proj_gdn(x, wa, ws, wb, wc, ba, bs, bb, bc, conv_w, alog_row, dtb_row, norm_g, batch, seq):
    n = x.shape[0]
    nblk = seq // GDN_R
    widths = (A_W, SMALL_W, 3 * B_W, C_W)
    row = lambda w: pl.BlockSpec((GDN_R, w), lambda b, i: (b * nblk + i, 0))
    const = lambda r, w: pl.BlockSpec((r, w), lambda b, i: (0, 0))
    return pl.pallas_call(
        _proj_gdn_kernel,
        grid=(batch, nblk),
        in_specs=[row(D_MODEL)] + [const(D_MODEL, w) for w in widths] + [const(1, w) for w in widths]
        + [const(CONV_K, QKV_A), const(1, SMALL_W), const(1, SMALL_W), const(1, DV_A)],
        out_specs=[row(A_V), row(3 * B_W), row(C_W)],
        out_shape=[jax.ShapeDtypeStruct((n, A_V), BF16), jax.ShapeDtypeStruct((n, 3 * B_W), BF16),
                   jax.ShapeDtypeStruct((n, C_W), BF16)],
        scratch_shapes=[pltpu.VMEM((GDN_R, A_W), F32), pltpu.VMEM((GDN_R, SMALL_W), F32),
                        pltpu.VMEM((CONV_PAD, QKV_A), F32), pltpu.VMEM((H_A, DK_A, DV_A), F32)],
        compiler_params=_params("parallel", "arbitrary"),
        name="proj_gdn",
    )(x, wa, ws, wb, wc, ba, bs, bb, bc, conv_w, alog_row, dtb_row, norm_g)


ATT_R = 256
ATT_CH = ATT_R // CHUNK
B_KBLK = 3
assert (B_KBLK - 1) * ATT_CH == B_PREV
B_STRIP_W = B_KBLK * ATT_R + (ATT_CH - 2) * CHUNK
B_QR = 2 * CHUNK
B_KW = (B_PREV + 2) * CHUNK
B_BIAS0 = (ATT_CH - 2) * CHUNK


def _attn_b_pieces():
    pieces = []
    for half in range(ATT_R // B_QR):
        lo, cut = half * B_QR, []
        while lo < half * B_QR + B_KW:
            hi = min((lo // ATT_R + 1) * ATT_R, half * B_QR + B_KW)
            cut.append((lo // ATT_R, lo % ATT_R, hi - lo))
            lo = hi
        pieces.append(cut)
    return pieces


B_JOBS = [(h, half) for h in range(H_B) for half in range(ATT_R // B_QR)]


def _attn_b_scores(i, q_ref, k_refs, bias_ref, row0):
    pieces = _attn_b_pieces()
    scores = []
    for h, half in B_JOBS:
        cols = slice(h * D_B, (h + 1) * D_B)
        q = q_ref[row0 + half * B_QR:row0 + (half + 1) * B_QR, cols]
        parts = []
        for j, r0, nr in pieces[half]:
            sj = _dot_nt(q, k_refs[j][r0:r0 + nr, cols])
            if j < B_KBLK - 1:
                sj = jnp.where(i + j >= B_KBLK - 1, sj, NEG_BIG)
            parts.append(sj)
        bias = jnp.concatenate([bias_ref[h, 1, :, B_BIAS0:B_BIAS0 + B_KW],
                                bias_ref[h, 0, :, B_BIAS0:B_BIAS0 + B_KW]], axis=0)
        scores.append(jnp.concatenate(parts, axis=1) + bias)
    return scores


def _attn_b_finish(scores, v_refs, o_ref, row0):
    pieces = _attn_b_pieces()
    probs, denoms = [], []
    for s in scores:
        p = jnp.exp2(s - jnp.max(s, axis=-1, keepdims=True))
        denoms.append(jnp.sum(p, axis=-1, keepdims=True))
        probs.append(p.astype(BF16))
    for (h, half), pb, denom in zip(B_JOBS, probs, denoms):
        cols = slice(h * D_B, (h + 1) * D_B)
        o, c0 = None, 0
        for j, r0, nr in pieces[half]:
            term = _dot(pb[:, c0:c0 + nr], v_refs[j][r0:r0 + nr, cols])
            o = term if o is None else o + term
            c0 += nr
        o_ref[row0 + half * B_QR:row0 + (half + 1) * B_QR, cols] = (o / denom).astype(BF16)


def _attn_b_bias(rel_bias):
    nh, nrel = rel_bias.shape
    assert nrel == CHUNK + REL_CLIP
    lead = (ATT_CH - 1) * CHUNK
    vis_end = lead + (B_PREV + 1) * CHUNK
    width = B_KBLK * ATT_R + lead
    period = 1 << (width + CHUNK - 1).bit_length()
    rb = rel_bias.astype(F32)
    far = jnp.broadcast_to(rb[:, -1:], (nh, period))
    f = jnp.concatenate([far[:, :vis_end - nrel], jnp.flip(rb, axis=1), far[:, :period - vis_end]], axis=1)
    g = jnp.concatenate([f[:, :1], jnp.flip(f[:, 1:], axis=1)], axis=1)
    skew = jnp.tile(g, (1, CHUNK + 1))[:, :CHUNK * (period + 1)].reshape(nh, CHUNK, period + 1)
    toep = LOG2E * jnp.flip(skew[:, :, period - width + 1:period + 1], axis=2)
    c = np.arange(width)
    strip = jnp.where(jnp.asarray((c >= lead) & (c < vis_end))[None, None, :], toep, NEG_BIG)
    return jnp.stack([strip[:, :, 0:B_STRIP_W], strip[:, :, CHUNK:CHUNK + B_STRIP_W]], axis=1)


C_KR = 2 * CHUNK
C_KBLK = 3


C_JOBS = [(h, half) for h in range(H_C) for half in range(ATT_R // C_KR)]


def _kv_cols(h):
    return slice((h // G_C) * D_C, (h // G_C + 1) * D_C)


def _attn_c_scores(i, q_ref, k_refs, tbl_ref, row0):
    scores = []
    for h, half in C_JOBS:
        q = q_ref[row0 + half * C_KR:row0 + (half + 1) * C_KR, h * D_C:(h + 1) * D_C]
        s_prev = _dot_nt(q, k_refs[half][:, _kv_cols(h)])
        if half == 0:
            s_prev = jnp.where(i > 0, s_prev, NEG_BIG)
        s_own = _dot_nt(q, k_refs[half + 1][:, _kv_cols(h)])
        scores.append(jnp.concatenate([s_prev, s_own], axis=1) + tbl_ref[h])
    return scores


def _attn_c_finish(scores, sink_ref, v_refs, o_ref, row0):
    probs, denoms = [], []
    for (h, half), s in zip(C_JOBS, scores):
        sink = sink_ref[h] * LOG2E
        mx = jnp.maximum(jnp.max(s, axis=-1, keepdims=True), sink)
        p = jnp.exp2(s - mx)
        denoms.append(jnp.sum(p, axis=-1, keepdims=True) + jnp.exp2(sink - mx))
        probs.append(p.astype(BF16))
    for (h, half), pb, denom in zip(C_JOBS, probs, denoms):
        o = (_dot(pb[:, 0:C_KR], v_refs[half][:, _kv_cols(h)])
             + _dot(pb[:, C_KR:2 * C_KR], v_refs[half + 1][:, _kv_cols(h)]))
        o_ref[row0 + half * C_KR:row0 + (half + 1) * C_KR, h * D_C:(h + 1) * D_C] = (
            o / denom).astype(BF16)


def _attn_c_table():
    tq = np.arange(C_KR)[:, None] + C_KR
    tk = np.arange(2 * C_KR)[None, :]
    delta = tq // CHUNK - tk // CHUNK
    vis = (delta >= 0) & (delta <= C_PREV)
    slopes = 2.0 ** (-8.0 * np.arange(1, H_C + 1) / H_C)
    alibi = -slopes[:, None, None] * np.abs(tq - tk)[None].astype(np.float64)
    tbl = np.where(vis[None], LOG2E * alibi, NEG_BIG)
    return jnp.asarray(tbl, F32)


MERGE_WR = 512
MIX_SUB = 2


def _mix_merge_kernel(layer, sink_ref, x_ref, oa_ref, qb_ref, *rest):
    kper = ATT_R // C_KR
    nkb, nkc = MIX_SUB + B_KBLK - 1, kper * MIX_SUB + C_KBLK - 2
    kb_refs, vb_refs = rest[0:nkb], rest[nkb:2 * nkb]
    bias_ref, qc_ref = rest[2 * nkb:2 * nkb + 2]
    rest = rest[2 * nkb + 2:]
    kc_refs, vc_refs = rest[0:nkc], rest[nkc:2 * nkc]
    (tbl_ref, wg_hbm, bg_ref, wa_hbm, wb_hbm, wc_hbm, wo_hbm, g_ref, b_ref, o_ref,
     wg_ref, wa_ref, wb_ref, wc_ref, wo_ref, stage, sem, ob_ref, oc_ref) = rest[2 * nkc:]
    i = pl.program_id(1)

    @pl.when((pl.program_id(0) == 0) & (i == 0))
    def _():
        r = MERGE_WR
        pairs = [(wg_hbm.at[layer, pl.ds(m * r, r), pl.ds(j * D_MODEL, D_MODEL)],
                  wg_ref.at[pl.ds(m * r, r), pl.ds(j * D_MODEL, D_MODEL)])
                 for m in range(D_MODEL // r) for j in range(N_BRANCH)]
        pairs += [(src.at[layer], dst) for src, dst in
                  ((wa_hbm, wa_ref), (wb_hbm, wb_ref), (wc_hbm, wc_ref))]
        pairs += [(wo_hbm.at[layer, pl.ds(m * r, r), :], wo_ref.at[pl.ds(m * r, r), :])
                  for m in range(D_MODEL // r)]
        _stage_cast(pairs, stage, sem)

    st = [dict() for _ in range(MIX_SUB)]

    def qk(s):
        blk = i * MIX_SUB + s
        st[s]["b"] = _attn_b_scores(blk, qb_ref, kb_refs[s:s + B_KBLK], bias_ref, s * ATT_R)
        st[s]["c"] = _attn_c_scores(blk, qc_ref, kc_refs[kper * s:kper * s + C_KBLK], tbl_ref,
                                    s * ATT_R)

    def gates(s):
        rows = slice(s * ATT_R, (s + 1) * ATT_R)
        x = x_ref[rows, :]
        xb = x.astype(BF16)
        st[s]["x"] = x
        st[s]["g"] = [jax.nn.sigmoid(_dot(xb, wg_ref[:, r * D_MODEL:(r + 1) * D_MODEL])
                                     + bg_ref[:, r * D_MODEL:(r + 1) * D_MODEL])
                      for r in range(N_BRANCH)]
        st[s]["m"] = st[s]["g"][0] * _dot(oa_ref[rows, :], wa_ref[...])

    def attend(s):
        _attn_b_finish(st[s]["b"], vb_refs[s:s + B_KBLK], ob_ref, s * ATT_R)
        _attn_c_finish(st[s]["c"], sink_ref, vc_refs[kper * s:kper * s + C_KBLK], oc_ref, s * ATT_R)

    def finish(s):
        rows = slice(s * ATT_R, (s + 1) * ATT_R)
        merged = st[s]["m"] + st[s]["g"][1] * _dot(ob_ref[rows, :], wb_ref[...])
        merged = merged + st[s]["g"][2] * _dot(oc_ref[rows, :], wc_ref[...])
        y = ALPHA * st[s]["x"] + _dot(merged.astype(BF16), wo_ref[...])
        o_ref[rows, :] = _layer_norm(y, g_ref[...], b_ref[...])

    qk(0)
    gates(0)
    for s in range(MIX_SUB):
        if s + 1 < MIX_SUB:
            qk(s + 1)
        attend(s)
        if s + 1 < MIX_SUB:
            gates(s + 1)
        finish(s)


def _mix_merge(x, oa, qkv_b, qkv_c, bias, sinks, w_gate, b_gate, w_a, w_b, w_c, w_out, g, b,
               batch, seq, layer):
    n = x.shape[0]
    assert A_V == B_W == C_Q == MERGE_WR
    tm = MIX_SUB * ATT_R
    nstep = seq // tm
    nblk = seq // ATT_R
    kper = ATT_R // C_KR
    kcol, vcol = C_Q // C_KV, C_Q // C_KV + 1
    row = lambda w: pl.BlockSpec((tm, w), lambda bb, i: (bb * nstep + i, 0))
    const = lambda *shape: pl.BlockSpec(shape, lambda bb, i: (0,) * len(shape))
    hbm = pl.BlockSpec(memory_space=pl.ANY)

    def kv_b(j, colblk):
        return pl.BlockSpec(
            (ATT_R, B_W),
            lambda bb, i: (bb * nblk + jnp.maximum(i * MIX_SUB + j - (B_KBLK - 1), 0), colblk))

    def kv_c(j, colblk):
        return pl.BlockSpec(
            (C_KR, C_KV),
            lambda bb, i: (bb * nblk * kper + jnp.maximum(i * MIX_SUB * kper + j - 1, 0), colblk))

    nkb, nkc = MIX_SUB + B_KBLK - 1, kper * MIX_SUB + C_KBLK - 2
    in_specs = ([pl.BlockSpec(memory_space=pltpu.SMEM), row(D_MODEL), row(A_V), row(B_W)]
                + [kv_b(j, 1) for j in range(nkb)] + [kv_b(j, 2) for j in range(nkb)]
                + [pl.BlockSpec((H_B, 2, CHUNK, B_STRIP_W), lambda bb, i: (layer, 0, 0, 0)), row(C_Q)]
                + [kv_c(j, kcol) for j in range(nkc)] + [kv_c(j, vcol) for j in range(nkc)]
                + [const(H_C, C_KR, 2 * C_KR),
                   hbm, const(1, N_BRANCH * D_MODEL), hbm, hbm, hbm, hbm,
                   const(1, D_MODEL), const(1, D_MODEL)])
    return pl.pallas_call(
        functools.partial(_mix_merge_kernel, layer),
        grid=(batch, nstep),
        in_specs=in_specs,
        out_specs=row(D_MODEL),
        out_shape=jax.ShapeDtypeStruct((n, D_MODEL), F32),
        scratch_shapes=[pltpu.VMEM((D_MODEL, N_BRANCH * D_MODEL), BF16),
                        pltpu.VMEM((A_V, D_MODEL), BF16), pltpu.VMEM((B_W, D_MODEL), BF16),
                        pltpu.VMEM((C_Q, D_MODEL), BF16), pltpu.VMEM((D_MODEL, D_MODEL), BF16),
                        pltpu.VMEM((2, MERGE_WR, D_MODEL), F32), pltpu.SemaphoreType.DMA((2,)),
                        pltpu.VMEM((tm, B_W), BF16), pltpu.VMEM((tm, C_Q), BF16)],
        compiler_params=_params("arbitrary", "arbitrary"),
        name="mix_merge",
    )(sinks, x, oa, qkv_b, *([qkv_b] * (2 * nkb)), bias,
      qkv_c, *([qkv_c] * (2 * nkc)), _attn_c_table(),
      w_gate, b_gate, w_a, w_b, w_c, w_out, g, b)


def _lane_row(vals, offset):
    return jnp.pad(vals.astype(F32), (offset, LANES - offset - vals.shape[0])).reshape(1, LANES)


def kernel(x, ln1_g, ln1_b, w_ff1_in, w_ff1_out, w_in, b_in, conv_w, a_log, dt_bias, gdn_norm_g,
           rel_bias, sinks, w_gate, b_gate, w_br_a, w_br_b, w_br_c, w_out, ln2_g, ln2_b,
           w_ff2_in, w_ff2_out, ln3_g, ln3_b):
    batch, seq, d = x.shape
    assert d == D_MODEL and seq % ATT_R == 0 and (batch * seq) % FFN_TM == 0
    h = x.reshape(batch * seq, d)
    row = lambda v: v.reshape(1, -1).astype(F32)
    o_small = QKV_A
    o_z = o_small + 2 * H_A
    o_b = o_z + A_V
    o_c = o_b + 3 * B_W
    bias_strips = _attn_b_bias(rel_bias.reshape(DEPTH * H_B, CHUNK + REL_CLIP))
    for l in range(DEPTH):
        h = _ffn(h, w_ff1_in, w_ff1_out, row(ln1_g[l]), row(ln1_b[l]), l)
        wi, bi = w_in[l], b_in[l]
        pad = SMALL_W - 2 * H_A
        wa = jnp.concatenate([wi[:, 0:QKV_A], wi[:, o_z:o_b]], axis=1).astype(BF16)
        ws = jnp.pad(wi[:, o_small:o_z], ((0, 0), (0, pad))).astype(BF16)
        ba = jnp.concatenate([bi[0:QKV_A], bi[o_z:o_b]])
        bs = jnp.pad(bi[o_small:o_z], (0, pad))
        o_a, pb, pc = _proj_gdn(h, wa, ws, wi[:, o_b:o_c].astype(BF16), wi[:, o_c:].astype(BF16),
                                row(ba), row(bs), row(bi[o_b:o_c]), row(bi[o_c:]),
                                conv_w[l].astype(F32), _lane_row(a_log[l], H_A),
                                _lane_row(dt_bias[l], H_A), row(gdn_norm_g[l]), batch, seq)
        h = _mix_merge(h, o_a, pb, pc, bias_strips, sinks[l].astype(F32), w_gate, row(b_gate[l]),
                       w_br_a, w_br_b, w_br_c, w_out, row(ln2_g[l]), row(ln2_b[l]), batch, seq, l)
        h = _ffn(h, w_ff2_in, w_ff2_out, row(ln3_g[l]), row(ln3_b[l]), l)
    return h.reshape(batch, seq, d)
```

```python
import functools
import math

import jax
import jax.numpy as jnp
import numpy as np
from jax import lax
from jax.experimental import pallas as pl
from jax.experimental.pallas import tpu as pltpu

F32 = jnp.float32
BF16 = jnp.bfloat16

D_MODEL = 1024
DEPTH = 2
CHUNK = 64
D_FF = 4096
LN_EPS = 1e-5
NORM_EPS = 1e-6
H_A, DK_A, DV_A, CONV_K = 4, 128, 128, 4
A_QK = H_A * DK_A
A_V = H_A * DV_A
H_B, D_B, B_PREV, REL_CLIP = 4, 128, 8, 128
B_W = H_B * D_B
H_C, HKV_C, D_C, WINDOW = 8, 2, 64, 128
G_C = H_C // HKV_C
C_PREV = WINDOW // CHUNK
C_Q = H_C * D_C
C_KV = HKV_C * D_C
N_BRANCH = 3
ALPHA = (2.0 * DEPTH) ** 0.25

LANES = 128
VMEM_LIMIT = 56 * 1024 * 1024
NEG_BIG = -1e30
LOG2E = math.log2(math.e)

SMALL_W = LANES
QKV_A = 2 * A_QK + A_V
A_W = QKV_A + A_V
C_W = C_Q + 2 * C_KV


def _dot(a, b):
    return jnp.dot(a, b, preferred_element_type=F32)


def _dot_nt(a, b):
    return lax.dot_general(a, b, (((1,), (1,)), ((), ())), preferred_element_type=F32)


def _layer_norm(y, g, b):
    mu = jnp.mean(y, axis=-1, keepdims=True)
    yc = y - mu
    var = jnp.mean(yc * yc, axis=-1, keepdims=True)
    return yc * lax.rsqrt(var + LN_EPS) * g + b


def _silu_of_half(xh):
    return xh + xh * jnp.tanh(xh)


def _silu(x):
    return _silu_of_half(0.5 * x)


def _params(*sem):
    return pltpu.CompilerParams(dimension_semantics=sem, vmem_limit_bytes=VMEM_LIMIT)


FFN_TM = 512
FFN_SUB = 256
FFN_TF = 1024


FFN_WCH = 1024


def _stage_cast(pairs, stage, sem):
    copies = [pltpu.make_async_copy(src, stage.at[c % 2], sem.at[c % 2])
              for c, (src, _) in enumerate(pairs)]
    copies[0].start()
    for c, (_, dst) in enumerate(pairs):
        if c + 1 < len(pairs):
            copies[c + 1].start()
        copies[c].wait()
        dst[...] = stage[c % 2].astype(BF16)


def _ffn_kernel(layer, x_ref, wi_hbm, wo_hbm, g_ref, b_ref, o_ref, wi_ref, wo_ref, h_ref, stage, sem):
    @pl.when(pl.program_id(0) == 0)
    def _():
        w = FFN_WCH
        pairs = [(wi_hbm.at[layer, :, pl.ds(c * w, w)], wi_ref.at[:, pl.ds(c * w, w)])
                 for c in range(2 * D_FF // w)]
        pairs += [(wo_hbm.at[layer, pl.ds(c * w, w), :], wo_ref.at[pl.ds(c * w, w), :])
                  for c in range(D_FF // w)]
        _stage_cast(pairs, stage, sem)

    for s in range(FFN_TM // FFN_SUB):
        rows = slice(s * FFN_SUB, (s + 1) * FFN_SUB)
        x = x_ref[rows, :]
        xb = x.astype(BF16)
        for f in range(D_FF // FFN_TF):
            gate = _dot(xb, wi_ref[:, f * FFN_TF:(f + 1) * FFN_TF])
            up = _dot(xb, wi_ref[:, D_FF + f * FFN_TF:D_FF + (f + 1) * FFN_TF])
            h_ref[rows, f * FFN_TF:(f + 1) * FFN_TF] = (_silu(gate) * up).astype(BF16)
        y = ALPHA * x + 0.5 * _dot(h_ref[rows, :], wo_ref[...])
        o_ref[rows, :] = _layer_norm(y, g_ref[...], b_ref[...])


def _ffn(x, w_in, w_out, g, b, layer):
    n = x.shape[0]
    resident = lambda r, w: pl.BlockSpec((r, w), lambda i: (0, 0), pipeline_mode=pl.Buffered(1))
    return pl.pallas_call(
        functools.partial(_ffn_kernel, layer),
        grid=(n // FFN_TM,),
        in_specs=[
            pl.BlockSpec((FFN_TM, D_MODEL), lambda i: (i, 0)),
            pl.BlockSpec(memory_space=pl.ANY), pl.BlockSpec(memory_space=pl.ANY),
            resident(1, D_MODEL), resident(1, D_MODEL),
        ],
        out_specs=pl.BlockSpec((FFN_TM, D_MODEL), lambda i: (i, 0)),
        out_shape=jax.ShapeDtypeStruct((n, D_MODEL), F32),
        scratch_shapes=[pltpu.VMEM((D_MODEL, 2 * D_FF), BF16), pltpu.VMEM((D_FF, D_MODEL), BF16),
                        pltpu.VMEM((FFN_TM, D_FF), BF16),
                        pltpu.VMEM((2, FFN_WCH, FFN_WCH), F32), pltpu.SemaphoreType.DMA((2,))],
        compiler_params=_params("arbitrary"),
        name="ffn_ln",
    )(x, w_in, w_out, g, b)


GDN_R = 256
PG_SUB = 2
GDN_NC = GDN_R // CHUNK
GDN_PAIR = 2 * CHUNK
CONV_PAD = 8
LOG2_CHUNK = CHUNK.bit_length() - 1
NEUMANN_STEPS = LOG2_CHUNK - 1


def _proj_gdn_kernel(x_ref, wa_ref, ws_ref, wb_ref, wc_ref, ba_ref, bs_ref, bb_ref, bc_ref,
                     cw_ref, alog_ref, dtb_ref, ng_ref, o_ref, ob_ref, oc_ref,
                     a_ref, s_ref, xbuf, st_ref):
    @pl.when(pl.program_id(1) == 0)
    def _():
        st_ref[...] = jnp.zeros_like(st_ref)
        xbuf[...] = jnp.zeros_like(xbuf)

    for sb in range(PG_SUB):
        rows = pl.ds(sb * GDN_R, GDN_R)
        hist_ref = xbuf if sb == 0 else a_ref.at[pl.ds(sb * GDN_R - CONV_PAD, CONV_PAD), pl.ds(0, QKV_A)]
        _proj_gdn_block(x_ref.at[rows], wa_ref, ws_ref, wb_ref, wc_ref, ba_ref, bs_ref, bb_ref, bc_ref,
                        cw_ref, alog_ref, dtb_ref, ng_ref, o_ref.at[rows], ob_ref.at[rows],
                        oc_ref.at[rows], a_ref.at[rows], s_ref.at[rows], hist_ref, xbuf, st_ref)


def _proj_gdn_block(x_ref, wa_ref, ws_ref, wb_ref, wc_ref, ba_ref, bs_ref, bb_ref, bc_ref,
                    cw_ref, alog_ref, dtb_ref, ng_ref, o_ref, ob_ref, oc_ref,
                    a_ref, s_ref, hist_ref, xbuf, st_ref):
    xb = x_ref[...].astype(BF16)
    a_ref[...] = _dot(xb, wa_ref[...]) + ba_ref[...]
    s_ref[...] = _dot(xb, ws_ref[...]) + bs_ref[...]
    yb = _dot(xb, wb_ref[...]) + bb_ref[...]
    ob_ref[:, 0:B_W] = (yb[:, 0:B_W] * (D_B ** -0.5 * LOG2E)).astype(BF16)
    ob_ref[:, B_W:3 * B_W] = yb[:, B_W:3 * B_W].astype(BF16)
    yc = _dot(xb, wc_ref[...]) + bc_ref[...]
    oc_ref[:, 0:C_Q] = (yc[:, 0:C_Q] * (D_C ** -0.5 * LOG2E)).astype(BF16)
    oc_ref[:, C_Q:C_W] = yc[:, C_Q:C_W].astype(BF16)

    assert CONV_K == 4
    x_new = a_ref[:, 0:QKV_A]
    n_grp = GDN_R // CONV_PAD
    x3 = x_new.reshape(n_grp, CONV_PAD, QKV_A)
    hist = hist_ref[...].reshape(1, CONV_PAD, QKV_A)
    sub = lax.broadcasted_iota(jnp.int32, (n_grp, CONV_PAD, QKV_A), 1)
    w0, w1, w2, w3 = [0.5 * cw_ref[j:j + 1, :].reshape(1, 1, QKV_A) for j in range(CONV_K)]

    def delay(v, v_hist, s):
        rot = pltpu.roll(v, s, axis=1)
        prev = jnp.concatenate([pltpu.roll(v_hist, s, axis=1), rot[0:n_grp - 1]], axis=0)
        return jnp.where(sub >= s, rot, prev)

    dx = delay(x3, hist, 1)
    inner = w1 * x3 + w0 * dx
    inner_hist = w1 * hist + w0 * pltpu.roll(hist, 1, axis=1)
    y_half = w3 * x3 + w2 * dx + delay(inner, inner_hist, 2)
    xbuf[...] = x_new[GDN_R - CONV_PAD:GDN_R]
    qkv = _silu_of_half(y_half.reshape(GDN_R, QKV_A))

    s = s_ref[...]
    beta_all = jax.nn.sigmoid(s)
    g_all = -jnp.exp(alog_ref[...]) * jax.nn.softplus(s + dtb_ref[...])
    row = lax.broadcasted_iota(jnp.int32, (GDN_PAIR, GDN_PAIR), 0)
    col = lax.broadcasted_iota(jnp.int32, (GDN_PAIR, GDN_PAIR), 1)
    same_chunk = jnp.right_shift(row, LOG2_CHUNK) == jnp.right_shift(col, LOG2_CHUNK)
    causal = same_chunk & (col <= row)
    strict = same_chunk & (col < row)
    eye = jnp.where(row == col, 1.0, 0.0).astype(F32)
    pairs = [slice(p * GDN_PAIR, (p + 1) * GDN_PAIR) for p in range(GDN_R // GDN_PAIR)]
    g_hi = g_all.astype(BF16)
    g_r1 = g_all - g_hi.astype(F32)
    g_mid = g_r1.astype(BF16)
    g_lo = (g_r1 - g_mid.astype(F32)).astype(BF16)
    g_split = jnp.concatenate([g_hi, g_mid, g_lo], axis=1)
    tri = jnp.where(causal, 1.0, 0.0).astype(BF16)
    g_parts = jnp.concatenate([_dot(tri, g_split[rows]) for rows in pairs], axis=0)
    gcum = g_parts[:, 0:LANES] + g_parts[:, LANES:2 * LANES] + g_parts[:, 2 * LANES:3 * LANES]
    gsum = jnp.concatenate(
        [jnp.broadcast_to(gcum[(c + 1) * CHUNK - 1:(c + 1) * CHUNK, :], (CHUNK, LANES))
         for c in range(GDN_NC)], axis=0)
    gcum_t = gcum.T
    ng = ng_ref[...]
    heads = range(H_A)

    jobs = [(h, pr) for h in heads for pr in range(len(pairs))]
    qn, rhs, glast, kdec = [], [], [], []
    kk, qk, decay = {}, {}, {}
    for h in heads:
        lo = h * DK_A
        qh = qkv[:, lo:lo + DK_A]
        kh = qkv[:, A_QK + lo:A_QK + lo + DK_A]
        vh = qkv[:, 2 * A_QK + lo:2 * A_QK + lo + DV_A]
        q_h = qh * (lax.rsqrt(jnp.sum(qh * qh, -1, keepdims=True) + NORM_EPS) * (DK_A ** -0.5))
        k_h = kh * lax.rsqrt(jnp.sum(kh * kh, -1, keepdims=True) + NORM_EPS)
        bcol = beta_all[:, h:h + 1]
        g_h = gcum[:, H_A + h:H_A + h + 1]
        gl_h = gsum[:, H_A + h:H_A + h + 1]
        eg = jnp.exp(g_h)
        kb = k_h * bcol
        k_b = k_h.astype(BF16)
        kb_b = kb.astype(BF16)
        q_b = q_h.astype(BF16)
        g_row = gcum_t[H_A + h:H_A + h + 1, :]
        for pr, rows in enumerate(pairs):
            job = (h, pr)
            kk[job] = _dot_nt(kb_b[rows], k_b[rows])
            qk[job] = _dot_nt(q_b[rows], k_b[rows])
            decay[job] = jnp.exp(jnp.where(causal, g_h[rows] - g_row[:, rows], -jnp.inf))
        rhs.append(jnp.concatenate([vh * bcol, kb * eg], axis=1).astype(BF16))
        qn.append(q_h * eg)
        kdec.append(k_h * jnp.exp(gl_h - g_h))
        glast.append(gl_h)

    p = {job: jnp.where(strict, -(kk[job] * decay[job]), 0.0) for job in jobs}
    inv = {job: eye + p[job] for job in jobs}
    for _ in range(NEUMANN_STEPS):
        pb = {job: p[job].astype(BF16) for job in jobs}
        p = {job: _dot(pb[job], pb[job]) for job in jobs}
        inv = {job: inv[job] + _dot(p[job].astype(BF16), inv[job].astype(BF16)) for job in jobs}
    uw_j = {(h, pr): _dot(inv[(h, pr)].astype(BF16), rhs[h][pairs[pr]]).astype(BF16)
            for h, pr in jobs}
    aw_j = {job: _dot((qk[job] * decay[job]).astype(BF16), uw_j[job]) for job in jobs}
    uw = [jnp.concatenate([uw_j[(h, pr)] for pr in range(len(pairs))], axis=0) for h in heads]
    aw = [jnp.concatenate([aw_j[(h, pr)] for pr in range(len(pairs))], axis=0) for h in heads]
    qt = [(qn[h] - aw[h][:, DV_A:DV_A + DK_A]).astype(BF16) for h in heads]

    bp = [[_dot(kdec[h][c * CHUNK:(c + 1) * CHUNK].T.astype(BF16), uw[h][c * CHUNK:(c + 1) * CHUNK])
           for c in range(GDN_NC)] for h in heads]
    state = [st_ref[h] for h in heads]
    for c in range(GDN_NC):
        r0 = c * CHUNK
        for h in heads:
            lo = h * DV_A
            s_b = state[h].astype(BF16)
            o = _dot(qt[h][r0:r0 + CHUNK], s_b) + aw[h][r0:r0 + CHUNK, 0:DV_A]
            decay_c = jnp.exp(glast[h][r0:r0 + 1])
            state[h] = (state[h] * decay_c + bp[h][c][:, 0:DV_A]
                        - _dot(bp[h][c][:, DV_A:DV_A + DK_A].astype(BF16), s_b))
            z = a_ref[r0:r0 + CHUNK, QKV_A + lo:QKV_A + lo + DV_A]
            on = o * lax.rsqrt(jnp.mean(o * o, -1, keepdims=True) + NORM_EPS) * ng * _silu(z)
            o_ref[r0:r0 + CHUNK, lo:lo + DV_A] = on.astype(BF16)
    for h in heads:
        st_ref[h] = state[h]


def _proj_gdn(x, wa, ws, wb, wc, ba, bs, bb, bc, conv_w, alog_row, dtb_row, norm_g, batch, seq):
    n = x.shape[0]
    tm = PG_SUB * GDN_R
    nblk = seq // tm
    widths = (A_W, SMALL_W, 3 * B_W, C_W)
    row = lambda w: pl.BlockSpec((tm, w), lambda b, i: (b * nblk + i, 0))
    const = lambda r, w: pl.BlockSpec((r, w), lambda b, i: (0, 0))
    return pl.pallas_call(
        _proj_gdn_kernel,
        grid=(batch, nblk),
        in_specs=[row(D_MODEL)] + [const(D_MODEL, w) for w in widths] + [const(1, w) for w in widths]
        + [const(CONV_K, QKV_A), const(1, SMALL_W), const(1, SMALL_W), const(1, DV_A)],
        out_specs=[row(A_V), row(3 * B_W), row(C_W)],
        out_shape=[jax.ShapeDtypeStruct((n, A_V), BF16), jax.ShapeDtypeStruct((n, 3 * B_W), BF16),
                   jax.ShapeDtypeStruct((n, C_W), BF16)],
        scratch_shapes=[pltpu.VMEM((tm, A_W), F32), pltpu.VMEM((tm, SMALL_W), F32),
                        pltpu.VMEM((CONV_PAD, QKV_A), F32), pltpu.VMEM((H_A, DK_A, DV_A), F32)],
        compiler_params=_params("parallel", "arbitrary"),
        name="proj_gdn",
    )(x, wa, ws, wb, wc, ba, bs, bb, bc, conv_w, alog_row, dtb_row, norm_g)


ATT_R = 256
ATT_CH = ATT_R // CHUNK
B_KBLK = 3
assert (B_KBLK - 1) * ATT_CH == B_PREV
B_STRIP_W = B_KBLK * ATT_R + (ATT_CH - 2) * CHUNK
B_QR = 2 * CHUNK
B_KW = (B_PREV + 2) * CHUNK
B_BIAS0 = (ATT_CH - 2) * CHUNK


def _attn_b_pieces():
    pieces = []
    for half in range(ATT_R // B_QR):
        lo, cut = half * B_QR, []
        while lo < half * B_QR + B_KW:
            hi = min((lo // ATT_R + 1) * ATT_R, half * B_QR + B_KW)
            cut.append((lo // ATT_R, lo % ATT_R, hi - lo))
            lo = hi
        pieces.append(cut)
    return pieces


B_JOBS = [(h, half) for h in range(H_B) for half in range(ATT_R // B_QR)]


def _attn_b_scores(i, q_ref, k_refs, bias_ref, row0):
    pieces = _attn_b_pieces()
    scores = []
    for h, half in B_JOBS:
        cols = slice(h * D_B, (h + 1) * D_B)
        q = q_ref[row0 + half * B_QR:row0 + (half + 1) * B_QR, cols]
        parts = []
        for j, r0, nr in pieces[half]:
            sj = _dot_nt(q, k_refs[j][r0:r0 + nr, cols])
            if j < B_KBLK - 1:
                sj = jnp.where(i + j >= B_KBLK - 1, sj, NEG_BIG)
            parts.append(sj)
        bias = jnp.concatenate([bias_ref[h, 1, :, B_BIAS0:B_BIAS0 + B_KW],
                                bias_ref[h, 0, :, B_BIAS0:B_BIAS0 + B_KW]], axis=0)
        scores.append(jnp.concatenate(parts, axis=1) + bias)
    return scores


def _attn_b_finish(scores, v_refs, o_ref, row0):
    pieces = _attn_b_pieces()
    probs, denoms = [], []
    for s in scores:
        p = jnp.exp2(s - jnp.max(s, axis=-1, keepdims=True))
        denoms.append(jnp.sum(p, axis=-1, keepdims=True))
        probs.append(p.astype(BF16))
    for (h, half), pb, denom in zip(B_JOBS, probs, denoms):
        cols = slice(h * D_B, (h + 1) * D_B)
        o, c0 = None, 0
        for j, r0, nr in pieces[half]:
            term = _dot(pb[:, c0:c0 + nr], v_refs[j][r0:r0 + nr, cols])
            o = term if o is None else o + term
            c0 += nr
        o_ref[row0 + half * B_QR:row0 + (half + 1) * B_QR, cols] = (o / denom).astype(BF16)


def _attn_b_bias(rel_bias):
    nh, nrel = rel_bias.shape
    assert nrel == CHUNK + REL_CLIP
    lead = (ATT_CH - 1) * CHUNK
    vis_end = lead + (B_PREV + 1) * CHUNK
    width = B_KBLK * ATT_R + lead
    period = 1 << (width + CHUNK - 1).bit_length()
    rb = rel_bias.astype(F32)
    far = jnp.broadcast_to(rb[:, -1:], (nh, period))
    f = jnp.concatenate([far[:, :vis_end - nrel], jnp.flip(rb, axis=1), far[:, :period - vis_end]], axis=1)
    g = jnp.concatenate([f[:, :1], jnp.flip(f[:, 1:], axis=1)], axis=1)
    skew = jnp.tile(g, (1, CHUNK + 1))[:, :CHUNK * (period + 1)].reshape(nh, CHUNK, period + 1)
    toep = LOG2E * jnp.flip(skew[:, :, period - width + 1:period + 1], axis=2)
    c = np.arange(width)
    strip = jnp.where(jnp.asarray((c >= lead) & (c < vis_end))[None, None, :], toep, NEG_BIG)
    return jnp.stack([strip[:, :, 0:B_STRIP_W], strip[:, :, CHUNK:CHUNK + B_STRIP_W]], axis=1)


C_KR = 2 * CHUNK
C_KBLK = 3


C_JOBS = [(h, half) for h in range(H_C) for half in range(ATT_R // C_KR)]


def _kv_cols(h):
    return slice((h // G_C) * D_C, (h // G_C + 1) * D_C)


def _attn_c_scores(i, q_ref, k_refs, tbl_ref, row0):
    scores = []
    for h, half in C_JOBS:
        q = q_ref[row0 + half * C_KR:row0 + (half + 1) * C_KR, h * D_C:(h + 1) * D_C]
        s_prev = _dot_nt(q, k_refs[half][:, _kv_cols(h)])
        if half == 0:
            s_prev = jnp.where(i > 0, s_prev, NEG_BIG)
        s_own = _dot_nt(q, k_refs[half + 1][:, _kv_cols(h)])
        scores.append(jnp.concatenate([s_prev, s_own], axis=1) + tbl_ref[h])
    return scores


def _attn_c_finish(scores, sink_ref, v_refs, o_ref, row0):
    probs, denoms = [], []
    for (h, half), s in zip(C_JOBS, scores):
        sink = sink_ref[h] * LOG2E
        mx = jnp.maximum(jnp.max(s, axis=-1, keepdims=True), sink)
        p = jnp.exp2(s - mx)
        denoms.append(jnp.sum(p, axis=-1, keepdims=True) + jnp.exp2(sink - mx))
        probs.append(p.astype(BF16))
    for (h, half), pb, denom in zip(C_JOBS, probs, denoms):
        o = (_dot(pb[:, 0:C_KR], v_refs[half][:, _kv_cols(h)])
             + _dot(pb[:, C_KR:2 * C_KR], v_refs[half + 1][:, _kv_cols(h)]))
        o_ref[row0 + half * C_KR:row0 + (half + 1) * C_KR, h * D_C:(h + 1) * D_C] = (
            o / denom).astype(BF16)


def _attn_c_table():
    tq = np.arange(C_KR)[:, None] + C_KR
    tk = np.arange(2 * C_KR)[None, :]
    delta = tq // CHUNK - tk // CHUNK
    vis = (delta >= 0) & (delta <= C_PREV)
    slopes = 2.0 ** (-8.0 * np.arange(1, H_C + 1) / H_C)
    alibi = -slopes[:, None, None] * np.abs(tq - tk)[None].astype(np.float64)
    tbl = np.where(vis[None], LOG2E * alibi, NEG_BIG)
    return jnp.asarray(tbl, F32)


MERGE_WR = 512
MIX_SUB = 2


def _mix_merge_kernel(layer, sink_ref, x_ref, oa_ref, qb_ref, *rest):
    kper = ATT_R // C_KR
    nkb, nkc = MIX_SUB + B_KBLK - 1, kper * MIX_SUB + C_KBLK - 2
    kb_refs, vb_refs = rest[0:nkb], rest[nkb:2 * nkb]
    bias_ref, qc_ref = rest[2 * nkb:2 * nkb + 2]
    rest = rest[2 * nkb + 2:]
    kc_refs, vc_refs = rest[0:nkc], rest[nkc:2 * nkc]
    (tbl_ref, wg_hbm, bg_ref, wa_hbm, wb_hbm, wc_hbm, wo_hbm, g_ref, b_ref, o_ref,
     wg_ref, wa_ref, wb_ref, wc_ref, wo_ref, stage, sem, ob_ref, oc_ref) = rest[2 * nkc:]
    i = pl.program_id(1)

    @pl.when((pl.program_id(0) == 0) & (i == 0))
    def _():
        r = MERGE_WR
        pairs = [(wg_hbm.at[layer, pl.ds(m * r, r), pl.ds(j * D_MODEL, D_MODEL)],
                  wg_ref.at[pl.ds(m * r, r), pl.ds(j * D_MODEL, D_MODEL)])
                 for m in range(D_MODEL // r) for j in range(N_BRANCH)]
        pairs += [(src.at[layer], dst) for src, dst in
                  ((wa_hbm, wa_ref), (wb_hbm, wb_ref), (wc_hbm, wc_ref))]
        pairs += [(wo_hbm.at[layer, pl.ds(m * r, r), :], wo_ref.at[pl.ds(m * r, r), :])
                  for m in range(D_MODEL // r)]
        _stage_cast(pairs, stage, sem)

    st = [dict() for _ in range(MIX_SUB)]

    def qk(s):
        blk = i * MIX_SUB + s
        st[s]["b"] = _attn_b_scores(blk, qb_ref, kb_refs[s:s + B_KBLK], bias_ref, s * ATT_R)
        st[s]["c"] = _attn_c_scores(blk, qc_ref, kc_refs[kper * s:kper * s + C_KBLK], tbl_ref,
                                    s * ATT_R)

    def gates(s):
        rows = slice(s * ATT_R, (s + 1) * ATT_R)
        x = x_ref[rows, :]
        xb = x.astype(BF16)
        st[s]["x"] = x
        st[s]["g"] = [jax.nn.sigmoid(_dot(xb, wg_ref[:, r * D_MODEL:(r + 1) * D_MODEL])
                                     + bg_ref[:, r * D_MODEL:(r + 1) * D_MODEL])
                      for r in range(N_BRANCH)]
        st[s]["m"] = st[s]["g"][0] * _dot(oa_ref[rows, :], wa_ref[...])

    def attend(s):
        _attn_b_finish(st[s]["b"], vb_refs[s:s + B_KBLK], ob_ref, s * ATT_R)
        _attn_c_finish(st[s]["c"], sink_ref, vc_refs[kper * s:kper * s + C_KBLK], oc_ref, s * ATT_R)

    def finish(s):
        rows = slice(s * ATT_R, (s + 1) * ATT_R)
        merged = st[s]["m"] + st[s]["g"][1] * _dot(ob_ref[rows, :], wb_ref[...])
        merged = merged + st[s]["g"][2] * _dot(oc_ref[rows, :], wc_ref[...])
        y = ALPHA * st[s]["x"] + _dot(merged.astype(BF16), wo_ref[...])
        o_ref[rows, :] = _layer_norm(y, g_ref[...], b_ref[...])

    qk(0)
    gates(0)
    for s in range(MIX_SUB):
        if s + 1 < MIX_SUB:
            qk(s + 1)
        attend(s)
        if s + 1 < MIX_SUB:
            gates(s + 1)
        finish(s)


def _mix_merge(x, oa, qkv_b, qkv_c, bias, sinks, w_gate, b_gate, w_a, w_b, w_c, w_out, g, b,
               batch, seq, layer):
    n = x.shape[0]
    assert A_V == B_W == C_Q == MERGE_WR
    tm = MIX_SUB * ATT_R
    nstep = seq // tm
    nblk = seq // ATT_R
    kper = ATT_R // C_KR
    kcol, vcol = C_Q // C_KV, C_Q // C_KV + 1
    row = lambda w: pl.BlockSpec((tm, w), lambda bb, i: (bb * nstep + i, 0))
    const = lambda *shape: pl.BlockSpec(shape, lambda bb, i: (0,) * len(shape))
    hbm = pl.BlockSpec(memory_space=pl.ANY)

    def kv_b(j, colblk):
        return pl.BlockSpec(
            (ATT_R, B_W),
            lambda bb, i: (bb * nblk + jnp.maximum(i * MIX_SUB + j - (B_KBLK - 1), 0), colblk))

    def kv_c(j, colblk):
        return pl.BlockSpec(
            (C_KR, C_KV),
            lambda bb, i: (bb * nblk * kper + jnp.maximum(i * MIX_SUB * kper + j - 1, 0), colblk))

    nkb, nkc = MIX_SUB + B_KBLK - 1, kper * MIX_SUB + C_KBLK - 2
    in_specs = ([pl.BlockSpec(memory_space=pltpu.SMEM), row(D_MODEL), row(A_V), row(B_W)]
                + [kv_b(j, 1) for j in range(nkb)] + [kv_b(j, 2) for j in range(nkb)]
                + [pl.BlockSpec((H_B, 2, CHUNK, B_STRIP_W), lambda bb, i: (layer, 0, 0, 0)), row(C_Q)]
                + [kv_c(j, kcol) for j in range(nkc)] + [kv_c(j, vcol) for j in range(nkc)]
                + [const(H_C, C_KR, 2 * C_KR),
                   hbm, const(1, N_BRANCH * D_MODEL), hbm, hbm, hbm, hbm,
                   const(1, D_MODEL), const(1, D_MODEL)])
    return pl.pallas_call(
        functools.partial(_mix_merge_kernel, layer),
        grid=(batch, nstep),
        in_specs=in_specs,
        out_specs=row(D_MODEL),
        out_shape=jax.ShapeDtypeStruct((n, D_MODEL), F32),
        scratch_shapes=[pltpu.VMEM((D_MODEL, N_BRANCH * D_MODEL), BF16),
                        pltpu.VMEM((A_V, D_MODEL), BF16), pltpu.VMEM((B_W, D_MODEL), BF16),
                        pltpu.VMEM((C_Q, D_MODEL), BF16), pltpu.VMEM((D_MODEL, D_MODEL), BF16),
                        pltpu.VMEM((2, MERGE_WR, D_MODEL), F32), pltpu.SemaphoreType.DMA((2,)),
                        pltpu.VMEM((tm, B_W), BF16), pltpu.VMEM((tm, C_Q), BF16)],
        compiler_params=_params("arbitrary", "arbitrary"),
        name="mix_merge",
    )(sinks, x, oa, qkv_b, *([qkv_b] * (2 * nkb)), bias,
      qkv_c, *([qkv_c] * (2 * nkc)), _attn_c_table(),
      w_gate, b_gate, w_a, w_b, w_c, w_out, g, b)


def _lane_row(vals, offset):
    return jnp.pad(vals.astype(F32), (offset, LANES - offset - vals.shape[0])).reshape(1, LANES)


def kernel(x, ln1_g, ln1_b, w_ff1_in, w_ff1_out, w_in, b_in, conv_w, a_log, dt_bias, gdn_norm_g,
           rel_bias, sinks, w_gate, b_gate, w_br_a, w_br_b, w_br_c, w_out, ln2_g, ln2_b,
           w_ff2_in, w_ff2_out, ln3_g, ln3_b):
    batch, seq, d = x.shape
    assert d == D_MODEL and (batch * seq) % FFN_TM == 0
    assert seq % (PG_SUB * GDN_R) == 0 and seq % (MIX_SUB * ATT_R) == 0
    h = x.reshape(batch * seq, d)
    row = lambda v: v.reshape(1, -1).astype(F32)
    o_small = QKV_A
    o_z = o_small + 2 * H_A
    o_b = o_z + A_V
    o_c = o_b + 3 * B_W
    bias_strips = _attn_b_bias(rel_bias.reshape(DEPTH * H_B, CHUNK + REL_CLIP))
    for l in range(DEPTH):
        h = _ffn(h, w_ff1_in, w_ff1_out, row(ln1_g[l]), row(ln1_b[l]), l)
        wi, bi = w_in[l], b_in[l]
        pad = SMALL_W - 2 * H_A
        wa = jnp.concatenate([wi[:, 0:QKV_A], wi[:, o_z:o_b]], axis=1).astype(BF16)
        ws = jnp.pad(wi[:, o_small:o_z], ((0, 0), (0, pad))).astype(BF16)
        ba = jnp.concatenate([bi[0:QKV_A], bi[o_z:o_b]])
        bs = jnp.pad(bi[o_small:o_z], (0, pad))
        o_a, pb, pc = _proj_gdn(h, wa, ws, wi[:, o_b:o_c].astype(BF16), wi[:, o_c:].astype(BF16),
                                row(ba), row(bs), row(bi[o_b:o_c]), row(bi[o_c:]),
                                conv_w[l].astype(F32), _lane_row(a_log[l], H_A),
                                _lane_row(dt_bias[l], H_A), row(gdn_norm_g[l]), batch, seq)
        h = _mix_merge(h, o_a, pb, pc, bias_strips, sinks[l].astype(F32), w_gate, row(b_gate[l]),
                       w_br_a, w_br_b, w_br_c, w_out, row(ln2_g[l]), row(ln2_b[l]), batch, seq, l)
        h = _ffn(h, w_ff2_in, w_ff2_out, row(ln3_g[l]), row(ln3_b[l]), l)
    return h.reshape(batch, seq, d)
```

```python
import functools
import math

import jax
import jax.numpy as jnp
import numpy as np
from jax import lax
from jax.experimental import pallas as pl
from jax.experimental.pallas import tpu as pltpu

F32 = jnp.float32
BF16 = jnp.bfloat16

D_MODEL = 1024
DEPTH = 2
CHUNK = 64
D_FF = 4096
LN_EPS = 1e-5
NORM_EPS = 1e-6
H_A, DK_A, DV_A, CONV_K = 4, 128, 128, 4
A_QK = H_A * DK_A
A_V = H_A * DV_A
H_B, D_B, B_PREV, REL_CLIP = 4, 128, 8, 128
B_W = H_B * D_B
H_C, HKV_C, D_C, WINDOW = 8, 2, 64, 128
G_C = H_C // HKV_C
C_PREV = WINDOW // CHUNK
C_Q = H_C * D_C
C_KV = HKV_C * D_C
N_BRANCH = 3
ALPHA = (2.0 * DEPTH) ** 0.25

LANES = 128
VMEM_LIMIT = 56 * 1024 * 1024
NEG_BIG = -1e30
LOG2E = math.log2(math.e)

SMALL_W = LANES
QKV_A = 2 * A_QK + A_V
A_W = QKV_A + A_V
C_W = C_Q + 2 * C_KV


def _dot(a, b):
    return jnp.dot(a, b, preferred_element_type=F32)


def _dot_nt(a, b):
    return lax.dot_general(a, b, (((1,), (1,)), ((), ())), preferred_element_type=F32)


def _layer_norm(y, g, b):
    mu = jnp.mean(y, axis=-1, keepdims=True)
    yc = y - mu
    var = jnp.mean(yc * yc, axis=-1, keepdims=True)
    return yc * lax.rsqrt(var + LN_EPS) * g + b


def _silu_of_half(xh):
    return xh + xh * jnp.tanh(xh)


def _silu(x):
    return _silu_of_half(0.5 * x)


def _params(*sem):
    return pltpu.CompilerParams(dimension_semantics=sem, vmem_limit_bytes=VMEM_LIMIT)


FFN_TM = 512
FFN_SUB = 256
FFN_TF = 1024
FFN_WCH = 1024


def _stage_cast(pairs, stage, sem):
    copies = [pltpu.make_async_copy(src, stage.at[c % 2], sem.at[c % 2])
              for c, (src, _) in enumerate(pairs)]
    copies[0].start()
    for c, (_, dst) in enumerate(pairs):
        if c + 1 < len(pairs):
            copies[c + 1].start()
        copies[c].wait()
        dst[...] = stage[c % 2].astype(BF16)


def _ffn_kernel(layer, x_ref, wi_hbm, wo_hbm, g_ref, b_ref, o_ref, wi_ref, wo_ref, h_ref, stage, sem):
    @pl.when(pl.program_id(0) == 0)
    def _():
        w = FFN_WCH
        pairs = [(wi_hbm.at[layer, :, pl.ds(c * w, w)], wi_ref.at[:, pl.ds(c * w, w)])
                 for c in range(2 * D_FF // w)]
        pairs += [(wo_hbm.at[layer, pl.ds(c * w, w), :], wo_ref.at[pl.ds(c * w, w), :])
                  for c in range(D_FF // w)]
        _stage_cast(pairs, stage, sem)

    for s in range(FFN_TM // FFN_SUB):
        rows = slice(s * FFN_SUB, (s + 1) * FFN_SUB)
        x = x_ref[rows, :]
        xb = x.astype(BF16)
        for f in range(D_FF // FFN_TF):
            gate = _dot(xb, wi_ref[:, f * FFN_TF:(f + 1) * FFN_TF])
            up = _dot(xb, wi_ref[:, D_FF + f * FFN_TF:D_FF + (f + 1) * FFN_TF])
            h_ref[rows, f * FFN_TF:(f + 1) * FFN_TF] = (_silu(gate) * up).astype(BF16)
        y = ALPHA * x + 0.5 * _dot(h_ref[rows, :], wo_ref[...])
        o_ref[rows, :] = _layer_norm(y, g_ref[...], b_ref[...])


def _ffn(x, w_in, w_out, g, b, layer):
    n = x.shape[0]
    resident = lambda r, w: pl.BlockSpec((r, w), lambda i: (0, 0), pipeline_mode=pl.Buffered(1))
    return pl.pallas_call(
        functools.partial(_ffn_kernel, layer),
        grid=(n // FFN_TM,),
        in_specs=[
            pl.BlockSpec((FFN_TM, D_MODEL), lambda i: (i, 0)),
            pl.BlockSpec(memory_space=pl.ANY), pl.BlockSpec(memory_space=pl.ANY),
            resident(1, D_MODEL), resident(1, D_MODEL),
        ],
        out_specs=pl.BlockSpec((FFN_TM, D_MODEL), lambda i: (i, 0)),
        out_shape=jax.ShapeDtypeStruct((n, D_MODEL), F32),
        scratch_shapes=[pltpu.VMEM((D_MODEL, 2 * D_FF), BF16), pltpu.VMEM((D_FF, D_MODEL), BF16),
                        pltpu.VMEM((FFN_TM, D_FF), BF16),
                        pltpu.VMEM((2, FFN_WCH, FFN_WCH), F32), pltpu.SemaphoreType.DMA((2,))],
        compiler_params=_params("arbitrary"),
        name="ffn_ln",
    )(x, w_in, w_out, g, b)


GDN_R = 256
PG_SUB = 2
GDN_NC = GDN_R // CHUNK
GDN_PAIR = 2 * CHUNK
CONV_PAD = 8
LOG2_CHUNK = CHUNK.bit_length() - 1
NEUMANN_STEPS = LOG2_CHUNK - 1


def _proj_gdn_kernel(x_ref, wa_ref, ws_ref, wb_ref, wc_ref, ba_ref, bs_ref, bb_ref, bc_ref,
                     cw_ref, alog_ref, dtb_ref, ng_ref, o_ref, ob_ref, oc_ref,
                     a_ref, s_ref, xbuf, st_ref):
    @pl.when(pl.program_id(1) == 0)
    def _():
        st_ref[...] = jnp.zeros_like(st_ref)
        xbuf[...] = jnp.zeros_like(xbuf)

    def project(sb):
        rows = pl.ds(sb * GDN_R, GDN_R)
        _proj_block(x_ref.at[rows], wa_ref, ws_ref, wb_ref, wc_ref, ba_ref, bs_ref, bb_ref, bc_ref,
                    ob_ref.at[rows], oc_ref.at[rows], a_ref.at[rows], s_ref.at[rows])

    project(0)
    for sb in range(PG_SUB):
        if sb + 1 < PG_SUB:
            project(sb + 1)
        rows = pl.ds(sb * GDN_R, GDN_R)
        hist_ref = xbuf if sb == 0 else a_ref.at[pl.ds(sb * GDN_R - CONV_PAD, CONV_PAD), pl.ds(0, QKV_A)]
        _gdn_block(cw_ref, alog_ref, dtb_ref, ng_ref, o_ref.at[rows], a_ref.at[rows], s_ref.at[rows],
                   hist_ref, xbuf, st_ref)


def _proj_block(x_ref, wa_ref, ws_ref, wb_ref, wc_ref, ba_ref, bs_ref, bb_ref, bc_ref,
                ob_ref, oc_ref, a_ref, s_ref):
    xb = x_ref[...].astype(BF16)
    a_ref[...] = _dot(xb, wa_ref[...]) + ba_ref[...]
    s_ref[...] = _dot(xb, ws_ref[...]) + bs_ref[...]
    yb = _dot(xb, wb_ref[...]) + bb_ref[...]
    ob_ref[:, 0:B_W] = (yb[:, 0:B_W] * (D_B ** -0.5 * LOG2E)).astype(BF16)
    ob_ref[:, B_W:3 * B_W] = yb[:, B_W:3 * B_W].astype(BF16)
    yc = _dot(xb, wc_ref[...]) + bc_ref[...]
    oc_ref[:, 0:C_Q] = (yc[:, 0:C_Q] * (D_C ** -0.5 * LOG2E)).astype(BF16)
    oc_ref[:, C_Q:C_W] = yc[:, C_Q:C_W].astype(BF16)


def _gdn_block(cw_ref, alog_ref, dtb_ref, ng_ref, o_ref, a_ref, s_ref, hist_ref, xbuf, st_ref):
    assert CONV_K == 4
    x_new = a_ref[:, 0:QKV_A]
    n_grp = GDN_R // CONV_PAD
    x3 = x_new.reshape(n_grp, CONV_PAD, QKV_A)
    hist = hist_ref[...].reshape(1, CONV_PAD, QKV_A)
    sub = lax.broadcasted_iota(jnp.int32, (n_grp, CONV_PAD, QKV_A), 1)
    w0, w1, w2, w3 = [0.5 * cw_ref[j:j + 1, :].reshape(1, 1, QKV_A) for j in range(CONV_K)]

    def delay(v, v_hist, s):
        rot = pltpu.roll(v, s, axis=1)
        prev = jnp.concatenate([pltpu.roll(v_hist, s, axis=1), rot[0:n_grp - 1]], axis=0)
        return jnp.where(sub >= s, rot, prev)

    dx = delay(x3, hist, 1)
    inner = w1 * x3 + w0 * dx
    inner_hist = w1 * hist + w0 * pltpu.roll(hist, 1, axis=1)
    y_half = w3 * x3 + w2 * dx + delay(inner, inner_hist, 2)
    xbuf[...] = x_new[GDN_R - CONV_PAD:GDN_R]
    qkv = _silu_of_half(y_half.reshape(GDN_R, QKV_A))

    s = s_ref[...]
    beta_all = jax.nn.sigmoid(s)
    g_all = -jnp.exp(alog_ref[...]) * jax.nn.softplus(s + dtb_ref[...])
    row = lax.broadcasted_iota(jnp.int32, (GDN_PAIR, GDN_PAIR), 0)
    col = lax.broadcasted_iota(jnp.int32, (GDN_PAIR, GDN_PAIR), 1)
    same_chunk = jnp.right_shift(row, LOG2_CHUNK) == jnp.right_shift(col, LOG2_CHUNK)
    causal = same_chunk & (col <= row)
    strict = same_chunk & (col < row)
    eye = jnp.where(row == col, 1.0, 0.0).astype(F32)
    pairs = [slice(p * GDN_PAIR, (p + 1) * GDN_PAIR) for p in range(GDN_R // GDN_PAIR)]
    g_hi = g_all.astype(BF16)
    g_r1 = g_all - g_hi.astype(F32)
    g_mid = g_r1.astype(BF16)
    g_lo = (g_r1 - g_mid.astype(F32)).astype(BF16)
    g_split = jnp.concatenate([g_hi, g_mid, g_lo], axis=1)
    tri = jnp.where(causal, 1.0, 0.0).astype(BF16)
    g_parts = jnp.concatenate([_dot(tri, g_split[rows]) for rows in pairs], axis=0)
    gcum = g_parts[:, 0:LANES] + g_parts[:, LANES:2 * LANES] + g_parts[:, 2 * LANES:3 * LANES]
    gsum = jnp.concatenate(
        [jnp.broadcast_to(gcum[(c + 1) * CHUNK - 1:(c + 1) * CHUNK, :], (CHUNK, LANES))
         for c in range(GDN_NC)], axis=0)
    gcum_t = gcum.T
    ng = ng_ref[...]
    heads = range(H_A)

    jobs = [(h, pr) for h in heads for pr in range(len(pairs))]
    qn, rhs, glast, kdec = [], [], [], []
    kk, qk, decay = {}, {}, {}
    for h in heads:
        lo = h * DK_A
        qh = qkv[:, lo:lo + DK_A]
        kh = qkv[:, A_QK + lo:A_QK + lo + DK_A]
        vh = qkv[:, 2 * A_QK + lo:2 * A_QK + lo + DV_A]
        q_h = qh * (lax.rsqrt(jnp.sum(qh * qh, -1, keepdims=True) + NORM_EPS) * (DK_A ** -0.5))
        k_h = kh * lax.rsqrt(jnp.sum(kh * kh, -1, keepdims=True) + NORM_EPS)
        bcol = beta_all[:, h:h + 1]
        g_h = gcum[:, H_A + h:H_A + h + 1]
        gl_h = gsum[:, H_A + h:H_A + h + 1]
        eg = jnp.exp(g_h)
        kb = k_h * bcol
        k_b = k_h.astype(BF16)
        kb_b = kb.astype(BF16)
        q_b = q_h.astype(BF16)
        g_row = gcum_t[H_A + h:H_A + h + 1, :]
        for pr, rows in enumerate(pairs):
            job = (h, pr)
            kk[job] = _dot_nt(kb_b[rows], k_b[rows])
            qk[job] = _dot_nt(q_b[rows], k_b[rows])
            decay[job] = jnp.exp(jnp.where(causal, g_h[rows] - g_row[:, rows], -jnp.inf))
        rhs.append(jnp.concatenate([vh * bcol, kb * eg], axis=1).astype(BF16))
        qn.append(q_h * eg)
        kdec.append(k_h * jnp.exp(gl_h - g_h))
        glast.append(gl_h)

    p = {job: jnp.where(strict, -(kk[job] * decay[job]), 0.0) for job in jobs}
    inv = {job: eye + p[job] for job in jobs}
    for _ in range(NEUMANN_STEPS):
        pb = {job: p[job].astype(BF16) for job in jobs}
        p = {job: _dot(pb[job], pb[job]) for job in jobs}
        inv = {job: inv[job] + _dot(p[job].astype(BF16), inv[job].astype(BF16)) for job in jobs}
    uw_j = {(h, pr): _dot(inv[(h, pr)].astype(BF16), rhs[h][pairs[pr]]).astype(BF16)
            for h, pr in jobs}
    aw_j = {job: _dot((qk[job] * decay[job]).astype(BF16), uw_j[job]) for job in jobs}
    uw = [jnp.concatenate([uw_j[(h, pr)] for pr in range(len(pairs))], axis=0) for h in heads]
    aw = [jnp.concatenate([aw_j[(h, pr)] for pr in range(len(pairs))], axis=0) for h in heads]
    qt = [(qn[h] - aw[h][:, DV_A:DV_A + DK_A]).astype(BF16) for h in heads]

    bp = [[_dot(kdec[h][c * CHUNK:(c + 1) * CHUNK].T.astype(BF16), uw[h][c * CHUNK:(c + 1) * CHUNK])
           for c in range(GDN_NC)] for h in heads]
    state = [st_ref[h] for h in heads]
    for c in range(GDN_NC):
        r0 = c * CHUNK
        for h in heads:
            lo = h * DV_A
            s_b = state[h].astype(BF16)
            o = _dot(qt[h][r0:r0 + CHUNK], s_b) + aw[h][r0:r0 + CHUNK, 0:DV_A]
            decay_c = jnp.exp(glast[h][r0:r0 + 1])
            state[h] = (state[h] * decay_c + bp[h][c][:, 0:DV_A]
                        - _dot(bp[h][c][:, DV_A:DV_A + DK_A].astype(BF16), s_b))
            z = a_ref[r0:r0 + CHUNK, QKV_A + lo:QKV_A + lo + DV_A]
            on = o * lax.rsqrt(jnp.mean(o * o, -1, keepdims=True) + NORM_EPS) * ng * _silu(z)
            o_ref[r0:r0 + CHUNK, lo:lo + DV_A] = on.astype(BF16)
    for h in heads:
        st_ref[h] = state[h]


def _proj_gdn(x, wa, ws, wb, wc, ba, bs, bb, bc, conv_w, alog_row, dtb_row, norm_g, batch, seq):
    n = x.shape[0]
    tm = PG_SUB * GDN_R
    nblk = seq // tm
    widths = (A_W, SMALL_W, 3 * B_W, C_W)
    row = lambda w: pl.BlockSpec((tm, w), lambda b, i: (b * nblk + i, 0))
    const = lambda r, w: pl.BlockSpec((r, w), lambda b, i: (0, 0))
    return pl.pallas_call(
        _proj_gdn_kernel,
        grid=(batch, nblk),
        in_specs=[row(D_MODEL)] + [const(D_MODEL, w) for w in widths] + [const(1, w) for w in widths]
        + [const(CONV_K, QKV_A), const(1, SMALL_W), const(1, SMALL_W), const(1, DV_A)],
        out_specs=[row(A_V), row(3 * B_W), row(C_W)],
        out_shape=[jax.ShapeDtypeStruct((n, A_V), BF16), jax.ShapeDtypeStruct((n, 3 * B_W), BF16),
                   jax.ShapeDtypeStruct((n, C_W), BF16)],
        scratch_shapes=[pltpu.VMEM((tm, A_W), F32), pltpu.VMEM((tm, SMALL_W), F32),
                        pltpu.VMEM((CONV_PAD, QKV_A), F32), pltpu.VMEM((H_A, DK_A, DV_A), F32)],
        compiler_params=_params("parallel", "arbitrary"),
        name="proj_gdn",
    )(x, wa, ws, wb, wc, ba, bs, bb, bc, conv_w, alog_row, dtb_row, norm_g)


ATT_R = 256
ATT_CH = ATT_R // CHUNK
B_KBLK = 3
assert (B_KBLK - 1) * ATT_CH == B_PREV
B_STRIP_W = B_KBLK * ATT_R + (ATT_CH - 2) * CHUNK
B_QR = 2 * CHUNK
B_KW = (B_PREV + 2) * CHUNK
B_BIAS0 = (ATT_CH - 2) * CHUNK


def _attn_b_pieces():
    pieces = []
    for half in range(ATT_R // B_QR):
        lo, cut = half * B_QR, []
        while lo < half * B_QR + B_KW:
            hi = min((lo // ATT_R + 1) * ATT_R, half * B_QR + B_KW)
            cut.append((lo // ATT_R, lo % ATT_R, hi - lo))
            lo = hi
        pieces.append(cut)
    return pieces


B_JOBS = [(h, half) for h in range(H_B) for half in range(ATT_R // B_QR)]


def _attn_b_scores(i, q_ref, k_refs, bias_ref, row0):
    pieces = _attn_b_pieces()
    scores = []
    for h, half in B_JOBS:
        cols = slice(h * D_B, (h + 1) * D_B)
        q = q_ref[row0 + half * B_QR:row0 + (half + 1) * B_QR, cols]
        parts = []
        for j, r0, nr in pieces[half]:
            sj = _dot_nt(q, k_refs[j][r0:r0 + nr, cols])
            if j < B_KBLK - 1:
                sj = jnp.where(i + j >= B_KBLK - 1, sj, NEG_BIG)
            parts.append(sj)
        bias = jnp.concatenate([bias_ref[h, 1, :, B_BIAS0:B_BIAS0 + B_KW],
                                bias_ref[h, 0, :, B_BIAS0:B_BIAS0 + B_KW]], axis=0)
        scores.append(jnp.concatenate(parts, axis=1) + bias)
    return scores


def _attn_b_finish(scores, v_refs, o_ref, row0):
    pieces = _attn_b_pieces()
    probs, denoms = [], []
    for s in scores:
        p = jnp.exp2(s - jnp.max(s, axis=-1, keepdims=True))
        denoms.append(jnp.sum(p, axis=-1, keepdims=True))
        probs.append(p.astype(BF16))
    for (h, half), pb, denom in zip(B_JOBS, probs, denoms):
        cols = slice(h * D_B, (h + 1) * D_B)
        o, c0 = None, 0
        for j, r0, nr in pieces[half]:
            term = _dot(pb[:, c0:c0 + nr], v_refs[j][r0:r0 + nr, cols])
            o = term if o is None else o + term
            c0 += nr
        o_ref[row0 + half * B_QR:row0 + (half + 1) * B_QR, cols] = (o / denom).astype(BF16)


def _attn_b_bias(rel_bias):
    nh, nrel = rel_bias.shape
    assert nrel == CHUNK + REL_CLIP
    lead = (ATT_CH - 1) * CHUNK
    vis_end = lead + (B_PREV + 1) * CHUNK
    width = B_KBLK * ATT_R + lead
    period = 1 << (width + CHUNK - 1).bit_length()
    rb = rel_bias.astype(F32)
    far = jnp.broadcast_to(rb[:, -1:], (nh, period))
    f = jnp.concatenate([far[:, :vis_end - nrel], jnp.flip(rb, axis=1), far[:, :period - vis_end]], axis=1)
    g = jnp.concatenate([f[:, :1], jnp.flip(f[:, 1:], axis=1)], axis=1)
    skew = jnp.tile(g, (1, CHUNK + 1))[:, :CHUNK * (period + 1)].reshape(nh, CHUNK, period + 1)
    toep = LOG2E * jnp.flip(skew[:, :, period - width + 1:period + 1], axis=2)
    c = np.arange(width)
    strip = jnp.where(jnp.asarray((c >= lead) & (c < vis_end))[None, None, :], toep, NEG_BIG)
    return jnp.stack([strip[:, :, 0:B_STRIP_W], strip[:, :, CHUNK:CHUNK + B_STRIP_W]], axis=1)


C_KR = 2 * CHUNK
C_KBLK = 3


C_JOBS = [(h, half) for h in range(H_C) for half in range(ATT_R // C_KR)]


def _kv_cols(h):
    return slice((h // G_C) * D_C, (h // G_C + 1) * D_C)


def _attn_c_scores(i, q_ref, k_refs, tbl_ref, row0):
    scores = []
    for h, half in C_JOBS:
        q = q_ref[row0 + half * C_KR:row0 + (half + 1) * C_KR, h * D_C:(h + 1) * D_C]
        s_prev = _dot_nt(q, k_refs[half][:, _kv_cols(h)])
        if half == 0:
            s_prev = jnp.where(i > 0, s_prev, NEG_BIG)
        s_own = _dot_nt(q, k_refs[half + 1][:, _kv_cols(h)])
        scores.append(jnp.concatenate([s_prev, s_own], axis=1) + tbl_ref[h])
    return scores


def _attn_c_finish(scores, sink_ref, v_refs, o_ref, row0):
    probs, denoms = [], []
    for (h, half), s in zip(C_JOBS, scores):
        sink = sink_ref[h] * LOG2E
        mx = jnp.maximum(jnp.max(s, axis=-1, keepdims=True), sink)
        p = jnp.exp2(s - mx)
        denoms.append(jnp.sum(p, axis=-1, keepdims=True) + jnp.exp2(sink - mx))
        probs.append(p.astype(BF16))
    for (h, half), pb, denom in zip(C_JOBS, probs, denoms):
        o = (_dot(pb[:, 0:C_KR], v_refs[half][:, _kv_cols(h)])
             + _dot(pb[:, C_KR:2 * C_KR], v_refs[half + 1][:, _kv_cols(h)]))
        o_ref[row0 + half * C_KR:row0 + (half + 1) * C_KR, h * D_C:(h + 1) * D_C] = (
            o / denom).astype(BF16)


def _attn_c_table():
    tq = np.arange(C_KR)[:, None] + C_KR
    tk = np.arange(2 * C_KR)[None, :]
    delta = tq // CHUNK - tk // CHUNK
    vis = (delta >= 0) & (delta <= C_PREV)
    slopes = 2.0 ** (-8.0 * np.arange(1, H_C + 1) / H_C)
    alibi = -slopes[:, None, None] * np.abs(tq - tk)[None].astype(np.float64)
    tbl = np.where(vis[None], LOG2E * alibi, NEG_BIG)
    return jnp.asarray(tbl, F32)


MERGE_WR = 512
MIX_SUB = 2


def _mix_merge_kernel(layer, sink_ref, x_ref, oa_ref, qb_ref, *rest):
    kper = ATT_R // C_KR
    nkb, nkc = MIX_SUB + B_KBLK - 1, kper * MIX_SUB + C_KBLK - 2
    kb_refs, vb_refs = rest[0:nkb], rest[nkb:2 * nkb]
    bias_ref, qc_ref = rest[2 * nkb:2 * nkb + 2]
    rest = rest[2 * nkb + 2:]
    kc_refs, vc_refs = rest[0:nkc], rest[nkc:2 * nkc]
    (tbl_ref, wg_hbm, bg_ref, wa_hbm, wb_hbm, wc_hbm, wo_hbm, g_ref, b_ref, o_ref,
     wg_ref, wa_ref, wb_ref, wc_ref, wo_ref, stage, sem, ob_ref, oc_ref) = rest[2 * nkc:]
    i = pl.program_id(1)

    @pl.when((pl.program_id(0) == 0) & (i == 0))
    def _():
        r = MERGE_WR
        pairs = [(wg_hbm.at[layer, pl.ds(m * r, r), pl.ds(j * D_MODEL, D_MODEL)],
                  wg_ref.at[pl.ds(m * r, r), pl.ds(j * D_MODEL, D_MODEL)])
                 for m in range(D_MODEL // r) for j in range(N_BRANCH)]
        pairs += [(src.at[layer], dst) for src, dst in
                  ((wa_hbm, wa_ref), (wb_hbm, wb_ref), (wc_hbm, wc_ref))]
        pairs += [(wo_hbm.at[layer, pl.ds(m * r, r), :], wo_ref.at[pl.ds(m * r, r), :])
                  for m in range(D_MODEL // r)]
        _stage_cast(pairs, stage, sem)

    st = [dict() for _ in range(MIX_SUB)]

    def qk(s):
        blk = i * MIX_SUB + s
        st[s]["b"] = _attn_b_scores(blk, qb_ref, kb_refs[s:s + B_KBLK], bias_ref, s * ATT_R)
        st[s]["c"] = _attn_c_scores(blk, qc_ref, kc_refs[kper * s:kper * s + C_KBLK], tbl_ref,
                                    s * ATT_R)

    def gates(s):
        rows = slice(s * ATT_R, (s + 1) * ATT_R)
        x = x_ref[rows, :]
        xb = x.astype(BF16)
        st[s]["x"] = x
        st[s]["g"] = [jax.nn.sigmoid(_dot(xb, wg_ref[:, r * D_MODEL:(r + 1) * D_MODEL])
                                     + bg_ref[:, r * D_MODEL:(r + 1) * D_MODEL])
                      for r in range(N_BRANCH)]
        st[s]["m"] = st[s]["g"][0] * _dot(oa_ref[rows, :], wa_ref[...])

    def attend(s):
        _attn_b_finish(st[s]["b"], vb_refs[s:s + B_KBLK], ob_ref, s * ATT_R)
        _attn_c_finish(st[s]["c"], sink_ref, vc_refs[kper * s:kper * s + C_KBLK], oc_ref, s * ATT_R)

    def finish(s):
        rows = slice(s * ATT_R, (s + 1) * ATT_R)
        merged = st[s]["m"] + st[s]["g"][1] * _dot(ob_ref[rows, :], wb_ref[...])
        merged = merged + st[s]["g"][2] * _dot(oc_ref[rows, :], wc_ref[...])
        y = ALPHA * st[s]["x"] + _dot(merged.astype(BF16), wo_ref[...])
        o_ref[rows, :] = _layer_norm(y, g_ref[...], b_ref[...])

    qk(0)
    gates(0)
    for s in range(MIX_SUB):
        if s + 1 < MIX_SUB:
            qk(s + 1)
        attend(s)
        if s + 1 < MIX_SUB:
            gates(s + 1)
        finish(s)


def _mix_merge(x, oa, qkv_b, qkv_c, bias, sinks, w_gate, b_gate, w_a, w_b, w_c, w_out, g, b,
               batch, seq, layer):
    n = x.shape[0]
    assert A_V == B_W == C_Q == MERGE_WR
    tm = MIX_SUB * ATT_R
    nstep = seq // tm
    nblk = seq // ATT_R
    kper = ATT_R // C_KR
    kcol, vcol = C_Q // C_KV, C_Q // C_KV + 1
    row = lambda w: pl.BlockSpec((tm, w), lambda bb, i: (bb * nstep + i, 0))
    const = lambda *shape: pl.BlockSpec(shape, lambda bb, i: (0,) * len(shape))
    hbm = pl.BlockSpec(memory_space=pl.ANY)

    def kv_b(j, colblk):
        return pl.BlockSpec(
            (ATT_R, B_W),
            lambda bb, i: (bb * nblk + jnp.maximum(i * MIX_SUB + j - (B_KBLK - 1), 0), colblk))

    def kv_c(j, colblk):
        return pl.BlockSpec(
            (C_KR, C_KV),
            lambda bb, i: (bb * nblk * kper + jnp.maximum(i * MIX_SUB * kper + j - 1, 0), colblk))

    nkb, nkc = MIX_SUB + B_KBLK - 1, kper * MIX_SUB + C_KBLK - 2
    in_specs = ([pl.BlockSpec(memory_space=pltpu.SMEM), row(D_MODEL), row(A_V), row(B_W)]
                + [kv_b(j, 1) for j in range(nkb)] + [kv_b(j, 2) for j in range(nkb)]
                + [pl.BlockSpec((H_B, 2, CHUNK, B_STRIP_W), lambda bb, i: (layer, 0, 0, 0)), row(C_Q)]
                + [kv_c(j, kcol) for j in range(nkc)] + [kv_c(j, vcol) for j in range(nkc)]
                + [const(H_C, C_KR, 2 * C_KR),
                   hbm, const(1, N_BRANCH * D_MODEL), hbm, hbm, hbm, hbm,
                   const(1, D_MODEL), const(1, D_MODEL)])
    return pl.pallas_call(
        functools.partial(_mix_merge_kernel, layer),
        grid=(batch, nstep),
        in_specs=in_specs,
        out_specs=row(D_MODEL),
        out_shape=jax.ShapeDtypeStruct((n, D_MODEL), F32),
        scratch_shapes=[pltpu.VMEM((D_MODEL, N_BRANCH * D_MODEL), BF16),
                        pltpu.VMEM((A_V, D_MODEL), BF16), pltpu.VMEM((B_W, D_MODEL), BF16),
                        pltpu.VMEM((C_Q, D_MODEL), BF16), pltpu.VMEM((D_MODEL, D_MODEL), BF16),
                        pltpu.VMEM((2, MERGE_WR, D_MODEL), F32), pltpu.SemaphoreType.DMA((2,)),
                        pltpu.VMEM((tm, B_W), BF16), pltpu.VMEM((tm, C_Q), BF16)],
        compiler_params=_params("arbitrary", "arbitrary"),
        name="mix_merge",
    )(sinks, x, oa, qkv_b, *([qkv_b] * (2 * nkb)), bias,
      qkv_c, *([qkv_c] * (2 * nkc)), _attn_c_table(),
      w_gate, b_gate, w_a, w_b, w_c, w_out, g, b)


def _lane_row(vals, offset):
    return jnp.pad(vals.astype(F32), (offset, LANES - offset - vals.shape[0])).reshape(1, LANES)


def kernel(x, ln1_g, ln1_b, w_ff1_in, w_ff1_out, w_in, b_in, conv_w, a_log, dt_bias, gdn_norm_g,
           rel_bias, sinks, w_gate, b_gate, w_br_a, w_br_b, w_br_c, w_out, ln2_g, ln2_b,
           w_ff2_in, w_ff2_out, ln3_g, ln3_b):
    batch, seq, d = x.shape
    assert d == D_MODEL and (batch * seq) % FFN_TM == 0
    assert seq % (PG_SUB * GDN_R) == 0 and seq % (MIX_SUB * ATT_R) == 0
    h = x.reshape(batch * seq, d)
    row = lambda v: v.reshape(1, -1).astype(F32)
    o_small = QKV_A
    o_z = o_small + 2 * H_A
    o_b = o_z + A_V
    o_c = o_b + 3 * B_W
    bias_strips = _attn_b_bias(rel_bias.reshape(DEPTH * H_B, CHUNK + REL_CLIP))
    for l in range(DEPTH):
        h = _ffn(h, w_ff1_in, w_ff1_out, row(ln1_g[l]), row(ln1_b[l]), l)
        wi, bi = w_in[l], b_in[l]
        pad = SMALL_W - 2 * H_A
        wa = jnp.concatenate([wi[:, 0:QKV_A], wi[:, o_z:o_b]], axis=1).astype(BF16)
        ws = jnp.pad(wi[:, o_small:o_z], ((0, 0), (0, pad))).astype(BF16)
        ba = jnp.concatenate([bi[0:QKV_A], bi[o_z:o_b]])
        bs = jnp.pad(bi[o_small:o_z], (0, pad))
        o_a, pb, pc = _proj_gdn(h, wa, ws, wi[:, o_b:o_c].astype(BF16), wi[:, o_c:].astype(BF16),
                                row(ba), row(bs), row(bi[o_b:o_c]), row(bi[o_c:]),
                                conv_w[l].astype(F32), _lane_row(a_log[l], H_A),
                                _lane_row(dt_bias[l], H_A), row(gdn_norm_g[l]), batch, seq)
        h = _mix_merge(h, o_a, pb, pc, bias_strips, sinks[l].astype(F32), w_gate, row(b_gate[l]),
                       w_br_a, w_br_b, w_br_c, w_out, row(ln2_g[l]), row(ln2_b[l]), batch, seq, l)
        h = _ffn(h, w_ff2_in, w_ff2_out, row(ln3_g[l]), row(ln3_b[l]), l)
    return h.reshape(batch, seq, d)
```

```python
import functools
import math

import jax
import jax.numpy as jnp
import numpy as np
from jax import lax
from jax.experimental import pallas as pl
from jax.experimental.pallas import tpu as pltpu

F32 = jnp.float32
BF16 = jnp.bfloat16

D_MODEL = 1024
DEPTH = 2
CHUNK = 64
D_FF = 4096
LN_EPS = 1e-5
NORM_EPS = 1e-6
H_A, DK_A, DV_A, CONV_K = 4, 128, 128, 4
A_QK = H_A * DK_A
A_V = H_A * DV_A
H_B, D_B, B_PREV, REL_CLIP = 4, 128, 8, 128
B_W = H_B * D_B
H_C, HKV_C, D_C, WINDOW = 8, 2, 64, 128
G_C = H_C // HKV_C
C_PREV = WINDOW // CHUNK
C_Q = H_C * D_C
C_KV = HKV_C * D_C
N_BRANCH = 3
ALPHA = (2.0 * DEPTH) ** 0.25

LANES = 128
VMEM_LIMIT = 56 * 1024 * 1024
NEG_BIG = -1e30
LOG2E = math.log2(math.e)

SMALL_W = LANES
QKV_A = 2 * A_QK + A_V
A_W = QKV_A + A_V
C_W = C_Q + 2 * C_KV


def _dot(a, b):
    return jnp.dot(a, b, preferred_element_type=F32)


def _dot_nt(a, b):
    return lax.dot_general(a, b, (((1,), (1,)), ((), ())), preferred_element_type=F32)


def _layer_norm(y, g, b):
    mu = jnp.mean(y, axis=-1, keepdims=True)
    yc = y - mu
    var = jnp.mean(yc * yc, axis=-1, keepdims=True)
    return yc * lax.rsqrt(var + LN_EPS) * g + b


def _silu_of_half(xh):
    return xh + xh * jnp.tanh(xh)


def _silu(x):
    return _silu_of_half(0.5 * x)


def _params(*sem):
    return pltpu.CompilerParams(dimension_semantics=sem, vmem_limit_bytes=VMEM_LIMIT)


FFN_TM = 512
FFN_SUB = 256
FFN_TF = 1024
FFN_WCH = 1024


def _stage_cast(pairs, stage, sem):
    copies = [pltpu.make_async_copy(src, stage.at[c % 2], sem.at[c % 2])
              for c, (src, _) in enumerate(pairs)]
    copies[0].start()
    for c, (_, dst) in enumerate(pairs):
        if c + 1 < len(pairs):
            copies[c + 1].start()
        copies[c].wait()
        dst[...] = stage[c % 2].astype(BF16)


def _ffn_kernel(layer, x_ref, wi_hbm, wo_hbm, g_ref, b_ref, o_ref, wi_ref, wo_ref, h_ref, stage, sem):
    @pl.when(pl.program_id(0) == 0)
    def _():
        w = FFN_WCH
        pairs = [(wi_hbm.at[layer, :, pl.ds(c * w, w)], wi_ref.at[:, pl.ds(c * w, w)])
                 for c in range(2 * D_FF // w)]
        pairs += [(wo_hbm.at[layer, pl.ds(c * w, w), :], wo_ref.at[pl.ds(c * w, w), :])
                  for c in range(D_FF // w)]
        _stage_cast(pairs, stage, sem)

    for s in range(FFN_TM // FFN_SUB):
        rows = slice(s * FFN_SUB, (s + 1) * FFN_SUB)
        x = x_ref[rows, :]
        xb = x.astype(BF16)
        for f in range(D_FF // FFN_TF):
            gate = _dot(xb, wi_ref[:, f * FFN_TF:(f + 1) * FFN_TF])
            up = _dot(xb, wi_ref[:, D_FF + f * FFN_TF:D_FF + (f + 1) * FFN_TF])
            h_ref[rows, f * FFN_TF:(f + 1) * FFN_TF] = (_silu(gate) * up).astype(BF16)
        y = ALPHA * x + 0.5 * _dot(h_ref[rows, :], wo_ref[...])
        o_ref[rows, :] = _layer_norm(y, g_ref[...], b_ref[...])


def _ffn(x, w_in, w_out, g, b, layer):
    n = x.shape[0]
    resident = lambda r, w: pl.BlockSpec((r, w), lambda i: (0, 0), pipeline_mode=pl.Buffered(1))
    return pl.pallas_call(
        functools.partial(_ffn_kernel, layer),
        grid=(n // FFN_TM,),
        in_specs=[
            pl.BlockSpec((FFN_TM, D_MODEL), lambda i: (i, 0)),
            pl.BlockSpec(memory_space=pl.ANY), pl.BlockSpec(memory_space=pl.ANY),
            resident(1, D_MODEL), resident(1, D_MODEL),
        ],
        out_specs=pl.BlockSpec((FFN_TM, D_MODEL), lambda i: (i, 0)),
        out_shape=jax.ShapeDtypeStruct((n, D_MODEL), F32),
        scratch_shapes=[pltpu.VMEM((D_MODEL, 2 * D_FF), BF16), pltpu.VMEM((D_FF, D_MODEL), BF16),
                        pltpu.VMEM((FFN_TM, D_FF), BF16),
                        pltpu.VMEM((2, FFN_WCH, FFN_WCH), F32), pltpu.SemaphoreType.DMA((2,))],
        compiler_params=_params("arbitrary"),
        name="ffn_ln",
    )(x, w_in, w_out, g, b)


GDN_R = 512
PG_SUB = 2
GDN_NC = GDN_R // CHUNK
GDN_PAIR = 2 * CHUNK
CONV_PAD = 8
LOG2_CHUNK = CHUNK.bit_length() - 1
NEUMANN_STEPS = LOG2_CHUNK - 1


def _proj_gdn_kernel(x_ref, wa_ref, ws_ref, wb_ref, wc_ref, ba_ref, bs_ref, bb_ref, bc_ref,
                     cw_ref, alog_ref, dtb_ref, ng_ref, o_ref, ob_ref, oc_ref,
                     a_ref, s_ref, xbuf, st_ref):
    @pl.when(pl.program_id(1) == 0)
    def _():
        st_ref[...] = jnp.zeros_like(st_ref)
        xbuf[...] = jnp.zeros_like(xbuf)

    def project(sb):
        rows = pl.ds(sb * GDN_R, GDN_R)
        _proj_block(x_ref.at[rows], wa_ref, ws_ref, wb_ref, wc_ref, ba_ref, bs_ref, bb_ref, bc_ref,
                    ob_ref.at[rows], oc_ref.at[rows], a_ref.at[rows], s_ref.at[rows])

    project(0)
    for sb in range(PG_SUB):
        if sb + 1 < PG_SUB:
            project(sb + 1)
        rows = pl.ds(sb * GDN_R, GDN_R)
        hist_ref = xbuf if sb == 0 else a_ref.at[pl.ds(sb * GDN_R - CONV_PAD, CONV_PAD), pl.ds(0, QKV_A)]
        _gdn_block(cw_ref, alog_ref, dtb_ref, ng_ref, o_ref.at[rows], a_ref.at[rows], s_ref.at[rows],
                   hist_ref, xbuf, st_ref)


def _proj_block(x_ref, wa_ref, ws_ref, wb_ref, wc_ref, ba_ref, bs_ref, bb_ref, bc_ref,
                ob_ref, oc_ref, a_ref, s_ref):
    xb = x_ref[...].astype(BF16)
    a_ref[...] = _dot(xb, wa_ref[...]) + ba_ref[...]
    s_ref[...] = _dot(xb, ws_ref[...]) + bs_ref[...]
    yb = _dot(xb, wb_ref[...]) + bb_ref[...]
    ob_ref[:, 0:B_W] = (yb[:, 0:B_W] * (D_B ** -0.5 * LOG2E)).astype(BF16)
    ob_ref[:, B_W:3 * B_W] = yb[:, B_W:3 * B_W].astype(BF16)
    yc = _dot(xb, wc_ref[...]) + bc_ref[...]
    oc_ref[:, 0:C_Q] = (yc[:, 0:C_Q] * (D_C ** -0.5 * LOG2E)).astype(BF16)
    oc_ref[:, C_Q:C_W] = yc[:, C_Q:C_W].astype(BF16)


def _gdn_block(cw_ref, alog_ref, dtb_ref, ng_ref, o_ref, a_ref, s_ref, hist_ref, xbuf, st_ref):
    assert CONV_K == 4
    x_new = a_ref[:, 0:QKV_A]
    n_grp = GDN_R // CONV_PAD
    x3 = x_new.reshape(n_grp, CONV_PAD, QKV_A)
    hist = hist_ref[...].reshape(1, CONV_PAD, QKV_A)
    sub = lax.broadcasted_iota(jnp.int32, (n_grp, CONV_PAD, QKV_A), 1)
    w0, w1, w2, w3 = [0.5 * cw_ref[j:j + 1, :].reshape(1, 1, QKV_A) for j in range(CONV_K)]

    def delay(v, v_hist, s):
        rot = pltpu.roll(v, s, axis=1)
        prev = jnp.concatenate([pltpu.roll(v_hist, s, axis=1), rot[0:n_grp - 1]], axis=0)
        return jnp.where(sub >= s, rot, prev)

    dx = delay(x3, hist, 1)
    inner = w1 * x3 + w0 * dx
    inner_hist = w1 * hist + w0 * pltpu.roll(hist, 1, axis=1)
    y_half = w3 * x3 + w2 * dx + delay(inner, inner_hist, 2)
    xbuf[...] = x_new[GDN_R - CONV_PAD:GDN_R]
    qkv = _silu_of_half(y_half.reshape(GDN_R, QKV_A))

    s = s_ref[...]
    beta_all = jax.nn.sigmoid(s)
    g_all = -jnp.exp(alog_ref[...]) * jax.nn.softplus(s + dtb_ref[...])
    row = lax.broadcasted_iota(jnp.int32, (GDN_PAIR, GDN_PAIR), 0)
    col = lax.broadcasted_iota(jnp.int32, (GDN_PAIR, GDN_PAIR), 1)
    same_chunk = jnp.right_shift(row, LOG2_CHUNK) == jnp.right_shift(col, LOG2_CHUNK)
    causal = same_chunk & (col <= row)
    strict = same_chunk & (col < row)
    eye = jnp.where(row == col, 1.0, 0.0).astype(F32)
    pairs = [slice(p * GDN_PAIR, (p + 1) * GDN_PAIR) for p in range(GDN_R // GDN_PAIR)]
    g_hi = g_all.astype(BF16)
    g_r1 = g_all - g_hi.astype(F32)
    g_mid = g_r1.astype(BF16)
    g_lo = (g_r1 - g_mid.astype(F32)).astype(BF16)
    g_split = jnp.concatenate([g_hi, g_mid, g_lo], axis=1)
    tri = jnp.where(causal, 1.0, 0.0).astype(BF16)
    g_parts = jnp.concatenate([_dot(tri, g_split[rows]) for rows in pairs], axis=0)
    gcum = g_parts[:, 0:LANES] + g_parts[:, LANES:2 * LANES] + g_parts[:, 2 * LANES:3 * LANES]
    gsum = jnp.concatenate(
        [jnp.broadcast_to(gcum[(c + 1) * CHUNK - 1:(c + 1) * CHUNK, :], (CHUNK, LANES))
         for c in range(GDN_NC)], axis=0)
    gcum_t = gcum.T
    ng = ng_ref[...]
    heads = range(H_A)

    jobs = [(h, pr) for h in heads for pr in range(len(pairs))]
    qn, rhs, glast, kdec = [], [], [], []
    kk, qk, decay = {}, {}, {}
    for h in heads:
        lo = h * DK_A
        qh = qkv[:, lo:lo + DK_A]
        kh = qkv[:, A_QK + lo:A_QK + lo + DK_A]
        vh = qkv[:, 2 * A_QK + lo:2 * A_QK + lo + DV_A]
        q_h = qh * (lax.rsqrt(jnp.sum(qh * qh, -1, keepdims=True) + NORM_EPS) * (DK_A ** -0.5))
        k_h = kh * lax.rsqrt(jnp.sum(kh * kh, -1, keepdims=True) + NORM_EPS)
        bcol = beta_all[:, h:h + 1]
        g_h = gcum[:, H_A + h:H_A + h + 1]
        gl_h = gsum[:, H_A + h:H_A + h + 1]
        eg = jnp.exp(g_h)
        kb = k_h * bcol
        k_b = k_h.astype(BF16)
        kb_b = kb.astype(BF16)
        q_b = q_h.astype(BF16)
        g_row = gcum_t[H_A + h:H_A + h + 1, :]
        for pr, rows in enumerate(pairs):
            job = (h, pr)
            kk[job] = _dot_nt(kb_b[rows], k_b[rows])
            qk[job] = _dot_nt(q_b[rows], k_b[rows])
            decay[job] = jnp.exp(jnp.where(causal, g_h[rows] - g_row[:, rows], -jnp.inf))
        rhs.append(jnp.concatenate([vh * bcol, kb * eg], axis=1).astype(BF16))
        qn.append(q_h * eg)
        kdec.append(k_h * jnp.exp(gl_h - g_h))
        glast.append(gl_h)

    p = {job: jnp.where(strict, -(kk[job] * decay[job]), 0.0) for job in jobs}
    inv = {job: eye + p[job] for job in jobs}
    for _ in range(NEUMANN_STEPS):
        pb = {job: p[job].astype(BF16) for job in jobs}
        p = {job: _dot(pb[job], pb[job]) for job in jobs}
        inv = {job: inv[job] + _dot(p[job].astype(BF16), inv[job].astype(BF16)) for job in jobs}
    uw_j = {(h, pr): _dot(inv[(h, pr)].astype(BF16), rhs[h][pairs[pr]]).astype(BF16)
            for h, pr in jobs}
    aw_j = {job: _dot((qk[job] * decay[job]).astype(BF16), uw_j[job]) for job in jobs}
    uw = [jnp.concatenate([uw_j[(h, pr)] for pr in range(len(pairs))], axis=0) for h in heads]
    aw = [jnp.concatenate([aw_j[(h, pr)] for pr in range(len(pairs))], axis=0) for h in heads]
    qt = [(qn[h] - aw[h][:, DV_A:DV_A + DK_A]).astype(BF16) for h in heads]

    bp = [[_dot(kdec[h][c * CHUNK:(c + 1) * CHUNK].T.astype(BF16), uw[h][c * CHUNK:(c + 1) * CHUNK])
           for c in range(GDN_NC)] for h in heads]
    state = [st_ref[h] for h in heads]
    for c in range(GDN_NC):
        r0 = c * CHUNK
        for h in heads:
            lo = h * DV_A
            s_b = state[h].astype(BF16)
            o = _dot(qt[h][r0:r0 + CHUNK], s_b) + aw[h][r0:r0 + CHUNK, 0:DV_A]
            decay_c = jnp.exp(glast[h][r0:r0 + 1])
            state[h] = (state[h] * decay_c + bp[h][c][:, 0:DV_A]
                        - _dot(bp[h][c][:, DV_A:DV_A + DK_A].astype(BF16), s_b))
            z = a_ref[r0:r0 + CHUNK, QKV_A + lo:QKV_A + lo + DV_A]
            on = o * lax.rsqrt(jnp.mean(o * o, -1, keepdims=True) + NORM_EPS) * ng * _silu(z)
            o_ref[r0:r0 + CHUNK, lo:lo + DV_A] = on.astype(BF16)
    for h in heads:
        st_ref[h] = state[h]


def _proj_gdn(x, wa, ws, wb, wc, ba, bs, bb, bc, conv_w, alog_row, dtb_row, norm_g, batch, seq):
    n = x.shape[0]
    tm = PG_SUB * GDN_R
    nblk = seq // tm
    widths = (A_W, SMALL_W, 3 * B_W, C_W)
    row = lambda w: pl.BlockSpec((tm, w), lambda b, i: (b * nblk + i, 0))
    const = lambda r, w: pl.BlockSpec((r, w), lambda b, i: (0, 0), pipeline_mode=pl.Buffered(1))
    return pl.pallas_call(
        _proj_gdn_kernel,
        grid=(batch, nblk),
        in_specs=[row(D_MODEL)] + [const(D_MODEL, w) for w in widths] + [const(1, w) for w in widths]
        + [const(CONV_K, QKV_A), const(1, SMALL_W), const(1, SMALL_W), const(1, DV_A)],
        out_specs=[row(A_V), row(3 * B_W), row(C_W)],
        out_shape=[jax.ShapeDtypeStruct((n, A_V), BF16), jax.ShapeDtypeStruct((n, 3 * B_W), BF16),
                   jax.ShapeDtypeStruct((n, C_W), BF16)],
        scratch_shapes=[pltpu.VMEM((tm, A_W), F32), pltpu.VMEM((tm, SMALL_W), F32),
                        pltpu.VMEM((CONV_PAD, QKV_A), F32), pltpu.VMEM((H_A, DK_A, DV_A), F32)],
        compiler_params=_params("parallel", "arbitrary"),
        name="proj_gdn",
    )(x, wa, ws, wb, wc, ba, bs, bb, bc, conv_w, alog_row, dtb_row, norm_g)


ATT_R = 256
ATT_CH = ATT_R // CHUNK
B_KBLK = 3
assert (B_KBLK - 1) * ATT_CH == B_PREV
B_STRIP_W = B_KBLK * ATT_R + (ATT_CH - 2) * CHUNK
B_QR = 2 * CHUNK
B_KW = (B_PREV + 2) * CHUNK
B_BIAS0 = (ATT_CH - 2) * CHUNK


def _attn_b_pieces():
    pieces = []
    for half in range(ATT_R // B_QR):
        lo, cut = half * B_QR, []
        while lo < half * B_QR + B_KW:
            hi = min((lo // ATT_R + 1) * ATT_R, half * B_QR + B_KW)
            cut.append((lo // ATT_R, lo % ATT_R, hi - lo))
            lo = hi
        pieces.append(cut)
    return pieces


B_JOBS = [(h, half) for h in range(H_B) for half in range(ATT_R // B_QR)]


def _attn_b_scores(i, q_ref, k_refs, bias_ref, row0):
    pieces = _attn_b_pieces()
    scores = []
    for h, half in B_JOBS:
        cols = slice(h * D_B, (h + 1) * D_B)
        q = q_ref[row0 + half * B_QR:row0 + (half + 1) * B_QR, cols]
        parts = []
        for j, r0, nr in pieces[half]:
            sj = _dot_nt(q, k_refs[j][r0:r0 + nr, cols])
            if j < B_KBLK - 1:
                sj = jnp.where(i + j >= B_KBLK - 1, sj, NEG_BIG)
            parts.append(sj)
        bias = jnp.concatenate([bias_ref[h, 1, :, B_BIAS0:B_BIAS0 + B_KW],
                                bias_ref[h, 0, :, B_BIAS0:B_BIAS0 + B_KW]], axis=0)
        scores.append(jnp.concatenate(parts, axis=1) + bias)
    return scores


def _attn_b_finish(scores, v_refs, o_ref, row0):
    pieces = _attn_b_pieces()
    probs, denoms = [], []
    for s in scores:
        p = jnp.exp2(s - jnp.max(s, axis=-1, keepdims=True))
        denoms.append(jnp.sum(p, axis=-1, keepdims=True))
        probs.append(p.astype(BF16))
    for (h, half), pb, denom in zip(B_JOBS, probs, denoms):
        cols = slice(h * D_B, (h + 1) * D_B)
        o, c0 = None, 0
        for j, r0, nr in pieces[half]:
            term = _dot(pb[:, c0:c0 + nr], v_refs[j][r0:r0 + nr, cols])
            o = term if o is None else o + term
            c0 += nr
        o_ref[row0 + half * B_QR:row0 + (half + 1) * B_QR, cols] = (o / denom).astype(BF16)


def _attn_b_bias(rel_bias):
    nh, nrel = rel_bias.shape
    assert nrel == CHUNK + REL_CLIP
    lead = (ATT_CH - 1) * CHUNK
    vis_end = lead + (B_PREV + 1) * CHUNK
    width = B_KBLK * ATT_R + lead
    period = 1 << (width + CHUNK - 1).bit_length()
    rb = rel_bias.astype(F32)
    far = jnp.broadcast_to(rb[:, -1:], (nh, period))
    f = jnp.concatenate([far[:, :vis_end - nrel], jnp.flip(rb, axis=1), far[:, :period - vis_end]], axis=1)
    g = jnp.concatenate([f[:, :1], jnp.flip(f[:, 1:], axis=1)], axis=1)
    skew = jnp.tile(g, (1, CHUNK + 1))[:, :CHUNK * (period + 1)].reshape(nh, CHUNK, period + 1)
    toep = LOG2E * jnp.flip(skew[:, :, period - width + 1:period + 1], axis=2)
    c = np.arange(width)
    strip = jnp.where(jnp.asarray((c >= lead) & (c < vis_end))[None, None, :], toep, NEG_BIG)
    return jnp.stack([strip[:, :, 0:B_STRIP_W], strip[:, :, CHUNK:CHUNK + B_STRIP_W]], axis=1)


C_KR = 2 * CHUNK
C_KBLK = 3


C_JOBS = [(h, half) for h in range(H_C) for half in range(ATT_R // C_KR)]


def _kv_cols(h):
    return slice((h // G_C) * D_C, (h // G_C + 1) * D_C)


def _attn_c_scores(i, q_ref, k_refs, tbl_ref, row0):
    scores = []
    for h, half in C_JOBS:
        q = q_ref[row0 + half * C_KR:row0 + (half + 1) * C_KR, h * D_C:(h + 1) * D_C]
        s_prev = _dot_nt(q, k_refs[half][:, _kv_cols(h)])
        if half == 0:
            s_prev = jnp.where(i > 0, s_prev, NEG_BIG)
        s_own = _dot_nt(q, k_refs[half + 1][:, _kv_cols(h)])
        scores.append(jnp.concatenate([s_prev, s_own], axis=1) + tbl_ref[h])
    return scores


def _attn_c_finish(scores, sink_ref, v_refs, o_ref, row0):
    probs, denoms = [], []
    for (h, half), s in zip(C_JOBS, scores):
        sink = sink_ref[h] * LOG2E
        mx = jnp.maximum(jnp.max(s, axis=-1, keepdims=True), sink)
        p = jnp.exp2(s - mx)
        denoms.append(jnp.sum(p, axis=-1, keepdims=True) + jnp.exp2(sink - mx))
        probs.append(p.astype(BF16))
    for (h, half), pb, denom in zip(C_JOBS, probs, denoms):
        o = (_dot(pb[:, 0:C_KR], v_refs[half][:, _kv_cols(h)])
             + _dot(pb[:, C_KR:2 * C_KR], v_refs[half + 1][:, _kv_cols(h)]))
        o_ref[row0 + half * C_KR:row0 + (half + 1) * C_KR, h * D_C:(h + 1) * D_C] = (
            o / denom).astype(BF16)


def _attn_c_table():
    tq = np.arange(C_KR)[:, None] + C_KR
    tk = np.arange(2 * C_KR)[None, :]
    delta = tq // CHUNK - tk // CHUNK
    vis = (delta >= 0) & (delta <= C_PREV)
    slopes = 2.0 ** (-8.0 * np.arange(1, H_C + 1) / H_C)
    alibi = -slopes[:, None, None] * np.abs(tq - tk)[None].astype(np.float64)
    tbl = np.where(vis[None], LOG2E * alibi, NEG_BIG)
    return jnp.asarray(tbl, F32)


MERGE_WR = 512
MIX_SUB = 2


def _mix_merge_kernel(layer, sink_ref, x_ref, oa_ref, qb_ref, *rest):
    kper = ATT_R // C_KR
    nkb, nkc = MIX_SUB + B_KBLK - 1, kper * MIX_SUB + C_KBLK - 2
    kb_refs, vb_refs = rest[0:nkb], rest[nkb:2 * nkb]
    bias_ref, qc_ref = rest[2 * nkb:2 * nkb + 2]
    rest = rest[2 * nkb + 2:]
    kc_refs, vc_refs = rest[0:nkc], rest[nkc:2 * nkc]
    (tbl_ref, wg_hbm, bg_ref, wa_hbm, wb_hbm, wc_hbm, wo_hbm, g_ref, b_ref, o_ref,
     wg_ref, wa_ref, wb_ref, wc_ref, wo_ref, stage, sem, ob_ref, oc_ref) = rest[2 * nkc:]
    i = pl.program_id(1)

    @pl.when((pl.program_id(0) == 0) & (i == 0))
    def _():
        r = MERGE_WR
        pairs = [(wg_hbm.at[layer, pl.ds(m * r, r), pl.ds(j * D_MODEL, D_MODEL)],
                  wg_ref.at[pl.ds(m * r, r), pl.ds(j * D_MODEL, D_MODEL)])
                 for m in range(D_MODEL // r) for j in range(N_BRANCH)]
        pairs += [(src.at[layer], dst) for src, dst in
                  ((wa_hbm, wa_ref), (wb_hbm, wb_ref), (wc_hbm, wc_ref))]
        pairs += [(wo_hbm.at[layer, pl.ds(m * r, r), :], wo_ref.at[pl.ds(m * r, r), :])
                  for m in range(D_MODEL // r)]
        _stage_cast(pairs, stage, sem)

    st = [dict() for _ in range(MIX_SUB)]

    def qk(s):
        blk = i * MIX_SUB + s
        st[s]["b"] = _attn_b_scores(blk, qb_ref, kb_refs[s:s + B_KBLK], bias_ref, s * ATT_R)
        st[s]["c"] = _attn_c_scores(blk, qc_ref, kc_refs[kper * s:kper * s + C_KBLK], tbl_ref,
                                    s * ATT_R)

    def gates(s):
        rows = slice(s * ATT_R, (s + 1) * ATT_R)
        x = x_ref[rows, :]
        xb = x.astype(BF16)
        st[s]["x"] = x
        st[s]["g"] = [jax.nn.sigmoid(_dot(xb, wg_ref[:, r * D_MODEL:(r + 1) * D_MODEL])
                                     + bg_ref[:, r * D_MODEL:(r + 1) * D_MODEL])
                      for r in range(N_BRANCH)]
        st[s]["m"] = st[s]["g"][0] * _dot(oa_ref[rows, :], wa_ref[...])

    def attend(s):
        _attn_b_finish(st[s]["b"], vb_refs[s:s + B_KBLK], ob_ref, s * ATT_R)
        _attn_c_finish(st[s]["c"], sink_ref, vc_refs[kper * s:kper * s + C_KBLK], oc_ref, s * ATT_R)

    def finish(s):
        rows = slice(s * ATT_R, (s + 1) * ATT_R)
        merged = st[s]["m"] + st[s]["g"][1] * _dot(ob_ref[rows, :], wb_ref[...])
        merged = merged + st[s]["g"][2] * _dot(oc_ref[rows, :], wc_ref[...])
        y = ALPHA * st[s]["x"] + _dot(merged.astype(BF16), wo_ref[...])
        o_ref[rows, :] = _layer_norm(y, g_ref[...], b_ref[...])

    qk(0)
    gates(0)
    for s in range(MIX_SUB):
        if s + 1 < MIX_SUB:
            qk(s + 1)
        attend(s)
        if s + 1 < MIX_SUB:
            gates(s + 1)
        finish(s)


def _mix_merge(x, oa, qkv_b, qkv_c, bias, sinks, w_gate, b_gate, w_a, w_b, w_c, w_out, g, b,
               batch, seq, layer):
    n = x.shape[0]
    assert A_V == B_W == C_Q == MERGE_WR
    tm = MIX_SUB * ATT_R
    nstep = seq // tm
    nblk = seq // ATT_R
    kper = ATT_R // C_KR
    kcol, vcol = C_Q // C_KV, C_Q // C_KV + 1
    row = lambda w: pl.BlockSpec((tm, w), lambda bb, i: (bb * nstep + i, 0))
    const = lambda *shape: pl.BlockSpec(shape, lambda bb, i: (0,) * len(shape))
    hbm = pl.BlockSpec(memory_space=pl.ANY)

    def kv_b(j, colblk):
        return pl.BlockSpec(
            (ATT_R, B_W),
            lambda bb, i: (bb * nblk + jnp.maximum(i * MIX_SUB + j - (B_KBLK - 1), 0), colblk))

    def kv_c(j, colblk):
        return pl.BlockSpec(
            (C_KR, C_KV),
            lambda bb, i: (bb * nblk * kper + jnp.maximum(i * MIX_SUB * kper + j - 1, 0), colblk))

    nkb, nkc = MIX_SUB + B_KBLK - 1, kper * MIX_SUB + C_KBLK - 2
    in_specs = ([pl.BlockSpec(memory_space=pltpu.SMEM), row(D_MODEL), row(A_V), row(B_W)]
                + [kv_b(j, 1) for j in range(nkb)] + [kv_b(j, 2) for j in range(nkb)]
                + [pl.BlockSpec((H_B, 2, CHUNK, B_STRIP_W), lambda bb, i: (layer, 0, 0, 0)), row(C_Q)]
                + [kv_c(j, kcol) for j in range(nkc)] + [kv_c(j, vcol) for j in range(nkc)]
                + [const(H_C, C_KR, 2 * C_KR),
                   hbm, const(1, N_BRANCH * D_MODEL), hbm, hbm, hbm, hbm,
                   const(1, D_MODEL), const(1, D_MODEL)])
    return pl.pallas_call(
        functools.partial(_mix_merge_kernel, layer),
        grid=(batch, nstep),
        in_specs=in_specs,
        out_specs=row(D_MODEL),
        out_shape=jax.ShapeDtypeStruct((n, D_MODEL), F32),
        scratch_shapes=[pltpu.VMEM((D_MODEL, N_BRANCH * D_MODEL), BF16),
                        pltpu.VMEM((A_V, D_MODEL), BF16), pltpu.VMEM((B_W, D_MODEL), BF16),
                        pltpu.VMEM((C_Q, D_MODEL), BF16), pltpu.VMEM((D_MODEL, D_MODEL), BF16),
                        pltpu.VMEM((2, MERGE_WR, D_MODEL), F32), pltpu.SemaphoreType.DMA((2,)),
                        pltpu.VMEM((tm, B_W), BF16), pltpu.VMEM((tm, C_Q), BF16)],
        compiler_params=_params("arbitrary", "arbitrary"),
        name="mix_merge",
    )(sinks, x, oa, qkv_b, *([qkv_b] * (2 * nkb)), bias,
      qkv_c, *([qkv_c] * (2 * nkc)), _attn_c_table(),
      w_gate, b_gate, w_a, w_b, w_c, w_out, g, b)


def _lane_row(vals, offset):
    return jnp.pad(vals.astype(F32), (offset, LANES - offset - vals.shape[0])).reshape(1, LANES)


def kernel(x, ln1_g, ln1_b, w_ff1_in, w_ff1_out, w_in, b_in, conv_w, a_log, dt_bias, gdn_norm_g,
           rel_bias, sinks, w_gate, b_gate, w_br_a, w_br_b, w_br_c, w_out, ln2_g, ln2_b,
           w_ff2_in, w_ff2_out, ln3_g, ln3_b):
    batch, seq, d = x.shape
    assert d == D_MODEL and (batch * seq) % FFN_TM == 0
    assert seq % (PG_SUB * GDN_R) == 0 and seq % (MIX_SUB * ATT_R) == 0
    h = x.reshape(batch * seq, d)
    row = lambda v: v.reshape(1, -1).astype(F32)
    o_small = QKV_A
    o_z = o_small + 2 * H_A
    o_b = o_z + A_V
    o_c = o_b + 3 * B_W
    bias_strips = _attn_b_bias(rel_bias.reshape(DEPTH * H_B, CHUNK + REL_CLIP))
    for l in range(DEPTH):
        h = _ffn(h, w_ff1_in, w_ff1_out, row(ln1_g[l]), row(ln1_b[l]), l)
        wi, bi = w_in[l], b_in[l]
        pad = SMALL_W - 2 * H_A
        wa = jnp.concatenate([wi[:, 0:QKV_A], wi[:, o_z:o_b]], axis=1).astype(BF16)
        ws = jnp.pad(wi[:, o_small:o_z], ((0, 0), (0, pad))).astype(BF16)
        ba = jnp.concatenate([bi[0:QKV_A], bi[o_z:o_b]])
        bs = jnp.pad(bi[o_small:o_z], (0, pad))
        o_a, pb, pc = _proj_gdn(h, wa, ws, wi[:, o_b:o_c].astype(BF16), wi[:, o_c:].astype(BF16),
                                row(ba), row(bs), row(bi[o_b:o_c]), row(bi[o_c:]),
                                conv_w[l].astype(F32), _lane_row(a_log[l], H_A),
                                _lane_row(dt_bias[l], H_A), row(gdn_norm_g[l]), batch, seq)
        h = _mix_merge(h, o_a, pb, pc, bias_strips, sinks[l].astype(F32), w_gate, row(b_gate[l]),
                       w_br_a, w_br_b, w_br_c, w_out, row(ln2_g[l]), row(ln2_b[l]), batch, seq, l)
        h = _ffn(h, w_ff2_in, w_ff2_out, row(ln3_g[l]), row(ln3_b[l]), l)
    return h.reshape(batch, seq, d)
```
